```python
import math
import jax, jax.numpy as jnp
from jax import lax
import numpy as np

D_MODEL = 2048
BATCH = 4
SEQ = 2048
DEPTH = 2

GRID_W = 64
CTX_LEN = 256
HEAD_DIM = 64
N_BRANCH = 4
BRANCH_W = D_MODEL // N_BRANCH
A_HEADS = BRANCH_W // HEAD_DIM
A_KV_HEADS = A_HEADS // 4
A_WINDOW = 128
A_BLOCK = 128
B_HEADS = BRANCH_W // HEAD_DIM
B_WIN_ROWS_MAX = 8
B_WIN_COLS = 16
B_QCOLS = 16
B_KCOLS = 32
C_WINDOWS = (2, 4, 8, 16)
C_GROUPS = len(C_WINDOWS)
C_GROUP_DIM = BRANCH_W // C_GROUPS
D_HEADS = BRANCH_W // (2 * HEAD_DIM)
D_BLOCK = 128
D_FF = 4 * D_MODEL
ROPE_BASE = 10000.0
EPS = 1e-6
NEG = -1e30

A_W = A_HEADS * HEAD_DIM
A_KV_W = A_KV_HEADS * HEAD_DIM
B_W = B_HEADS * HEAD_DIM
C_W = C_GROUPS * C_GROUP_DIM
D_W = D_HEADS * 2 * HEAD_DIM
SPLIT_SIZES = (A_W, A_KV_W, A_KV_W, B_W, B_W, B_W, C_W, D_W, D_W, D_W, N_BRANCH * D_MODEL)
SPLIT_POINTS = tuple(int(v) for v in np.cumsum(SPLIT_SIZES)[:-1])
PROJ_W = int(sum(SPLIT_SIZES))

kernel_name = "hybrid_gated_parallel_mixers_dit"

F32 = jnp.float32


def rms_norm(x, g):
    xf = x.astype(F32)
    y = xf * lax.rsqrt(jnp.mean(xf * xf, axis=-1, keepdims=True) + EPS)
    return (y * g.astype(F32)).astype(x.dtype)


def modulate(h, shift, scale):
    return h * (1 + scale) + shift


def axial_rope(n):
    t = jnp.arange(n)
    row = (t // GRID_W).astype(F32)
    col = (t % GRID_W).astype(F32)
    n_freq = HEAD_DIM // 4
    inv = ROPE_BASE ** (-jnp.arange(n_freq, dtype=F32) / n_freq)
    ang = jnp.concatenate([row[:, None] * inv, col[:, None] * inv], axis=-1)
    return jnp.cos(ang), jnp.sin(ang)


def rope2d(x, cos, sin):
    q = HEAD_DIM // 4
    xf = x.astype(F32)
    x1 = jnp.concatenate([xf[..., :q], xf[..., 2 * q:3 * q]], axis=-1)
    x2 = jnp.concatenate([xf[..., q:2 * q], xf[..., 3 * q:]], axis=-1)
    c = cos[:, None, :]
    s = sin[:, None, :]
    y1 = x1 * c - x2 * s
    y2 = x2 * c + x1 * s
    return jnp.concatenate([y1[..., :q], y2[..., :q], y1[..., q:], y2[..., q:]], axis=-1).astype(x.dtype)


def split_heads(z, qk_g, rope=None):
    B_, n, _ = z.shape
    aq, ak, av, bq, bk, bv, cu, dq, dk, dv, gz = jnp.split(z, SPLIT_POINTS, axis=-1)
    aq = rms_norm(aq.reshape(B_, n, A_HEADS, HEAD_DIM), qk_g[0, 0])
    ak = rms_norm(ak.reshape(B_, n, A_KV_HEADS, HEAD_DIM), qk_g[0, 1])
    bq = rms_norm(bq.reshape(B_, n, B_HEADS, HEAD_DIM), qk_g[1, 0])
    bk = rms_norm(bk.reshape(B_, n, B_HEADS, HEAD_DIM), qk_g[1, 1])
    dq = rms_norm(dq.reshape(B_, n, 2 * D_HEADS, HEAD_DIM), qk_g[2, 0])
    dk = rms_norm(dk.reshape(B_, n, 2 * D_HEADS, HEAD_DIM), qk_g[2, 1])
    if rope is not None:
        cos, sin = rope
        aq = rope2d(aq, cos, sin)
        ak = rope2d(ak, cos, sin)
        dq = rope2d(dq, cos, sin)
        dk = rope2d(dk, cos, sin)
    av = av.reshape(B_, n, A_KV_HEADS, HEAD_DIM)
    bv = bv.reshape(B_, n, B_HEADS, HEAD_DIM)
    dq = dq.reshape(B_, n, D_HEADS, 2, HEAD_DIM)
    dk = dk.reshape(B_, n, D_HEADS, 2, HEAD_DIM)
    dv = dv.reshape(B_, n, D_HEADS, 2 * HEAD_DIM)
    return (aq, ak, av, bq, bk, bv, cu, dq, dk, dv, gz)


def context_attn(q, k, v, sink=None):
    B_, nq, H, Dh = q.shape
    G = k.shape[2]
    R = H // G
    nk = k.shape[1]
    qg = q.reshape(B_, nq, G, R, Dh)
    s = jnp.einsum('bqgrd,bkgd->bgrqk', qg, k, preferred_element_type=F32) * Dh ** -0.5
    if sink is not None:
        s_sink = jnp.broadcast_to(sink.astype(F32).reshape(G, R)[None, :, :, None, None], s.shape[:-1] + (1,))
        s = jnp.concatenate([s, s_sink], axis=-1)
    p = jax.nn.softmax(s, axis=-1)[..., :nk]
    o = jnp.einsum('bgrqk,bkgd->bqgrd', p.astype(v.dtype), v)
    return o.reshape(B_, nq, H * Dh)


def window_gqa(q, k, v, kc, vc, sink):
    B_, S, _, _ = q.shape
    nb = S // A_BLOCK
    G, R = A_KV_HEADS, A_HEADS // A_KV_HEADS
    nc = kc.shape[1]
    qb = q.reshape(B_, nb, A_BLOCK, G, R, HEAD_DIM)

    def band(t):
        tb = t.reshape(B_, nb, A_BLOCK, G, HEAD_DIM)
        tp = jnp.pad(tb, ((0, 0), (1, 1), (0, 0), (0, 0), (0, 0)))
        return jnp.concatenate([tp[:, :-2], tp[:, 1:-1], tp[:, 2:]], axis=2)

    kb, vb = band(k), band(v)
    scale = HEAD_DIM ** -0.5
    s_lat = jnp.einsum('bnqgrd,bnkgd->bngrqk', qb, kb, preferred_element_type=F32) * scale
    qpos = jnp.arange(S).reshape(nb, A_BLOCK)
    kpos = (jnp.arange(nb)[:, None] - 1) * A_BLOCK + jnp.arange(3 * A_BLOCK)[None, :]
    valid = ((jnp.abs(qpos[:, :, None] - kpos[:, None, :]) <= A_WINDOW)
             & (kpos[:, None, :] >= 0) & (kpos[:, None, :] < S))
    s_lat = jnp.where(valid[None, :, None, None], s_lat, NEG)
    s_ctx = jnp.einsum('bnqgrd,bkgd->bngrqk', qb, kc, preferred_element_type=F32) * scale
    s_sink = jnp.broadcast_to(sink.astype(F32).reshape(G, R)[None, None, :, :, None, None], s_lat.shape[:-1] + (1,))
    p = jax.nn.softmax(jnp.concatenate([s_lat, s_ctx, s_sink], axis=-1), axis=-1).astype(v.dtype)
    L = 3 * A_BLOCK
    o = (jnp.einsum('bngrqk,bnkgd->bnqgrd', p[..., :L], vb)
         + jnp.einsum('bngrqk,bkgd->bnqgrd', p[..., L:L + nc], vc))
    return o.reshape(B_, S, A_W)


def neighbourhood_attn(q, k, v, kc, vc, rpb):
    B_, S, H, Dh = q.shape
    rows = S // GRID_W
    kh = min(B_WIN_ROWS_MAX, rows)
    ncb = GRID_W // B_QCOLS
    nc = kc.shape[1]
    r = jnp.arange(rows)
    rs = jnp.clip(r - kh // 2, 0, rows - kh)
    key_rows = rs[:, None] + jnp.arange(kh)[None, :]
    cs = jnp.clip(jnp.arange(GRID_W) - B_WIN_COLS // 2, 0, GRID_W - B_WIN_COLS)
    kb0 = jnp.clip(jnp.arange(ncb) * B_QCOLS - B_WIN_COLS // 2, 0, GRID_W - B_KCOLS)
    key_cols = kb0[:, None] + jnp.arange(B_KCOLS)[None, :]
    qcol = jnp.arange(ncb)[:, None] * B_QCOLS + jnp.arange(B_QCOLS)[None, :]

    def gather(t):
        tg = t.reshape(B_, rows, GRID_W, H, Dh)
        return tg[:, key_rows[:, :, None, None], key_cols[None, None, :, :]]

    kg, vg = gather(k), gather(v)
    qg = q.reshape(B_, rows, ncb, B_QCOLS, H, Dh)
    scale = Dh ** -0.5
    s = jnp.einsum('brjqhd,brkjchd->bhrjqkc', qg, kg, preferred_element_type=F32) * scale
    cs_q = cs[qcol]
    kcol = key_cols[:, None, :]
    valid = (kcol >= cs_q[:, :, None]) & (kcol < cs_q[:, :, None] + B_WIN_COLS)
    ri = key_rows - r[:, None] + (B_WIN_ROWS_MAX - 1)
    ci = jnp.clip(kcol - qcol[:, :, None] + (B_WIN_COLS - 1), 0, 2 * B_WIN_COLS - 2)
    bias = rpb[:, ri[:, None, None, :, None], ci[None, :, :, None, :]]
    s = s + bias.astype(F32)[None]
    s = jnp.where(valid[None, None, None, :, :, None, :], s, NEG)
    s = s.reshape(B_, H, rows, ncb, B_QCOLS, kh * B_KCOLS)
    s_ctx = jnp.einsum('brjqhd,bkhd->bhrjqk', qg, kc, preferred_element_type=F32) * scale
    p = jax.nn.softmax(jnp.concatenate([s, s_ctx], axis=-1), axis=-1).astype(v.dtype)
    nl = kh * B_KCOLS
    p_lat = p[..., :nl].reshape(B_, H, rows, ncb, B_QCOLS, kh, B_KCOLS)
    o = (jnp.einsum('bhrjqkc,brkjchd->brjqhd', p_lat, vg)
         + jnp.einsum('bhrjqk,bkhd->brjqhd', p[..., nl:nl + nc], vc))
    return o.reshape(B_, S, B_W)


def multiscale_pool(u):
    n = u.shape[1]
    uf = u.astype(F32)
    csum = jnp.pad(jnp.cumsum(uf, axis=1), ((0, 0), (1, 0), (0, 0)))
    t = jnp.arange(n)
    outs = []
    for g, w in enumerate(C_WINDOWS):
        lo = jnp.clip(t - w // 2, 0, n - 1)
        hi = jnp.clip(t + w - 1 - w // 2, 0, n - 1)
        cg = csum[..., g * C_GROUP_DIM:(g + 1) * C_GROUP_DIM]
        cnt = (hi - lo + 1).astype(F32)[None, :, None]
        outs.append((cg[:, hi + 1] - cg[:, lo]) / cnt)
    return (jnp.concatenate(outs, axis=-1) - uf).astype(u.dtype)


def pool_branch(u, w, scale):
    B_, n, _ = u.shape
    pooled = multiscale_pool(u).reshape(B_, n, C_GROUPS, C_GROUP_DIM)
    y = jnp.einsum('bngc,gce->bnge', pooled, w).reshape(B_, n, C_W)
    return y * scale


def diff_core(q, k, v, lam):
    s = jnp.einsum('bqhmd,bkhmd->bhmqk', q, k, preferred_element_type=F32) * HEAD_DIM ** -0.5
    p = jax.nn.softmax(s, axis=-1)
    a = p[:, :, 0] - lam * p[:, :, 1]
    return jnp.einsum('bhqk,bkhe->bqhe', a.astype(v.dtype), v)


def diff_post(o, g, lam_init):
    B_, n = o.shape[:2]
    return (rms_norm(o, g) * (1 - lam_init)).reshape(B_, n, D_W)


def diff_latent(q, k, v, kc, vc, lam):
    B_, S = q.shape[:2]
    nb = S // D_BLOCK
    kall = jnp.concatenate([k, kc], axis=1)
    vall = jnp.concatenate([v, vc], axis=1)
    qb = q.reshape(B_, nb, D_BLOCK, D_HEADS, 2, HEAD_DIM).swapaxes(0, 1)
    o = lax.map(lambda qblk: diff_core(qblk, kall, vall, lam), qb)
    return o.swapaxes(0, 1).reshape(B_, S, D_HEADS, 2 * HEAD_DIM)


def merge_branches(ys, gz, b_gate, w_branch, w_out):
    B_, n, _ = gz.shape
    g = jax.nn.sigmoid((gz + b_gate).astype(F32)).astype(gz.dtype).reshape(B_, n, N_BRANCH, D_MODEL)
    proj = jnp.einsum('bnkw,kwd->bnkd', jnp.stack(ys, axis=2), w_branch)
    return jnp.sum(g * proj, axis=2) @ w_out


def sq_relu_mlp(h, w1, w2):
    return jnp.square(jax.nn.relu(h @ w1)) @ w2


def setup_inputs(seed: int = 0) -> dict:
    key = jax.random.key(seed)
    ks = jax.random.split(key, 24)

    def nrm(k, shape, s):
        return s * jax.random.normal(k, shape, F32)

    return {
        "x": nrm(ks[0], (BATCH, SEQ, D_MODEL), 1.0),
        "c": nrm(ks[1], (BATCH, D_MODEL), 1.0),
        "ctx": nrm(ks[2], (BATCH, CTX_LEN, D_MODEL), 1.0),
        "c_ctx": nrm(ks[3], (D_MODEL,), 1.0),
        "w_ada": nrm(ks[4], (DEPTH, D_MODEL, 6 * D_MODEL), 0.5 * D_MODEL ** -0.5),
        "b_ada": nrm(ks[5], (DEPTH, 6 * D_MODEL), 0.01),
        "g_norm1": 1.0 + nrm(ks[6], (DEPTH, D_MODEL), 0.05),
        "g_norm2": 1.0 + nrm(ks[7], (DEPTH, D_MODEL), 0.05),
        "w_in": nrm(ks[8], (DEPTH, D_MODEL, PROJ_W), D_MODEL ** -0.5),
        "b_gate": nrm(ks[9], (DEPTH, N_BRANCH * D_MODEL), 0.01),
        "qk_gain": 1.0 + nrm(ks[10], (DEPTH, 3, 2, HEAD_DIM), 0.05),
        "a_sink": nrm(ks[11], (DEPTH, A_HEADS), 0.5),
        "b_rpb": nrm(ks[12], (DEPTH, B_HEADS, 2 * B_WIN_ROWS_MAX - 1, 2 * B_WIN_COLS - 1), 0.1),
        "c_w": nrm(ks[13], (DEPTH, C_GROUPS, C_GROUP_DIM, C_GROUP_DIM), C_GROUP_DIM ** -0.5),
        "c_scale": 1.0 + nrm(ks[14], (DEPTH, C_W), 0.1),
        "d_lambda": nrm(ks[15], (DEPTH, 4, HEAD_DIM), 0.1),
        "d_subln": 1.0 + nrm(ks[16], (DEPTH, 2 * HEAD_DIM), 0.05),
        "w_branch": nrm(ks[17], (DEPTH, N_BRANCH, BRANCH_W, D_MODEL), BRANCH_W ** -0.5),
        "w_out": nrm(ks[18], (DEPTH, D_MODEL, D_MODEL), D_MODEL ** -0.5),
        "w_ff1": nrm(ks[19], (DEPTH, D_MODEL, D_FF), D_MODEL ** -0.5),
        "w_ff2": nrm(ks[20], (DEPTH, D_FF, D_MODEL), D_FF ** -0.5),
    }


def reference(x, c, ctx, c_ctx, w_ada, b_ada, g_norm1, g_norm2, w_in, b_gate, qk_gain, a_sink, b_rpb,
              c_w, c_scale, d_lambda, d_subln, w_branch, w_out, w_ff1, w_ff2):
    n = x.shape[1]
    rope = axial_rope(n)
    xc = ctx
    for l in range(DEPTH):
        last = l == DEPTH - 1
        mod = (jax.nn.silu(c) @ w_ada[l] + b_ada[l])[:, None, :]
        mod_c = (jax.nn.silu(c_ctx) @ w_ada[l] + b_ada[l])[None, None, :]
        sh1, sc1, gt1, sh2, sc2, gt2 = jnp.split(mod, 6, axis=-1)
        sh1c, sc1c, gt1c, sh2c, sc2c, gt2c = jnp.split(mod_c, 6, axis=-1)
        lam_init = 0.8 - 0.6 * math.exp(-0.3 * l)
        dl = d_lambda[l].astype(F32)
        lam = jnp.exp(jnp.sum(dl[0] * dl[1])) - jnp.exp(jnp.sum(dl[2] * dl[3])) + lam_init

        h = modulate(rms_norm(x, g_norm1[l]), sh1, sc1)
        hc = modulate(rms_norm(xc, g_norm1[l]), sh1c, sc1c)
        aq, ak, av, bq, bk, bv, cu, dq, dk, dv, gz = split_heads(h @ w_in[l], qk_gain[l], rope)
        aqc, akc, avc, bqc, bkc, bvc, cuc, dqc, dkc, dvc, gzc = split_heads(hc @ w_in[l], qk_gain[l])
        ya = window_gqa(aq, ak, av, akc, avc, a_sink[l])
        yb = neighbourhood_attn(bq, bk, bv, bkc, bvc, b_rpb[l])
        yc = pool_branch(cu, c_w[l], c_scale[l])
        yd = diff_post(diff_latent(dq, dk, dv, dkc, dvc, lam), d_subln[l], lam_init)
        x = x + gt1 * merge_branches([ya, yb, yc, yd], gz, b_gate[l], w_branch[l], w_out[l])
        if not last:
            yac = context_attn(aqc, akc, avc, a_sink[l])
            ybc = context_attn(bqc, bkc, bvc)
            ycc = pool_branch(cuc, c_w[l], c_scale[l])
            ydc = diff_post(diff_core(dqc, dkc, dvc, lam), d_subln[l], lam_init)
            xc = xc + gt1c * merge_branches([yac, ybc, ycc, ydc], gzc, b_gate[l], w_branch[l], w_out[l])

        h2 = modulate(rms_norm(x, g_norm2[l]), sh2, sc2)
        x = x + gt2 * sq_relu_mlp(h2, w_ff1[l], w_ff2[l])
        if not last:
            h2c = modulate(rms_norm(xc, g_norm2[l]), sh2c, sc2c)
            xc = xc + gt2c * sq_relu_mlp(h2c, w_ff1[l], w_ff2[l])
    return x
```

```python
import functools
import math

import jax
import jax.numpy as jnp
from jax import lax
from jax.experimental import pallas as pl
from jax.experimental.pallas import tpu as pltpu

D_MODEL = 2048
BATCH = 4
SEQ = 2048
DEPTH = 2
GRID_W = 64
GRID_H = SEQ // GRID_W
CTX_LEN = 256
HEAD_DIM = 64
N_BRANCH = 4
BRANCH_W = D_MODEL // N_BRANCH
A_HEADS = 8
A_KV_HEADS = 2
A_WINDOW = 128
A_BLOCK = 128
B_HEADS = 8
B_WIN_ROWS = 8
B_WIN_COLS = 16
C_WINDOWS = (2, 4, 8, 16)
C_GROUP_DIM = 128
D_HEADS = 4
D_FF = 4 * D_MODEL
ROPE_BASE = 10000.0
EPS = 1e-6
NEG = -1e30
ATTN_SCALE = HEAD_DIM ** -0.5

F32 = jnp.float32
BF16 = jnp.bfloat16

V7X_VMEM_LIMIT_BYTES = 56 * 1024 * 1024
LANES = 128

Z_TILE = 512
GATE_TILES = N_BRANCH * D_MODEL // Z_TILE
T_AQ, T_AKV, T_BQ, T_BK, T_BV, T_CU, T_DQ, T_DK, T_DV = range(GATE_TILES, GATE_TILES + 9)
Z_TILES = GATE_TILES + 9
Z_WIDTH = Z_TILES * Z_TILE


def _cparams(sem):
    return pltpu.CompilerParams(dimension_semantics=sem, vmem_limit_bytes=V7X_VMEM_LIMIT_BYTES)


def _dot_nt(a, b):
    return lax.dot_general(a, b, (((1,), (1,)), ((), ())), preferred_element_type=F32)


def _dot(a, b):
    return jnp.dot(a, b, preferred_element_type=F32)


def _mod_kernel(c_ref, w_ref, b_ref, o_ref):
    c = c_ref[...]
    s = c * (0.5 * (jnp.tanh(0.5 * c) + 1.0))
    o_ref[0] = _dot(s, w_ref[0]) + b_ref[0]


def _modulation(cc, w_ada, b_ada):
    tn = 1024
    n = 6 * D_MODEL
    return pl.pallas_call(
        _mod_kernel,
        grid=(DEPTH, n // tn),
        in_specs=[
            pl.BlockSpec((8, D_MODEL), lambda l, j: (0, 0)),
            pl.BlockSpec((1, D_MODEL, tn), lambda l, j: (l, 0, j)),
            pl.BlockSpec((1, 1, tn), lambda l, j: (l, 0, j)),
        ],
        out_specs=pl.BlockSpec((1, 8, tn), lambda l, j: (l, 0, j)),
        out_shape=jax.ShapeDtypeStruct((DEPTH, 8, n), F32),
        compiler_params=_cparams(("parallel", "parallel")),
        name="modulation",
    )(cc, w_ada, b_ada.reshape(DEPTH, 1, n))


def _norm_modulate(x, g, shift, scale):
    ms = jnp.mean(x * x, axis=-1, keepdims=True)
    y = x * lax.rsqrt(ms + EPS) * g
    return y * (1.0 + scale) + shift


def _swap16(y):
    lane = lax.broadcasted_iota(jnp.int32, y.shape, 1)
    fwd = pltpu.roll(y, LANES - 16, axis=1)
    bwd = pltpu.roll(y, 16, axis=1)
    return jnp.where((lane & 16) == 0, fwd, bwd)


def _inproj_kernel(x_ref, gn_ref, shift_ref, scale_ref, w_ref, bias_ref, gain_ref, cos_ref, sin_ref, bd_ref,
                   z_ref, h_scr, *, rope):
    j = pl.program_id(1)

    @pl.when(j == 0)
    def _():
        h = _norm_modulate(x_ref[...], gn_ref[...], shift_ref[0], scale_ref[0])
        h_scr[...] = h.astype(BF16)

    acc = _dot(h_scr[...], w_ref[...])

    def head_norm(a, gain, with_rope):
        ms = _dot((a * a).astype(BF16), bd_ref[...]) * (1.0 / HEAD_DIM)
        y = a * lax.rsqrt(ms + EPS) * gain
        if with_rope:
            y = y * cos_ref[...] + _swap16(y) * sin_ref[...]
        return y

    def norm_tile(n_norm_chunks, with_rope):
        for c in range(Z_TILE // LANES):
            sl = slice(c * LANES, (c + 1) * LANES)
            a = acc[:, sl]
            if c < n_norm_chunks:
                a = head_norm(a, gain_ref[:, sl], with_rope)
            z_ref[:, sl] = a.astype(BF16)

    @pl.when(j < GATE_TILES)
    def _():
        t = acc + bias_ref[...]
        z_ref[...] = (0.5 * (jnp.tanh(0.5 * t) + 1.0)).astype(BF16)

    @pl.when((j == T_AQ) | (j == T_DQ) | (j == T_DK))
    def _():
        norm_tile(4, rope)

    @pl.when(j == T_AKV)
    def _():
        norm_tile(1, rope)

    @pl.when((j == T_BQ) | (j == T_BK))
    def _():
        norm_tile(4, False)

    @pl.when((j == T_BV) | (j == T_CU) | (j == T_DV))
    def _():
        z_ref[...] = acc.astype(BF16)


def _inproj(x, gn, mod3, w_p, bias_p, gain_p, cos_t, sin_t, bd, *, tm, rope, mod_row):
    m = x.shape[0]
    kernel = functools.partial(_inproj_kernel, rope=rope)
    tiles_per_seq = SEQ // tm if rope else 1
    return pl.pallas_call(
        kernel,
        grid=(m // tm, Z_TILES),
        in_specs=[
            pl.BlockSpec((tm, D_MODEL), lambda i, j: (i, 0)),
            pl.BlockSpec((1, D_MODEL), lambda i, j: (0, 0)),
            pl.BlockSpec((1, 1, D_MODEL), lambda i, j: (mod_row(i), 0, 0)),
            pl.BlockSpec((1, 1, D_MODEL), lambda i, j: (mod_row(i), 0, 1)),
            pl.BlockSpec((D_MODEL, Z_TILE), lambda i, j: (0, j)),
            pl.BlockSpec((1, Z_TILE), lambda i, j: (0, j)),
            pl.BlockSpec((1, Z_TILE), lambda i, j: (0, j)),
            pl.BlockSpec((tm, LANES), lambda i, j: (i % tiles_per_seq, 0)),
            pl.BlockSpec((tm, LANES), lambda i, j: (i % tiles_per_seq, 0)),
            pl.BlockSpec((LANES, LANES), lambda i, j: (0, 0)),
        ],
        out_specs=pl.BlockSpec((tm, Z_TILE), lambda i, j: (i, j)),
        out_shape=jax.ShapeDtypeStruct((m, Z_WIDTH), BF16),
        scratch_shapes=[pltpu.VMEM((tm, D_MODEL), BF16)],
        compiler_params=_cparams(("parallel", "arbitrary")),
        name="inproj_rope" if rope else "inproj_ctx",
    )(x, gn, mod3, mod3, w_p, bias_p, gain_p, cos_t, sin_t, bd)


def _softmax_attend(qh, segs, extra_logit=None):
    scores = []
    for k, _, bias in segs:
        s = _dot_nt(qh, k)
        if bias is not None:
            s = s + bias
        scores.append(s)
    m = scores[0].max(axis=-1, keepdims=True)
    for s in scores[1:]:
        m = jnp.maximum(m, s.max(axis=-1, keepdims=True))
    if extra_logit is not None:
        m = jnp.maximum(m, extra_logit)
    l = None
    o = None
    for s, (_, v, _) in zip(scores, segs):
        p = jnp.exp(s - m)
        ls = p.sum(axis=-1, keepdims=True)
        os_ = _dot(p.astype(BF16), v)
        l = ls if l is None else l + ls
        o = os_ if o is None else o + os_
    if extra_logit is not None:
        l = l + jnp.exp(extra_logit - m)
    return o, l


def _win_gqa_kernel(sink_ref, q_ref, *rest, latent):
    if latent:
        kv_ref, ckv_ref, o_ref = rest
    else:
        ckv_ref, o_ref = rest
    q = q_ref[...] * ATTN_SCALE
    tq = q.shape[0]
    ckv = ckv_ref[...]
    if latent:
        n = pl.program_id(1)
        span = 3 * A_BLOCK
        start = pl.multiple_of(jnp.clip((n - 1) * A_BLOCK, 0, SEQ - span), A_BLOCK)
        kvw = kv_ref[pl.ds(start, span), :]
        qpos = n * A_BLOCK + lax.broadcasted_iota(jnp.int32, (tq, span), 0)
        kpos = start + lax.broadcasted_iota(jnp.int32, (tq, span), 1)
        band = jnp.where(jnp.abs(qpos - kpos) <= A_WINDOW, 0.0, NEG).astype(F32)
    group = A_HEADS // A_KV_HEADS
    kv_w = A_KV_HEADS * HEAD_DIM
    outs = []
    for h in range(A_HEADS):
        g = h // group
        ks = slice(g * HEAD_DIM, (g + 1) * HEAD_DIM)
        vs = slice(kv_w + g * HEAD_DIM, kv_w + (g + 1) * HEAD_DIM)
        segs = [(ckv[:, ks], ckv[:, vs], None)]
        if latent:
            segs.append((kvw[:, ks], kvw[:, vs], band))
        o, l = _softmax_attend(q[:, h * HEAD_DIM:(h + 1) * HEAD_DIM], segs, extra_logit=sink_ref[h])
        outs.append(o / l)
    o_ref[...] = jnp.concatenate(outs, axis=1).astype(BF16)


def _win_gqa(z, zc, sink, *, latent):
    kvb = 2 * A_KV_HEADS * HEAD_DIM
    kv_col = T_AKV * Z_TILE // kvb
    smem = pl.BlockSpec(memory_space=pltpu.SMEM)
    if latent:
        nq = SEQ // A_BLOCK
        grid = (BATCH, nq)
        in_specs = [
            smem,
            pl.BlockSpec((A_BLOCK, Z_TILE), lambda b, n: (b * nq + n, T_AQ)),
            pl.BlockSpec((SEQ, kvb), lambda b, n: (b, kv_col)),
            pl.BlockSpec((CTX_LEN, kvb), lambda b, n: (b, kv_col)),
        ]
        out_specs = pl.BlockSpec((A_BLOCK, Z_TILE), lambda b, n: (b * nq + n, 0))
        args = (sink, z, z, zc)
        rows = BATCH * SEQ
        sem = ("parallel", "arbitrary")
    else:
        grid = (BATCH,)
        in_specs = [
            smem,
            pl.BlockSpec((CTX_LEN, Z_TILE), lambda b: (b, T_AQ)),
            pl.BlockSpec((CTX_LEN, kvb), lambda b: (b, kv_col)),
        ]
        out_specs = pl.BlockSpec((CTX_LEN, Z_TILE), lambda b: (b, 0))
        args = (sink, zc, zc)
        rows = BATCH * CTX_LEN
        sem = ("parallel",)
    return pl.pallas_call(
        functools.partial(_win_gqa_kernel, latent=latent),
        grid=grid,
        in_specs=in_specs,
        out_specs=out_specs,
        out_shape=jax.ShapeDtypeStruct((rows, BRANCH_W), BF16),
        compiler_params=_cparams(sem),
        name="win_gqa" if latent else "win_gqa_ctx",
    )(*args)


N_ROW_OFFSETS = 2 * B_WIN_ROWS - 1
N_COL_OFFSETS = 2 * B_WIN_COLS - 1
N_PAIR = N_ROW_OFFSETS + 1


def _rpb_table_kernel(rpb_ref, o_ref):
    h = pl.program_id(0)
    shape = (GRID_W, LANES)
    qcol = lax.broadcasted_iota(jnp.int32, shape, 0)
    lane = lax.broadcasted_iota(jnp.int32, shape, 1)
    kcol = lane & (GRID_W - 1)
    hi = lane >= GRID_W
    cs = jnp.clip(qcol - B_WIN_COLS // 2, 0, GRID_W - B_WIN_COLS)
    in_window = (kcol >= cs) & (kcol < cs + B_WIN_COLS)
    ci = kcol - qcol + (B_WIN_COLS - 1)
    def rpb(d, c):
        return rpb_ref[(h * N_ROW_OFFSETS + d) * N_COL_OFFSETS + c]

    for p in range(N_PAIR):
        acc = jnp.zeros(shape, F32)
        for c in range(N_COL_OFFSETS):
            hit = ci == c
            if p - 1 >= 0:
                acc = jnp.where(hit & ~hi, rpb(p - 1, c), acc)
            if p < N_ROW_OFFSETS:
                acc = jnp.where(hit & hi, rpb(p, c), acc)
        ok = in_window
        if p == 0:
            ok = ok & hi
        if p == N_PAIR - 1:
            ok = ok & ~hi
        o_ref[p] = jnp.where(ok, acc, NEG)


def _rpb_table(rpb):
    return pl.pallas_call(
        _rpb_table_kernel,
        grid=(B_HEADS,),
        in_specs=[pl.BlockSpec(memory_space=pltpu.SMEM)],
        out_specs=pl.BlockSpec((N_PAIR, GRID_W, LANES), lambda h: (h, 0, 0)),
        out_shape=jax.ShapeDtypeStruct((B_HEADS * N_PAIR, GRID_W, LANES), F32),
        compiler_params=_cparams(("parallel",)),
        name="rpb_table",
    )(rpb.reshape(-1))


def _nbr_kernel(q_ref, k_ref, v_ref, kc_ref, vc_ref, t_ref, o_ref):
    r = pl.program_id(1)
    rs = jnp.clip(r - B_WIN_ROWS // 2, 0, GRID_H - B_WIN_ROWS)
    d0 = rs - r + (B_WIN_ROWS - 1)
    base = pl.multiple_of(rs * GRID_W, GRID_W)
    nk = B_WIN_ROWS * GRID_W
    q = q_ref[...] * ATTN_SCALE
    ks = k_ref[pl.ds(base, nk), :]
    vs = v_ref[pl.ds(base, nk), :]
    kc = kc_ref[...]
    vc = vc_ref[...]
    outs = []
    for h in range(B_HEADS):
        hs = slice(h * HEAD_DIM, (h + 1) * HEAD_DIM)
        bias = jnp.concatenate(
            [t_ref[h * N_PAIR + d0 + 1 + 2 * i] for i in range(B_WIN_ROWS // 2)], axis=1)
        o, l = _softmax_attend(q[:, hs], [(ks[:, hs], vs[:, hs], bias), (kc[:, hs], vc[:, hs], None)])
        outs.append(o / l)
    o_ref[...] = jnp.concatenate(outs, axis=1).astype(BF16)


def _nbr_attn(z, zc, table):
    return pl.pallas_call(
        _nbr_kernel,
        grid=(BATCH, GRID_H),
        in_specs=[
            pl.BlockSpec((GRID_W, Z_TILE), lambda b, r: (b * GRID_H + r, T_BQ)),
            pl.BlockSpec((SEQ, Z_TILE), lambda b, r: (b, T_BK)),
            pl.BlockSpec((SEQ, Z_TILE), lambda b, r: (b, T_BV)),
            pl.BlockSpec((CTX_LEN, Z_TILE), lambda b, r: (b, T_BK)),
            pl.BlockSpec((CTX_LEN, Z_TILE), lambda b, r: (b, T_BV)),
            pl.BlockSpec((B_HEADS * N_PAIR, GRID_W, LANES), lambda b, r: (0, 0, 0)),
        ],
        out_specs=pl.BlockSpec((GRID_W, Z_TILE), lambda b, r: (b * GRID_H + r, 0)),
        out_shape=jax.ShapeDtypeStruct((BATCH * SEQ, BRANCH_W), BF16),
        compiler_params=_cparams(("parallel", "arbitrary")),
        name="nbr_attn",
    )(z, z, z, zc, zc, table)


def _ctx_mha_kernel(q_ref, k_ref, v_ref, o_ref):
    q = q_ref[...] * ATTN_SCALE
    k = k_ref[...]
    v = v_ref[...]
    outs = []
    for h in range(B_HEADS):
        hs = slice(h * HEAD_DIM, (h + 1) * HEAD_DIM)
        o, l = _softmax_attend(q[:, hs], [(k[:, hs], v[:, hs], None)])
        outs.append(o / l)
    o_ref[...] = jnp.concatenate(outs, axis=1).astype(BF16)


def _ctx_mha(zc):
    return pl.pallas_call(
        _ctx_mha_kernel,
        grid=(BATCH,),
        in_specs=[
            pl.BlockSpec((CTX_LEN, Z_TILE), lambda b: (b, T_BQ)),
            pl.BlockSpec((CTX_LEN, Z_TILE), lambda b: (b, T_BK)),
            pl.BlockSpec((CTX_LEN, Z_TILE), lambda b: (b, T_BV)),
        ],
        out_specs=pl.BlockSpec((CTX_LEN, Z_TILE), lambda b: (b, 0)),
        out_shape=jax.ShapeDtypeStruct((BATCH * CTX_LEN, BRANCH_W), BF16),
        compiler_params=_cparams(("parallel",)),
        name="nbr_attn_ctx",
    )(zc, zc, zc)


def _pool_kernel(u_ref, w_ref, scale_ref, o_ref):
    n = u_ref.shape[0]
    t = lax.broadcasted_iota(jnp.int32, (n, C_GROUP_DIM), 0)

    def down(a, k):
        return jnp.where(t >= k, pltpu.roll(a, k, axis=0), 0.0)

    def up(a, k):
        return jnp.where(t < n - k, pltpu.roll(a, n - k, axis=0), 0.0)

    for g, w in enumerate(C_WINDOWS):
        sl = slice(g * C_GROUP_DIM, (g + 1) * C_GROUP_DIM)
        u = u_ref[:, sl].astype(F32)
        half = w // 2
        back = u
        fwd = u
        k = 1
        while k < half:
            back = back + down(back, k)
            fwd = fwd + up(fwd, k)
            k *= 2
        total = down(back, 1) + fwd
        lo = jnp.maximum(t - half, 0)
        hi = jnp.minimum(t + half - 1, n - 1)
        cnt = (hi - lo + 1).astype(F32)
        pooled = total / cnt - u
        y = _dot(pooled.astype(BF16), w_ref[g]) * scale_ref[:, sl]
        o_ref[:, sl] = y.astype(BF16)


def _pool(z, c_w, c_scale, *, n):
    rows = z.shape[0]
    return pl.pallas_call(
        _pool_kernel,
        grid=(rows // n,),
        in_specs=[
            pl.BlockSpec((n, Z_TILE), lambda b: (b, T_CU)),
            pl.BlockSpec((len(C_WINDOWS), C_GROUP_DIM, C_GROUP_DIM), lambda b: (0, 0, 0)),
            pl.BlockSpec((1, BRANCH_W), lambda b: (0, 0)),
        ],
        out_specs=pl.BlockSpec((n, BRANCH_W), lambda b: (b, 0)),
        out_shape=jax.ShapeDtypeStruct((rows, BRANCH_W), BF16),
        compiler_params=_cparams(("parallel",)),
        name=f"pool_{n}",
    )(z, c_w, c_scale)


def _diff_kernel(lam_ref, q_ref, *rest, lam_init, latent):
    if latent:
        k_ref, v_ref, kc_ref, vc_ref, g_ref, o_ref = rest
    else:
        kc_ref, vc_ref, g_ref, o_ref = rest
    dl = lam_ref[...]
    lam = (jnp.exp(jnp.sum(dl[0:1] * dl[1:2], axis=-1, keepdims=True))
           - jnp.exp(jnp.sum(dl[2:3] * dl[3:4], axis=-1, keepdims=True)) + lam_init)
    q = q_ref[...] * ATTN_SCALE
    kc = kc_ref[...]
    vc = vc_ref[...]
    if latent:
        k = k_ref[...]
        v = v_ref[...]
    hw = 2 * HEAD_DIM
    outs = []
    for h in range(D_HEADS):
        vsl = slice(h * hw, (h + 1) * hw)
        parts = []
        for m in range(2):
            sl = slice(h * hw + m * HEAD_DIM, h * hw + (m + 1) * HEAD_DIM)
            segs = [(kc[:, sl], vc[:, vsl], None)]
            if latent:
                segs.append((k[:, sl], v[:, vsl], None))
            o, l = _softmax_attend(q[:, sl], segs)
            parts.append(o / l)
        o = parts[0] - lam * parts[1]
        ms = jnp.mean(o * o, axis=-1, keepdims=True)
        y = o * lax.rsqrt(ms + EPS) * g_ref[...] * (1.0 - lam_init)
        outs.append(y)
    o_ref[...] = jnp.concatenate(outs, axis=1).astype(BF16)


def _diff_attn(z, zc, d_lambda, subln, *, lam_init, latent):
    full = lambda shape: pl.BlockSpec(shape, lambda *_: (0,) * len(shape))
    if latent:
        tq = 256
        nq = SEQ // tq
        grid = (BATCH, nq)
        in_specs = [
            full((4, HEAD_DIM)),
            pl.BlockSpec((tq, Z_TILE), lambda b, n: (b * nq + n, T_DQ)),
            pl.BlockSpec((SEQ, Z_TILE), lambda b, n: (b, T_DK)),
            pl.BlockSpec((SEQ, Z_TILE), lambda b, n: (b, T_DV)),
            pl.BlockSpec((CTX_LEN, Z_TILE), lambda b, n: (b, T_DK)),
            pl.BlockSpec((CTX_LEN, Z_TILE), lambda b, n: (b, T_DV)),
            full((1, 2 * HEAD_DIM)),
        ]
        out_specs = pl.BlockSpec((tq, Z_TILE), lambda b, n: (b * nq + n, 0))
        args = (d_lambda, z, z, z, zc, zc, subln)
        rows = BATCH * SEQ
        sem = ("parallel", "arbitrary")
    else:
        grid = (BATCH,)
        in_specs = [
            full((4, HEAD_DIM)),
            pl.BlockSpec((CTX_LEN, Z_TILE), lambda b: (b, T_DQ)),
            pl.BlockSpec((CTX_LEN, Z_TILE), lambda b: (b, T_DK)),
            pl.BlockSpec((CTX_LEN, Z_TILE), lambda b: (b, T_DV)),
            full((1, 2 * HEAD_DIM)),
        ]
        out_specs = pl.BlockSpec((CTX_LEN, Z_TILE), lambda b: (b, 0))
        args = (d_lambda, zc, zc, zc, subln)
        rows = BATCH * CTX_LEN
        sem = ("parallel",)
    return pl.pallas_call(
        functools.partial(_diff_kernel, lam_init=lam_init, latent=latent),
        grid=grid,
        in_specs=in_specs,
        out_specs=out_specs,
        out_shape=jax.ShapeDtypeStruct((rows, BRANCH_W), BF16),
        compiler_params=_cparams(sem),
        name="diff_attn" if latent else "diff_attn_ctx",
    )(*args)


def _merge_kernel(ya_ref, yb_ref, yc_ref, yd_ref, g_ref, wb_ref, wo_ref, x_ref, gt_ref, o_ref, acc_scr):
    k = pl.program_id(1)
    y_refs = (ya_ref, yb_ref, yc_ref, yd_ref)

    for kk in range(N_BRANCH):
        @pl.when(k == kk)
        def _(kk=kk):
            proj = _dot(y_refs[kk][...], wb_ref[0]) * g_ref[...].astype(F32)
            if kk == 0:
                acc_scr[...] = proj
            else:
                acc_scr[...] += proj

    @pl.when(k == N_BRANCH - 1)
    def _():
        out = _dot(acc_scr[...].astype(BF16), wo_ref[...])
        o_ref[...] = x_ref[...] + gt_ref[0] * out


def _merge(ys, z, w_branch, w_out, x, mod3, *, tm, mod_row):
    m = x.shape[0]
    y_spec = pl.BlockSpec((tm, BRANCH_W), lambda i, k: (i, 0))
    return pl.pallas_call(
        _merge_kernel,
        grid=(m // tm, N_BRANCH),
        in_specs=[
            y_spec, y_spec, y_spec, y_spec,
            pl.BlockSpec((tm, D_MODEL), lambda i, k: (i, k)),
            pl.BlockSpec((1, BRANCH_W, D_MODEL), lambda i, k: (k, 0, 0)),
            pl.BlockSpec((D_MODEL, D_MODEL), lambda i, k: (0, 0)),
            pl.BlockSpec((tm, D_MODEL), lambda i, k: (i, 0)),
            pl.BlockSpec((1, 1, D_MODEL), lambda i, k: (mod_row(i), 0, 2)),
        ],
        out_specs=pl.BlockSpec((tm, D_MODEL), lambda i, k: (i, 0)),
        out_shape=jax.ShapeDtypeStruct((m, D_MODEL), F32),
        scratch_shapes=[pltpu.VMEM((tm, D_MODEL), F32)],
        compiler_params=_cparams(("parallel", "arbitrary")),
        name=f"merge_{m}",
    )(*ys, z, w_branch, w_out, x, mod3)


def _mlp_kernel(x_ref, gn_ref, shift_ref, scale_ref, gt_ref, w1_ref, w2_ref, o_ref, h_scr, acc_scr):
    f = pl.program_id(1)

    @pl.when(f == 0)
    def _():
        h = _norm_modulate(x_ref[...], gn_ref[...], shift_ref[0], scale_ref[0])
        h_scr[...] = h.astype(BF16)

    a = jnp.maximum(_dot(h_scr[...], w1_ref[...]), 0.0)
    part = _dot((a * a).astype(BF16), w2_ref[...])

    @pl.when(f == 0)
    def _():
        acc_scr[...] = part

    @pl.when(f > 0)
    def _():
        acc_scr[...] += part

    @pl.when(f == pl.num_programs(1) - 1)
    def _():
        o_ref[...] = x_ref[...] + gt_ref[0] * acc_scr[...]


def _mlp(x, gn, mod3, w1, w2, *, tm, tf, mod_row):
    m = x.shape[0]
    return pl.pallas_call(
        _mlp_kernel,
        grid=(m // tm, D_FF // tf),
        in_specs=[
            pl.BlockSpec((tm, D_MODEL), lambda i, f: (i, 0)),
            pl.BlockSpec((1, D_MODEL), lambda i, f: (0, 0)),
            pl.BlockSpec((1, 1, D_MODEL), lambda i, f: (mod_row(i), 0, 3)),
            pl.BlockSpec((1, 1, D_MODEL), lambda i, f: (mod_row(i), 0, 4)),
            pl.BlockSpec((1, 1, D_MODEL), lambda i, f: (mod_row(i), 0, 5)),
            pl.BlockSpec((D_MODEL, tf), lambda i, f: (0, f)),
            pl.BlockSpec((tf, D_MODEL), lambda i, f: (f, 0)),
        ],
        out_specs=pl.BlockSpec((tm, D_MODEL), lambda i, f: (i, 0)),
        out_shape=jax.ShapeDtypeStruct((m, D_MODEL), F32),
        scratch_shapes=[pltpu.VMEM((tm, D_MODEL), BF16), pltpu.VMEM((tm, D_MODEL), F32)],
        compiler_params=_cparams(("parallel", "arbitrary")),
        name=f"mlp_{m}",
    )(x, gn, mod3, mod3, mod3, w1, w2)


def _rope_tables():
    t = jnp.arange(SEQ)
    row = (t // GRID_W).astype(F32)
    col = (t % GRID_W).astype(F32)
    n_freq = HEAD_DIM // 4
    inv = ROPE_BASE ** (-jnp.arange(n_freq, dtype=F32) / n_freq)
    ar = row[:, None] * inv
    ac = col[:, None] * inv
    cos_h = jnp.concatenate([jnp.cos(ar), jnp.cos(ar), jnp.cos(ac), jnp.cos(ac)], axis=-1)
    sin_h = jnp.concatenate([-jnp.sin(ar), jnp.sin(ar), -jnp.sin(ac), jnp.sin(ac)], axis=-1)
    return jnp.tile(cos_h, (1, LANES // HEAD_DIM)), jnp.tile(sin_h, (1, LANES // HEAD_DIM))


def _permute_in_weights(w_in_l, b_gate_l, qk_gain_l):
    head_cols = Z_WIDTH - N_BRANCH * D_MODEL
    n_small = 768
    n_gate0 = 4352
    pad = jnp.zeros((D_MODEL, Z_TILE - 2 * A_KV_HEADS * HEAD_DIM), F32)
    w = jnp.concatenate([w_in_l[:, n_gate0:], w_in_l[:, :n_small], pad, w_in_l[:, n_small:n_gate0]], axis=1)
    bias = jnp.concatenate([b_gate_l, jnp.zeros((head_cols,), F32)])[None, :]
    ones = jnp.ones((Z_TILE,), F32)
    rep = lambda g: jnp.tile(g, Z_TILE // HEAD_DIM)
    akv_gain = jnp.concatenate([jnp.tile(qk_gain_l[0, 1], A_KV_HEADS), jnp.ones((Z_TILE - A_KV_HEADS * HEAD_DIM,), F32)])
    gain = jnp.concatenate([
        jnp.ones((N_BRANCH * D_MODEL,), F32),
        rep(qk_gain_l[0, 0]), akv_gain,
        rep(qk_gain_l[1, 0]), rep(qk_gain_l[1, 1]), ones, ones,
        rep(qk_gain_l[2, 0]), rep(qk_gain_l[2, 1]), ones,
    ])[None, :]
    return w.astype(BF16), bias, gain


def kernel(x, c, ctx, c_ctx, w_ada, b_ada, g_norm1, g_norm2, w_in, b_gate, qk_gain, a_sink, b_rpb,
           c_w, c_scale, d_lambda, d_subln, w_branch, w_out, w_ff1, w_ff2):
    cos_t, sin_t = _rope_tables()
    lane = jnp.arange(LANES)
    bd = (lane[:, None] // HEAD_DIM == lane[None, :] // HEAD_DIM).astype(BF16)

    cc = jnp.concatenate([c, c_ctx[None, :], jnp.zeros((8 - BATCH - 1, D_MODEL), F32)], axis=0)
    mod_all = _modulation(cc, w_ada, b_ada)

    xl = x.reshape(BATCH * SEQ, D_MODEL)
    xc = ctx.reshape(BATCH * CTX_LEN, D_MODEL)
    ctx_row = lambda i: BATCH

    for l in range(DEPTH):
        last = l == DEPTH - 1
        lam_init = 0.8 - 0.6 * math.exp(-0.3 * l)
        mod3 = mod_all[l].reshape(8, 1, 6 * D_MODEL)
        w_p, bias_p, gain_p = _permute_in_weights(w_in[l], b_gate[l], qk_gain[l])
        gn1 = g_norm1[l][None, :]
        gn2 = g_norm2[l][None, :]
        wb = w_branch[l].astype(BF16)
        wo = w_out[l].astype(BF16)
        w1 = w_ff1[l].astype(BF16)
        w2 = w_ff2[l].astype(BF16)
        cw = c_w[l].astype(BF16)
        cs = c_scale[l][None, :]
        subln = d_subln[l][None, :]

        tm = 1024
        z = _inproj(xl, gn1, mod3, w_p, bias_p, gain_p, cos_t, sin_t, bd,
                    tm=tm, rope=True, mod_row=lambda i: i // (SEQ // 1024))
        zc = _inproj(xc, gn1, mod3, w_p, bias_p, gain_p, cos_t, sin_t, bd,
                     tm=BATCH * CTX_LEN, rope=False, mod_row=ctx_row)

        table = _rpb_table(b_rpb[l])
        ya = _win_gqa(z, zc, a_sink[l], latent=True)
        yb = _nbr_attn(z, zc, table)
        yc = _pool(z, cw, cs, n=SEQ)
        yd = _diff_attn(z, zc, d_lambda[l], subln, lam_init=lam_init, latent=True)
        tmm = 256
        xl = _merge((ya, yb, yc, yd), z, wb, wo, xl, mod3, tm=tmm, mod_row=lambda i: i // (SEQ // 256))
        if not last:
            yac = _win_gqa(z, zc, a_sink[l], latent=False)
            ybc = _ctx_mha(zc)
            ycc = _pool(zc, cw, cs, n=CTX_LEN)
            ydc = _diff_attn(z, zc, d_lambda[l], subln, lam_init=lam_init, latent=False)
            xc = _merge((yac, ybc, ycc, ydc), zc, wb, wo, xc, mod3, tm=tmm, mod_row=ctx_row)

        tmf = 512
        xl = _mlp(xl, gn2, mod3, w1, w2, tm=tmf, tf=1024, mod_row=lambda i: i // (SEQ // 512))
        if not last:
            xc = _mlp(xc, gn2, mod3, w1, w2, tm=tmf, tf=1024, mod_row=ctx_row)

    return xl.reshape(BATCH, SEQ, D_MODEL)
```

```python
import functools
import math

import jax
import jax.numpy as jnp
from jax import lax
from jax.experimental import pallas as pl
from jax.experimental.pallas import tpu as pltpu

D_MODEL = 2048
BATCH = 4
SEQ = 2048
DEPTH = 2
GRID_W = 64
GRID_H = SEQ // GRID_W
CTX_LEN = 256
HEAD_DIM = 64
N_BRANCH = 4
BRANCH_W = D_MODEL // N_BRANCH
A_HEADS = 8
A_KV_HEADS = 2
A_GROUP = A_HEADS // A_KV_HEADS
A_WINDOW = 128
A_BLOCK = 128
B_HEADS = 8
B_WIN_ROWS = 8
B_WIN_COLS = 16
C_WINDOWS = (2, 4, 8, 16)
C_GROUP_DIM = 128
D_HEADS = 4
D_FF = 4 * D_MODEL
ROPE_BASE = 10000.0
EPS = 1e-6
NEG = -1e30
ATTN_SCALE = HEAD_DIM ** -0.5

F32 = jnp.float32
BF16 = jnp.bfloat16

V7X_VMEM_LIMIT_BYTES = 56 * 1024 * 1024
LANES = 128

Z_TILE = 512
GATE_TILES = N_BRANCH * D_MODEL // Z_TILE
T_AQ, T_AKV, T_BQ, T_BK, T_BV, T_CU, T_DQ, T_DK, T_DV = range(GATE_TILES, GATE_TILES + 9)
Z_TILES = GATE_TILES + 9
Z_WIDTH = Z_TILES * Z_TILE
W_IN_COLS = 12544
W_SRC = 256


def _cparams(sem):
    return pltpu.CompilerParams(dimension_semantics=sem, vmem_limit_bytes=V7X_VMEM_LIMIT_BYTES)


def _dot_nt(a, b):
    return lax.dot_general(a, b, (((1,), (1,)), ((), ())), preferred_element_type=F32)


def _dot(a, b):
    return jnp.dot(a, b, preferred_element_type=F32)


def _mod_kernel(c_ref, w_ref, b_ref, o_ref):
    c = c_ref[...]
    s = c * (0.5 * (jnp.tanh(0.5 * c) + 1.0))
    o_ref[0] = _dot(s, w_ref[0]) + b_ref[0]


def _modulation(cc, w_ada, b_ada):
    tn = 1024
    n = 6 * D_MODEL
    return pl.pallas_call(
        _mod_kernel,
        grid=(DEPTH, n // tn),
        in_specs=[
            pl.BlockSpec((8, D_MODEL), lambda l, j: (0, 0)),
            pl.BlockSpec((1, D_MODEL, tn), lambda l, j: (l, 0, j)),
            pl.BlockSpec((1, 1, tn), lambda l, j: (l, 0, j)),
        ],
        out_specs=pl.BlockSpec((1, 8, tn), lambda l, j: (l, 0, j)),
        out_shape=jax.ShapeDtypeStruct((DEPTH, 8, n), F32),
        compiler_params=_cparams(("parallel", "parallel")),
        name="modulation",
    )(cc, w_ada, b_ada.reshape(DEPTH, 1, n))


def _w_in_src_block(jb):
    n_gate = N_BRANCH * D_MODEL // W_SRC
    first_gate = (W_IN_COLS - N_BRANCH * D_MODEL) // W_SRC
    return jnp.where(jb < n_gate, jb + first_gate,
                     jnp.where(jb < n_gate + 3, jb - n_gate,
                               jnp.where(jb == n_gate + 3, 2, jb - n_gate - 1)))


def _w_in_prep_kernel(a_ref, b_ref, o_ref):
    o_ref[0, :, :W_SRC] = a_ref[0].astype(BF16)
    o_ref[0, :, W_SRC:] = b_ref[0].astype(BF16)


def _w_in_prep(w_in):
    src = lambda half: pl.BlockSpec((1, D_MODEL, W_SRC), lambda l, j: (l, 0, _w_in_src_block(2 * j + half)))
    return pl.pallas_call(
        _w_in_prep_kernel,
        grid=(DEPTH, Z_TILES),
        in_specs=[src(0), src(1)],
        out_specs=pl.BlockSpec((1, D_MODEL, Z_TILE), lambda l, j: (l, 0, j)),
        out_shape=jax.ShapeDtypeStruct((DEPTH, D_MODEL, Z_WIDTH), BF16),
        compiler_params=_cparams(("parallel", "parallel")),
        name="w_in_prep",
    )(w_in, w_in)


def _norm_modulate(x, g, shift, scale):
    ms = jnp.mean(x * x, axis=-1, keepdims=True)
    y = x * lax.rsqrt(ms + EPS) * g
    return y * (1.0 + scale) + shift


def _swap16(y):
    lane = lax.broadcasted_iota(jnp.int32, y.shape, 1)
    fwd = pltpu.roll(y, LANES - 16, axis=1)
    bwd = pltpu.roll(y, 16, axis=1)
    return jnp.where((lane & 16) == 0, fwd, bwd)


def _inproj_kernel(x_ref, gn_ref, shift_ref, scale_ref, w_ref, bias_ref, gain_ref, cos_ref, sin_ref, bd_ref,
                   z_ref, h_scr, *, rope):
    j = pl.program_id(1)

    @pl.when(j == 0)
    def _():
        h = _norm_modulate(x_ref[...], gn_ref[...], shift_ref[0], scale_ref[0])
        h_scr[...] = h.astype(BF16)

    acc = _dot(h_scr[...], w_ref[0])

    def head_norm(a, gain, with_rope):
        ms = _dot((a * a).astype(BF16), bd_ref[...]) * (1.0 / HEAD_DIM)
        y = a * lax.rsqrt(ms + EPS) * gain
        if with_rope:
            y = y * cos_ref[...] + _swap16(y) * sin_ref[...]
        return y

    def norm_tile(n_norm_chunks, with_rope):
        for c in range(Z_TILE // LANES):
            sl = slice(c * LANES, (c + 1) * LANES)
            a = acc[:, sl]
            if c < n_norm_chunks:
                a = head_norm(a, gain_ref[:, sl], with_rope)
            z_ref[:, sl] = a.astype(BF16)

    @pl.when(j < GATE_TILES)
    def _():
        t = acc + bias_ref[...]
        z_ref[...] = (0.5 * (jnp.tanh(0.5 * t) + 1.0)).astype(BF16)

    @pl.when((j == T_AQ) | (j == T_DQ) | (j == T_DK))
    def _():
        norm_tile(4, rope)

    @pl.when(j == T_AKV)
    def _():
        norm_tile(1, rope)

    @pl.when((j == T_BQ) | (j == T_BK))
    def _():
        norm_tile(4, False)

    @pl.when((j == T_BV) | (j == T_CU) | (j == T_DV))
    def _():
        z_ref[...] = acc.astype(BF16)


def _inproj(x, gn, mod3, w_p, bias_p, gain_p, cos_t, sin_t, bd, *, layer, tm, rope, mod_row):
    m = x.shape[0]
    kernel = functools.partial(_inproj_kernel, rope=rope)
    tiles_per_seq = SEQ // tm if rope else 1
    return pl.pallas_call(
        kernel,
        grid=(m // tm, Z_TILES),
        in_specs=[
            pl.BlockSpec((tm, D_MODEL), lambda i, j: (i, 0)),
            pl.BlockSpec((1, D_MODEL), lambda i, j: (0, 0)),
            pl.BlockSpec((1, 1, D_MODEL), lambda i, j: (mod_row(i), 0, 0)),
            pl.BlockSpec((1, 1, D_MODEL), lambda i, j: (mod_row(i), 0, 1)),
            pl.BlockSpec((1, D_MODEL, Z_TILE), lambda i, j: (layer, 0, j)),
            pl.BlockSpec((1, Z_TILE), lambda i, j: (0, j)),
            pl.BlockSpec((1, Z_TILE), lambda i, j: (0, j)),
            pl.BlockSpec((tm, LANES), lambda i, j: (i % tiles_per_seq, 0)),
            pl.BlockSpec((tm, LANES), lambda i, j: (i % tiles_per_seq, 0)),
            pl.BlockSpec((LANES, LANES), lambda i, j: (0, 0)),
        ],
        out_specs=pl.BlockSpec((tm, Z_TILE), lambda i, j: (i, j)),
        out_shape=jax.ShapeDtypeStruct((m, Z_WIDTH), BF16),
        scratch_shapes=[pltpu.VMEM((tm, D_MODEL), BF16)],
        compiler_params=_cparams(("parallel", "arbitrary")),
        name="inproj_rope" if rope else "inproj_ctx",
    )(x, gn, mod3, mod3, w_p, bias_p, gain_p, cos_t, sin_t, bd)


def _softmax_attend(q, segs, extra_logit=None):
    scores = []
    for k, _, bias in segs:
        s = _dot_nt(q, k)
        if bias is not None:
            s = s + bias
        scores.append(s)
    m = scores[0].max(axis=-1, keepdims=True)
    for s in scores[1:]:
        m = jnp.maximum(m, s.max(axis=-1, keepdims=True))
    if extra_logit is not None:
        m = jnp.maximum(m, extra_logit)
    l = None
    o = None
    for s, (_, v, _) in zip(scores, segs):
        p = jnp.exp(s - m)
        ls = p.sum(axis=-1, keepdims=True)
        os_ = _dot(p.astype(BF16), v)
        l = ls if l is None else l + ls
        o = os_ if o is None else o + os_
    if extra_logit is not None:
        l = l + jnp.exp(extra_logit - m)
    return o, l


def _lane_group(shape, width):
    return lax.broadcasted_iota(jnp.int32, shape, 1) // width


def _repeat_kv_heads(kv):
    pieces = [kv[:, i * HEAD_DIM:(i + 1) * HEAD_DIM] for i in range(2 * A_KV_HEADS)]
    k = jnp.concatenate([pieces[g] for g in range(A_KV_HEADS) for _ in range(A_GROUP)], axis=1)
    v = jnp.concatenate([pieces[A_KV_HEADS + g] for g in range(A_KV_HEADS) for _ in range(A_GROUP)], axis=1)
    return k, v


def _win_gqa_kernel(sink_ref, q_ref, *rest, latent):
    if latent:
        kv_ref, ckv_ref, o_ref, k_scr, v_scr, kc_scr, vc_scr = rest
    else:
        ckv_ref, o_ref, kc_scr, vc_scr = rest
    first = pl.program_id(1) == 0 if latent else True
    gw = A_GROUP * HEAD_DIM

    def build():
        kc_scr[...], vc_scr[...] = _repeat_kv_heads(ckv_ref[...])
        if latent:
            k_scr[...], v_scr[...] = _repeat_kv_heads(kv_ref[...])

    if latent:
        pl.when(first)(build)
    else:
        build()

    q = q_ref[...] * ATTN_SCALE
    tq = q.shape[0]
    rows = A_GROUP * tq
    if latent:
        n = pl.program_id(1)
        span = 3 * A_BLOCK
        start = pl.multiple_of(jnp.clip((n - 1) * A_BLOCK, 0, SEQ - span), A_BLOCK)
        qpos = n * A_BLOCK + lax.broadcasted_iota(jnp.int32, (tq, span), 0)
        kpos = start + lax.broadcasted_iota(jnp.int32, (tq, span), 1)
        band = jnp.where(jnp.abs(qpos - kpos) <= A_WINDOW, 0.0, NEG).astype(F32)
        band = jnp.concatenate([band] * A_GROUP, axis=0)
    row_head = lax.broadcasted_iota(jnp.int32, (rows, 1), 0) // tq
    lane_head = _lane_group((tq, gw), HEAD_DIM)
    for g in range(A_KV_HEADS):
        gs = slice(g * gw, (g + 1) * gw)
        qg = q[:, gs]
        zero = jnp.zeros_like(qg)
        q_stack = jnp.concatenate([jnp.where(lane_head == r, qg, zero) for r in range(A_GROUP)], axis=0)
        sink = jnp.zeros((rows, 1), F32)
        for r in range(A_GROUP):
            sink = jnp.where(row_head == r, sink_ref[g * A_GROUP + r], sink)
        segs = [(kc_scr[:, gs], vc_scr[:, gs], None)]
        if latent:
            segs.append((k_scr[pl.ds(start, span), gs], v_scr[pl.ds(start, span), gs], band))
        o, l = _softmax_attend(q_stack, segs, extra_logit=sink)
        o = o / l
        og = jnp.zeros((tq, gw), F32)
        for r in range(A_GROUP):
            og = jnp.where(lane_head == r, o[r * tq:(r + 1) * tq], og)
        o_ref[:, gs] = og.astype(BF16)


def _win_gqa(z, zc, sink, *, latent):
    kvb = 2 * A_KV_HEADS * HEAD_DIM
    kv_col = T_AKV * Z_TILE // kvb
    smem = pl.BlockSpec(memory_space=pltpu.SMEM)
    rep = lambda n: pltpu.VMEM((n, A_HEADS * HEAD_DIM), BF16)
    if latent:
        nq = SEQ // A_BLOCK
        grid = (BATCH, nq)
        in_specs = [
            smem,
            pl.BlockSpec((A_BLOCK, Z_TILE), lambda b, n: (b * nq + n, T_AQ)),
            pl.BlockSpec((SEQ, kvb), lambda b, n: (b, kv_col)),
            pl.BlockSpec((CTX_LEN, kvb), lambda b, n: (b, kv_col)),
        ]
        out_specs = pl.BlockSpec((A_BLOCK, Z_TILE), lambda b, n: (b * nq + n, 0))
        args = (sink, z, z, zc)
        rows = BATCH * SEQ
        sem = ("parallel", "arbitrary")
        scratch = [rep(SEQ), rep(SEQ), rep(CTX_LEN), rep(CTX_LEN)]
    else:
        grid = (BATCH,)
        in_specs = [
            smem,
            pl.BlockSpec((CTX_LEN, Z_TILE), lambda b: (b, T_AQ)),
            pl.BlockSpec((CTX_LEN, kvb), lambda b: (b, kv_col)),
        ]
        out_specs = pl.BlockSpec((CTX_LEN, Z_TILE), lambda b: (b, 0))
        args = (sink, zc, zc)
        rows = BATCH * CTX_LEN
        sem = ("parallel",)
        scratch = [rep(CTX_LEN), rep(CTX_LEN)]
    return pl.pallas_call(
        functools.partial(_win_gqa_kernel, latent=latent),
        grid=grid,
        in_specs=in_specs,
        out_specs=out_specs,
        out_shape=jax.ShapeDtypeStruct((rows, BRANCH_W), BF16),
        scratch_shapes=scratch,
        compiler_params=_cparams(sem),
        name="win_gqa" if latent else "win_gqa_ctx",
    )(*args)


N_ROW_OFFSETS = 2 * B_WIN_ROWS - 1
N_COL_OFFSETS = 2 * B_WIN_COLS - 1
N_PAIR = N_ROW_OFFSETS + 1
NBR_QROWS = 4
NBR_SLAB = NBR_QROWS + B_WIN_ROWS
NBR_GROUPS = GRID_H // NBR_QROWS
NBR_CASES = 3


def _nbr_case_geometry(case):
    if case == 0:
        return B_WIN_ROWS - 1, lambda a, c: 0 <= c < B_WIN_ROWS
    if case == 1:
        return B_WIN_ROWS - 1 - NBR_QROWS, lambda a, c: 0 <= c - a < B_WIN_ROWS
    return B_WIN_ROWS - 1 - 2 * NBR_QROWS, lambda a, c: NBR_QROWS <= c < NBR_SLAB


def _rpb_table_kernel(rpb_ref, o_ref, pair_scr):
    h = pl.program_id(0)
    shape = (GRID_W, LANES)
    qcol = lax.broadcasted_iota(jnp.int32, shape, 0)
    lane = lax.broadcasted_iota(jnp.int32, shape, 1)
    kcol = lane & (GRID_W - 1)
    hi = lane >= GRID_W
    cs = jnp.clip(qcol - B_WIN_COLS // 2, 0, GRID_W - B_WIN_COLS)
    in_window = (kcol >= cs) & (kcol < cs + B_WIN_COLS)
    ci = kcol - qcol + (B_WIN_COLS - 1)
    neg = jnp.full(shape, NEG, F32)

    def rpb(d, c):
        return rpb_ref[(h * N_ROW_OFFSETS + d) * N_COL_OFFSETS + c]

    for p in range(N_PAIR):
        acc = jnp.zeros(shape, F32)
        for c in range(N_COL_OFFSETS):
            hit = ci == c
            if p - 1 >= 0:
                acc = jnp.where(hit & ~hi, rpb(p - 1, c), acc)
            if p < N_ROW_OFFSETS:
                acc = jnp.where(hit & hi, rpb(p, c), acc)
        pair_scr[p] = jnp.where(in_window, acc, NEG)

    for case in range(NBR_CASES):
        shift, in_rows = _nbr_case_geometry(case)
        for a in range(NBR_QROWS):
            for cp in range(NBR_SLAB // 2):
                c = 2 * cp
                lo_ok, hi_ok = in_rows(a, c), in_rows(a, c + 1)
                if lo_ok or hi_ok:
                    piece = pair_scr[c - a + shift + 1]
                    if not lo_ok:
                        piece = jnp.where(hi, piece, neg)
                    if not hi_ok:
                        piece = jnp.where(hi, neg, piece)
                else:
                    piece = neg
                o_ref[case, 0, a * GRID_W:(a + 1) * GRID_W, cp * LANES:(cp + 1) * LANES] = piece


def _rpb_table(rpb):
    nq, nk = NBR_QROWS * GRID_W, NBR_SLAB * GRID_W
    return pl.pallas_call(
        _rpb_table_kernel,
        grid=(B_HEADS,),
        in_specs=[pl.BlockSpec(memory_space=pltpu.SMEM)],
        out_specs=pl.BlockSpec((NBR_CASES, 1, nq, nk), lambda h: (0, h, 0, 0)),
        out_shape=jax.ShapeDtypeStruct((NBR_CASES, B_HEADS, nq, nk), F32),
        scratch_shapes=[pltpu.VMEM((N_PAIR, GRID_W, LANES), F32)],
        compiler_params=_cparams(("parallel",)),
        name="rpb_table",
    )(rpb.reshape(-1))


def _pair_attend(q, lane_hi, segs_of_head):
    zero = jnp.zeros_like(q)
    halves = []
    for hh in range(2):
        qm = jnp.where(lane_hi, q, zero) if hh else jnp.where(lane_hi, zero, q)
        o, l = _softmax_attend(qm, segs_of_head(hh))
        halves.append(o / l)
    return jnp.where(lane_hi, halves[1], halves[0])


def _nbr_kernel(q_ref, k_ref, v_ref, kc_ref, vc_ref, bias_ref, o_ref):
    g = pl.program_id(1)
    r0 = jnp.clip(NBR_QROWS * g - B_WIN_ROWS // 2, 0, GRID_H - NBR_SLAB)
    base = pl.multiple_of(r0 * GRID_W, GRID_W)
    nk = NBR_SLAB * GRID_W
    q = q_ref[...] * ATTN_SCALE
    lane_hi = lax.broadcasted_iota(jnp.int32, (q.shape[0], LANES), 1) >= HEAD_DIM
    for pr in range(B_HEADS // 2):
        sl = slice(pr * LANES, (pr + 1) * LANES)
        kp = k_ref[pl.ds(base, nk), sl]
        vp = v_ref[pl.ds(base, nk), sl]
        kcp = kc_ref[:, sl]
        vcp = vc_ref[:, sl]
        segs = lambda hh: [(kp, vp, bias_ref[0, 2 * pr + hh]), (kcp, vcp, None)]
        o_ref[:, sl] = _pair_attend(q[:, sl], lane_hi, segs).astype(BF16)


def _nbr_attn(z, zc, table):
    nq, nk = NBR_QROWS * GRID_W, NBR_SLAB * GRID_W
    case = lambda g: jnp.where(g == 0, 0, jnp.where(g == NBR_GROUPS - 1, 2, 1))
    return pl.pallas_call(
        _nbr_kernel,
        grid=(BATCH, NBR_GROUPS),
        in_specs=[
            pl.BlockSpec((nq, Z_TILE), lambda b, g: (b * NBR_GROUPS + g, T_BQ)),
            pl.BlockSpec((SEQ, Z_TILE), lambda b, g: (b, T_BK)),
            pl.BlockSpec((SEQ, Z_TILE), lambda b, g: (b, T_BV)),
            pl.BlockSpec((CTX_LEN, Z_TILE), lambda b, g: (b, T_BK)),
            pl.BlockSpec((CTX_LEN, Z_TILE), lambda b, g: (b, T_BV)),
            pl.BlockSpec((1, B_HEADS, nq, nk), lambda b, g: (case(g), 0, 0, 0)),
        ],
        out_specs=pl.BlockSpec((nq, Z_TILE), lambda b, g: (b * NBR_GROUPS + g, 0)),
        out_shape=jax.ShapeDtypeStruct((BATCH * SEQ, BRANCH_W), BF16),
        compiler_params=_cparams(("parallel", "arbitrary")),
        name="nbr_attn",
    )(z, z, z, zc, zc, table)


def _ctx_mha_kernel(q_ref, k_ref, v_ref, o_ref):
    q = q_ref[...] * ATTN_SCALE
    lane_hi = lax.broadcasted_iota(jnp.int32, (q.shape[0], LANES), 1) >= HEAD_DIM
    for pr in range(B_HEADS // 2):
        sl = slice(pr * LANES, (pr + 1) * LANES)
        segs = lambda hh: [(k_ref[:, sl], v_ref[:, sl], None)]
        o_ref[:, sl] = _pair_attend(q[:, sl], lane_hi, segs).astype(BF16)


def _ctx_mha(zc):
    return pl.pallas_call(
        _ctx_mha_kernel,
        grid=(BATCH,),
        in_specs=[
            pl.BlockSpec((CTX_LEN, Z_TILE), lambda b: (b, T_BQ)),
            pl.BlockSpec((CTX_LEN, Z_TILE), lambda b: (b, T_BK)),
            pl.BlockSpec((CTX_LEN, Z_TILE), lambda b: (b, T_BV)),
        ],
        out_specs=pl.BlockSpec((CTX_LEN, Z_TILE), lambda b: (b, 0)),
        out_shape=jax.ShapeDtypeStruct((BATCH * CTX_LEN, BRANCH_W), BF16),
        compiler_params=_cparams(("parallel",)),
        name="nbr_attn_ctx",
    )(zc, zc, zc)


def _pool_kernel(u_ref, w_ref, scale_ref, o_ref):
    n = u_ref.shape[0]
    t = lax.broadcasted_iota(jnp.int32, (n, C_GROUP_DIM), 0)

    def down(a, k):
        return jnp.where(t >= k, pltpu.roll(a, k, axis=0), 0.0)

    def up(a, k):
        return jnp.where(t < n - k, pltpu.roll(a, n - k, axis=0), 0.0)

    for g, w in enumerate(C_WINDOWS):
        sl = slice(g * C_GROUP_DIM, (g + 1) * C_GROUP_DIM)
        u = u_ref[:, sl].astype(F32)
        half = w // 2
        back = u
        fwd = u
        k = 1
        while k < half:
            back = back + down(back, k)
            fwd = fwd + up(fwd, k)
            k *= 2
        total = down(back, 1) + fwd
        lo = jnp.maximum(t - half, 0)
        hi = jnp.minimum(t + half - 1, n - 1)
        cnt = (hi - lo + 1).astype(F32)
        pooled = total / cnt - u
        y = _dot(pooled.astype(BF16), w_ref[0, g]) * scale_ref[:, sl]
        o_ref[:, sl] = y.astype(BF16)


def _pool(z, c_w, c_scale, *, layer, n):
    rows = z.shape[0]
    return pl.pallas_call(
        _pool_kernel,
        grid=(rows // n,),
        in_specs=[
            pl.BlockSpec((n, Z_TILE), lambda b: (b, T_CU)),
            pl.BlockSpec((1, len(C_WINDOWS), C_GROUP_DIM, C_GROUP_DIM), lambda b: (layer, 0, 0, 0)),
            pl.BlockSpec((1, BRANCH_W), lambda b: (0, 0)),
        ],
        out_specs=pl.BlockSpec((n, BRANCH_W), lambda b: (b, 0)),
        out_shape=jax.ShapeDtypeStruct((rows, BRANCH_W), BF16),
        compiler_params=_cparams(("parallel",)),
        name=f"pool_{n}",
    )(z, c_w, c_scale)


def _diff_kernel(lam_ref, q_ref, *rest, lam_init, latent):
    if latent:
        k_ref, v_ref, kc_ref, vc_ref, g_ref, o_ref = rest
    else:
        kc_ref, vc_ref, g_ref, o_ref = rest
    dl = lam_ref[...]
    lam = (jnp.exp(jnp.sum(dl[0:1] * dl[1:2], axis=-1, keepdims=True))
           - jnp.exp(jnp.sum(dl[2:3] * dl[3:4], axis=-1, keepdims=True)) + lam_init)
    q = q_ref[...] * ATTN_SCALE
    tq = q.shape[0]
    hw = 2 * HEAD_DIM
    lane_hi = lax.broadcasted_iota(jnp.int32, (tq, hw), 1) >= HEAD_DIM
    for h in range(D_HEADS):
        sl = slice(h * hw, (h + 1) * hw)
        qh = q[:, sl]
        zero = jnp.zeros_like(qh)
        q_stack = jnp.concatenate([jnp.where(lane_hi, zero, qh), jnp.where(lane_hi, qh, zero)], axis=0)
        segs = [(kc_ref[:, sl], vc_ref[:, sl], None)]
        if latent:
            segs.append((k_ref[:, sl], v_ref[:, sl], None))
        o, l = _softmax_attend(q_stack, segs)
        o = o / l
        od = o[:tq] - lam * o[tq:]
        ms = jnp.mean(od * od, axis=-1, keepdims=True)
        y = od * lax.rsqrt(ms + EPS) * g_ref[...] * (1.0 - lam_init)
        o_ref[:, sl] = y.astype(BF16)


def _diff_attn(z, zc, d_lambda, subln, *, lam_init, latent):
    full = lambda shape: pl.BlockSpec(shape, lambda *_: (0,) * len(shape))
    if latent:
        tq = 256
        nq = SEQ // tq
        grid = (BATCH, nq)
        in_specs = [
            full((4, HEAD_DIM)),
            pl.BlockSpec((tq, Z_TILE), lambda b, n: (b * nq + n, T_DQ)),
            pl.BlockSpec((SEQ, Z_TILE), lambda b, n: (b, T_DK)),
            pl.BlockSpec((SEQ, Z_TILE), lambda b, n: (b, T_DV)),
            pl.BlockSpec((CTX_LEN, Z_TILE), lambda b, n: (b, T_DK)),
            pl.BlockSpec((CTX_LEN, Z_TILE), lambda b, n: (b, T_DV)),
            full((1, 2 * HEAD_DIM)),
        ]
        out_specs = pl.BlockSpec((tq, Z_TILE), lambda b, n: (b * nq + n, 0))
        args = (d_lambda, z, z, z, zc, zc, subln)
        rows = BATCH * SEQ
        sem = ("parallel", "arbitrary")
    else:
        grid = (BATCH,)
        in_specs = [
            full((4, HEAD_DIM)),
            pl.BlockSpec((CTX_LEN, Z_TILE), lambda b: (b, T_DQ)),
            pl.BlockSpec((CTX_LEN, Z_TILE), lambda b: (b, T_DK)),
            pl.BlockSpec((CTX_LEN, Z_TILE), lambda b: (b, T_DV)),
            full((1, 2 * HEAD_DIM)),
        ]
        out_specs = pl.BlockSpec((CTX_LEN, Z_TILE), lambda b: (b, 0))
        args = (d_lambda, zc, zc, zc, subln)
        rows = BATCH * CTX_LEN
        sem = ("parallel",)
    return pl.pallas_call(
        functools.partial(_diff_kernel, lam_init=lam_init, latent=latent),
        grid=grid,
        in_specs=in_specs,
        out_specs=out_specs,
        out_shape=jax.ShapeDtypeStruct((rows, BRANCH_W), BF16),
        compiler_params=_cparams(sem),
        name="diff_attn" if latent else "diff_attn_ctx",
    )(*args)


def _merge_kernel(ya_ref, yb_ref, yc_ref, yd_ref, g_ref, wb_ref, wo_ref, x_ref, gt_ref, o_ref, acc_scr):
    k = pl.program_id(1)
    y_refs = (ya_ref, yb_ref, yc_ref, yd_ref)

    for kk in range(N_BRANCH):
        @pl.when(k == kk)
        def _(kk=kk):
            proj = _dot(y_refs[kk][...], wb_ref[0, 0]) * g_ref[...].astype(F32)
            if kk == 0:
                acc_scr[...] = proj
            else:
                acc_scr[...] += proj

    @pl.when(k == N_BRANCH - 1)
    def _():
        out = _dot(acc_scr[...].astype(BF16), wo_ref[0])
        o_ref[...] = x_ref[...] + gt_ref[0] * out


def _merge(ys, z, w_branch, w_out, x, mod3, *, layer, tm, mod_row):
    m = x.shape[0]
    y_spec = pl.BlockSpec((tm, BRANCH_W), lambda i, k: (i, 0))
    return pl.pallas_call(
        _merge_kernel,
        grid=(m // tm, N_BRANCH),
        in_specs=[
            y_spec, y_spec, y_spec, y_spec,
            pl.BlockSpec((tm, D_MODEL), lambda i, k: (i, k)),
            pl.BlockSpec((1, 1, BRANCH_W, D_MODEL), lambda i, k: (layer, k, 0, 0)),
            pl.BlockSpec((1, D_MODEL, D_MODEL), lambda i, k: (layer, 0, 0)),
            pl.BlockSpec((tm, D_MODEL), lambda i, k: (i, 0)),
            pl.BlockSpec((1, 1, D_MODEL), lambda i, k: (mod_row(i), 0, 2)),
        ],
        out_specs=pl.BlockSpec((tm, D_MODEL), lambda i, k: (i, 0)),
        out_shape=jax.ShapeDtypeStruct((m, D_MODEL), F32),
        scratch_shapes=[pltpu.VMEM((tm, D_MODEL), F32)],
        compiler_params=_cparams(("parallel", "arbitrary")),
        name=f"merge_{m}",
    )(*ys, z, w_branch, w_out, x, mod3)


def _mlp_kernel(x_ref, gn_ref, shift_ref, scale_ref, gt_ref, w1_ref, w2_ref, o_ref, h_scr, acc_scr):
    f = pl.program_id(1)

    @pl.when(f == 0)
    def _():
        h = _norm_modulate(x_ref[...], gn_ref[...], shift_ref[0], scale_ref[0])
        h_scr[...] = h.astype(BF16)

    a = jnp.maximum(_dot(h_scr[...], w1_ref[0]), 0.0)
    part = _dot((a * a).astype(BF16), w2_ref[0])

    @pl.when(f == 0)
    def _():
        acc_scr[...] = part

    @pl.when(f > 0)
    def _():
        acc_scr[...] += part

    @pl.when(f == pl.num_programs(1) - 1)
    def _():
        o_ref[...] = x_ref[...] + gt_ref[0] * acc_scr[...]


def _mlp(x, gn, mod3, w1, w2, *, layer, tm, tf, mod_row):
    m = x.shape[0]
    return pl.pallas_call(
        _mlp_kernel,
        grid=(m // tm, D_FF // tf),
        in_specs=[
            pl.BlockSpec((tm, D_MODEL), lambda i, f: (i, 0)),
            pl.BlockSpec((1, D_MODEL), lambda i, f: (0, 0)),
            pl.BlockSpec((1, 1, D_MODEL), lambda i, f: (mod_row(i), 0, 3)),
            pl.BlockSpec((1, 1, D_MODEL), lambda i, f: (mod_row(i), 0, 4)),
            pl.BlockSpec((1, 1, D_MODEL), lambda i, f: (mod_row(i), 0, 5)),
            pl.BlockSpec((1, D_MODEL, tf), lambda i, f: (layer, 0, f)),
            pl.BlockSpec((1, tf, D_MODEL), lambda i, f: (layer, f, 0)),
        ],
        out_specs=pl.BlockSpec((tm, D_MODEL), lambda i, f: (i, 0)),
        out_shape=jax.ShapeDtypeStruct((m, D_MODEL), F32),
        scratch_shapes=[pltpu.VMEM((tm, D_MODEL), BF16), pltpu.VMEM((tm, D_MODEL), F32)],
        compiler_params=_cparams(("parallel", "arbitrary")),
        name=f"mlp_{m}",
    )(x, gn, mod3, mod3, mod3, w1, w2)


def _rope_tables():
    t = jnp.arange(SEQ)
    row = (t // GRID_W).astype(F32)
    col = (t % GRID_W).astype(F32)
    n_freq = HEAD_DIM // 4
    inv = ROPE_BASE ** (-jnp.arange(n_freq, dtype=F32) / n_freq)
    ar = row[:, None] * inv
    ac = col[:, None] * inv
    cos_h = jnp.concatenate([jnp.cos(ar), jnp.cos(ar), jnp.cos(ac), jnp.cos(ac)], axis=-1)
    sin_h = jnp.concatenate([-jnp.sin(ar), jnp.sin(ar), -jnp.sin(ac), jnp.sin(ac)], axis=-1)
    return jnp.tile(cos_h, (1, LANES // HEAD_DIM)), jnp.tile(sin_h, (1, LANES // HEAD_DIM))


def _bias_and_gain_rows(b_gate_l, qk_gain_l):
    head_cols = Z_WIDTH - N_BRANCH * D_MODEL
    bias = jnp.concatenate([b_gate_l, jnp.zeros((head_cols,), F32)])[None, :]
    ones = jnp.ones((Z_TILE,), F32)
    rep = lambda g: jnp.tile(g, Z_TILE // HEAD_DIM)
    akv_gain = jnp.concatenate([jnp.tile(qk_gain_l[0, 1], A_KV_HEADS), jnp.ones((Z_TILE - A_KV_HEADS * HEAD_DIM,), F32)])
    gain = jnp.concatenate([
        jnp.ones((N_BRANCH * D_MODEL,), F32),
        rep(qk_gain_l[0, 0]), akv_gain,
        rep(qk_gain_l[1, 0]), rep(qk_gain_l[1, 1]), ones, ones,
        rep(qk_gain_l[2, 0]), rep(qk_gain_l[2, 1]), ones,
    ])[None, :]
    return bias, gain


def kernel(x, c, ctx, c_ctx, w_ada, b_ada, g_norm1, g_norm2, w_in, b_gate, qk_gain, a_sink, b_rpb,
           c_w, c_scale, d_lambda, d_subln, w_branch, w_out, w_ff1, w_ff2):
    cos_t, sin_t = _rope_tables()
    lane = jnp.arange(LANES)
    bd = (lane[:, None] // HEAD_DIM == lane[None, :] // HEAD_DIM).astype(BF16)

    cc = jnp.concatenate([c, c_ctx[None, :], jnp.zeros((8 - BATCH - 1, D_MODEL), F32)], axis=0)
    mod_all = _modulation(cc, w_ada, b_ada)

    w_p = _w_in_prep(w_in)
    wb = w_branch.astype(BF16)
    wo = w_out.astype(BF16)
    w1 = w_ff1.astype(BF16)
    w2 = w_ff2.astype(BF16)
    cw = c_w.astype(BF16)

    xl = x.reshape(BATCH * SEQ, D_MODEL)
    xc = ctx.reshape(BATCH * CTX_LEN, D_MODEL)
    ctx_row = lambda i: BATCH

    for l in range(DEPTH):
        last = l == DEPTH - 1
        lam_init = 0.8 - 0.6 * math.exp(-0.3 * l)
        mod3 = mod_all[l].reshape(8, 1, 6 * D_MODEL)
        bias_p, gain_p = _bias_and_gain_rows(b_gate[l], qk_gain[l])
        gn1 = g_norm1[l][None, :]
        gn2 = g_norm2[l][None, :]
        cs = c_scale[l][None, :]
        subln = d_subln[l][None, :]

        tm = 1024
        z = _inproj(xl, gn1, mod3, w_p, bias_p, gain_p, cos_t, sin_t, bd,
                    layer=l, tm=tm, rope=True, mod_row=lambda i: i // (SEQ // 1024))
        zc = _inproj(xc, gn1, mod3, w_p, bias_p, gain_p, cos_t, sin_t, bd,
                     layer=l, tm=BATCH * CTX_LEN, rope=False, mod_row=ctx_row)

        table = _rpb_table(b_rpb[l])
        ya = _win_gqa(z, zc, a_sink[l], latent=True)
        yb = _nbr_attn(z, zc, table)
        yc = _pool(z, cw, cs, layer=l, n=SEQ)
        yd = _diff_attn(z, zc, d_lambda[l], subln, lam_init=lam_init, latent=True)
        tmm = 256
        xl = _merge((ya, yb, yc, yd), z, wb, wo, xl, mod3, layer=l, tm=tmm, mod_row=lambda i: i // (SEQ // 256))
        if not last:
            yac = _win_gqa(z, zc, a_sink[l], latent=False)
            ybc = _ctx_mha(zc)
            ycc = _pool(zc, cw, cs, layer=l, n=CTX_LEN)
            ydc = _diff_attn(z, zc, d_lambda[l], subln, lam_init=lam_init, latent=False)
            xc = _merge((yac, ybc, ycc, ydc), zc, wb, wo, xc, mod3, layer=l, tm=tmm, mod_row=ctx_row)

        tmf = 512
        xl = _mlp(xl, gn2, mod3, w1, w2, layer=l, tm=tmf, tf=1024, mod_row=lambda i: i // (SEQ // 512))
        if not last:
            xc = _mlp(xc, gn2, mod3, w1, w2, layer=l, tm=tmf, tf=1024, mod_row=ctx_row)

    return xl.reshape(BATCH, SEQ, D_MODEL)
```

```python
import functools
import math

import jax
import jax.numpy as jnp
from jax import lax
from jax.experimental import pallas as pl
from jax.experimental.pallas import tpu as pltpu

D_MODEL = 2048
BATCH = 4
SEQ = 2048
DEPTH = 2
GRID_W = 64
GRID_H = SEQ // GRID_W
CTX_LEN = 256
HEAD_DIM = 64
N_BRANCH = 4
BRANCH_W = D_MODEL // N_BRANCH
A_HEADS = 8
A_KV_HEADS = 2
A_GROUP = A_HEADS // A_KV_HEADS
A_WINDOW = 128
A_BLOCK = 128
B_HEADS = 8
B_WIN_ROWS = 8
B_WIN_COLS = 16
C_WINDOWS = (2, 4, 8, 16)
C_GROUP_DIM = 128
D_HEADS = 4
D_FF = 4 * D_MODEL
ROPE_BASE = 10000.0
EPS = 1e-6
NEG = -1e30
ATTN_SCALE = HEAD_DIM ** -0.5

F32 = jnp.float32
BF16 = jnp.bfloat16

V7X_VMEM_LIMIT_BYTES = 56 * 1024 * 1024
LANES = 128

Z_TILE = 512
GATE_TILES = N_BRANCH * D_MODEL // Z_TILE
T_AQ, T_AKV, T_BQ, T_BK, T_BV, T_CU, T_DQ, T_DK, T_DV = range(GATE_TILES, GATE_TILES + 9)
Z_TILES = GATE_TILES + 9
Z_WIDTH = Z_TILES * Z_TILE
W_IN_COLS = 12544
W_SRC = 256


def _cparams(sem):
    return pltpu.CompilerParams(dimension_semantics=sem, vmem_limit_bytes=V7X_VMEM_LIMIT_BYTES)


def _dot_nt(a, b):
    return lax.dot_general(a, b, (((1,), (1,)), ((), ())), preferred_element_type=F32)


def _dot(a, b):
    return jnp.dot(a, b, preferred_element_type=F32)


def _mod_kernel(c_ref, w_ref, b_ref, o_ref):
    c = c_ref[...]
    s = c * (0.5 * (jnp.tanh(0.5 * c) + 1.0))
    o_ref[0] = _dot(s, w_ref[0]) + b_ref[0]


def _modulation(cc, w_ada, b_ada):
    tn = 1024
    n = 6 * D_MODEL
    return pl.pallas_call(
        _mod_kernel,
        grid=(DEPTH, n // tn),
        in_specs=[
            pl.BlockSpec((8, D_MODEL), lambda l, j: (0, 0)),
            pl.BlockSpec((1, D_MODEL, tn), lambda l, j: (l, 0, j)),
            pl.BlockSpec((1, 1, tn), lambda l, j: (l, 0, j)),
        ],
        out_specs=pl.BlockSpec((1, 8, tn), lambda l, j: (l, 0, j)),
        out_shape=jax.ShapeDtypeStruct((DEPTH, 8, n), F32),
        compiler_params=_cparams(("parallel", "parallel")),
        name="modulation",
    )(cc, w_ada, b_ada.reshape(DEPTH, 1, n))


def _w_in_src_block(jb):
    n_gate = N_BRANCH * D_MODEL // W_SRC
    first_gate = (W_IN_COLS - N_BRANCH * D_MODEL) // W_SRC
    return jnp.where(jb < n_gate, jb + first_gate,
                     jnp.where(jb < n_gate + 3, jb - n_gate,
                               jnp.where(jb == n_gate + 3, 2, jb - n_gate - 1)))


def _w_in_prep_kernel(a_ref, b_ref, o_ref):
    o_ref[0, :, :W_SRC] = a_ref[0].astype(BF16)
    o_ref[0, :, W_SRC:] = b_ref[0].astype(BF16)


def _w_in_prep(w_in):
    src = lambda half: pl.BlockSpec((1, D_MODEL, W_SRC), lambda l, j: (l, 0, _w_in_src_block(2 * j + half)))
    return pl.pallas_call(
        _w_in_prep_kernel,
        grid=(DEPTH, Z_TILES),
        in_specs=[src(0), src(1)],
        out_specs=pl.BlockSpec((1, D_MODEL, Z_TILE), lambda l, j: (l, 0, j)),
        out_shape=jax.ShapeDtypeStruct((DEPTH, D_MODEL, Z_WIDTH), BF16),
        compiler_params=_cparams(("parallel", "parallel")),
        name="w_in_prep",
    )(w_in, w_in)


NORM_ROWS = 16


def _norm_modulate_rows(x_ref, gn_ref, shift_ref, scale_ref, h_ref):
    n = x_ref.shape[0]
    gs = gn_ref[...] * (1.0 + scale_ref[0])
    shift = shift_ref[0]

    def body(c, carry):
        rows = pl.ds(pl.multiple_of(c * NORM_ROWS, NORM_ROWS), NORM_ROWS)
        x = x_ref[rows, :]
        inv = lax.rsqrt(jnp.mean(x * x, axis=-1, keepdims=True) + EPS)
        h_ref[rows, :] = ((x * inv) * gs + shift).astype(BF16)
        return carry

    lax.fori_loop(0, n // NORM_ROWS, body, 0, unroll=8)


def _norm_mod_kernel(x_ref, gn_ref, shift_ref, scale_ref, h_ref):
    _norm_modulate_rows(x_ref, gn_ref, shift_ref, scale_ref, h_ref)


def _norm_mod(x, gn, mod3, *, tm, chunk, mod_row):
    m = x.shape[0]
    return pl.pallas_call(
        _norm_mod_kernel,
        grid=(m // tm,),
        in_specs=[
            pl.BlockSpec((tm, D_MODEL), lambda i: (i, 0)),
            pl.BlockSpec((1, D_MODEL), lambda i: (0, 0)),
            pl.BlockSpec((1, 1, D_MODEL), lambda i: (mod_row(i), 0, chunk)),
            pl.BlockSpec((1, 1, D_MODEL), lambda i: (mod_row(i), 0, chunk + 1)),
        ],
        out_specs=pl.BlockSpec((tm, D_MODEL), lambda i: (i, 0)),
        out_shape=jax.ShapeDtypeStruct((m, D_MODEL), BF16),
        compiler_params=_cparams(("parallel",)),
        name=f"norm_mod_{m}",
    )(x, gn, mod3, mod3)


def _swap16(y):
    lane = lax.broadcasted_iota(jnp.int32, y.shape, 1)
    fwd = pltpu.roll(y, LANES - 16, axis=1)
    bwd = pltpu.roll(y, 16, axis=1)
    return jnp.where((lane & 16) == 0, fwd, bwd)


def _inproj_kernel(h_ref, w_ref, bias_ref, gain_ref, cos_ref, sin_ref, bd_ref, z_ref, *, rope):
    j = pl.program_id(1)

    def project():
        return _dot(h_ref[...], w_ref[0])

    def head_norm(a, gain, with_rope):
        ms = _dot((a * a).astype(BF16), bd_ref[...]) * (1.0 / HEAD_DIM)
        y = a * lax.rsqrt(ms + EPS) * gain
        if with_rope:
            y = y * cos_ref[...] + _swap16(y) * sin_ref[...]
        return y

    def norm_tile(n_norm_chunks, with_rope):
        acc = project()
        for c in range(Z_TILE // LANES):
            sl = slice(c * LANES, (c + 1) * LANES)
            a = acc[:, sl]
            if c < n_norm_chunks:
                a = head_norm(a, gain_ref[:, sl], with_rope)
            z_ref[:, sl] = a.astype(BF16)

    @pl.when(j < GATE_TILES)
    def _():
        t = project() + bias_ref[...]
        z_ref[...] = (0.5 * (jnp.tanh(0.5 * t) + 1.0)).astype(BF16)

    @pl.when((j == T_AQ) | (j == T_DQ) | (j == T_DK))
    def _():
        norm_tile(4, rope)

    @pl.when(j == T_AKV)
    def _():
        norm_tile(1, rope)

    @pl.when((j == T_BQ) | (j == T_BK))
    def _():
        norm_tile(4, False)

    @pl.when((j == T_BV) | (j == T_CU) | (j == T_DV))
    def _():
        z_ref[...] = project().astype(BF16)


def _inproj(h, w_p, bias_p, gain_p, cos_t, sin_t, bd, *, layer, tm, rope):
    m = h.shape[0]
    kernel = functools.partial(_inproj_kernel, rope=rope)
    tiles_per_seq = SEQ // tm if rope else 1
    return pl.pallas_call(
        kernel,
        grid=(m // tm, Z_TILES),
        in_specs=[
            pl.BlockSpec((tm, D_MODEL), lambda i, j: (i, 0)),
            pl.BlockSpec((1, D_MODEL, Z_TILE), lambda i, j: (layer, 0, j)),
            pl.BlockSpec((1, Z_TILE), lambda i, j: (0, j)),
            pl.BlockSpec((1, Z_TILE), lambda i, j: (0, j)),
            pl.BlockSpec((tm, LANES), lambda i, j: (i % tiles_per_seq, 0)),
            pl.BlockSpec((tm, LANES), lambda i, j: (i % tiles_per_seq, 0)),
            pl.BlockSpec((LANES, LANES), lambda i, j: (0, 0)),
        ],
        out_specs=pl.BlockSpec((tm, Z_TILE), lambda i, j: (i, j)),
        out_shape=jax.ShapeDtypeStruct((m, Z_WIDTH), BF16),
        compiler_params=_cparams(("parallel", "arbitrary")),
        name="inproj_rope" if rope else "inproj_ctx",
    )(h, w_p, bias_p, gain_p, cos_t, sin_t, bd)


def _softmax_attend(q, segs, extra_logit=None):
    scores = []
    for k, _, bias in segs:
        s = _dot_nt(q, k)
        if bias is not None:
            s = s + bias
        scores.append(s)
    m = scores[0].max(axis=-1, keepdims=True)
    for s in scores[1:]:
        m = jnp.maximum(m, s.max(axis=-1, keepdims=True))
    if extra_logit is not None:
        m = jnp.maximum(m, extra_logit)
    l = None
    o = None
    for s, (_, v, _) in zip(scores, segs):
        p = jnp.exp(s - m)
        ls = p.sum(axis=-1, keepdims=True)
        os_ = _dot(p.astype(BF16), v)
        l = ls if l is None else l + ls
        o = os_ if o is None else o + os_
    if extra_logit is not None:
        l = l + jnp.exp(extra_logit - m)
    return o, l


def _lane_group(shape, width):
    return lax.broadcasted_iota(jnp.int32, shape, 1) // width


def _repeat_kv_heads(kv):
    pieces = [kv[:, i * HEAD_DIM:(i + 1) * HEAD_DIM] for i in range(2 * A_KV_HEADS)]
    k = jnp.concatenate([pieces[g] for g in range(A_KV_HEADS) for _ in range(A_GROUP)], axis=1)
    v = jnp.concatenate([pieces[A_KV_HEADS + g] for g in range(A_KV_HEADS) for _ in range(A_GROUP)], axis=1)
    return k, v


def _win_gqa_kernel(sink_ref, q_ref, *rest, latent):
    if latent:
        kv_ref, ckv_ref, o_ref, k_scr, v_scr, kc_scr, vc_scr = rest
    else:
        ckv_ref, o_ref, kc_scr, vc_scr = rest
    first = pl.program_id(1) == 0 if latent else True
    gw = A_GROUP * HEAD_DIM

    def build():
        kc_scr[...], vc_scr[...] = _repeat_kv_heads(ckv_ref[...])
        if latent:
            k_scr[...], v_scr[...] = _repeat_kv_heads(kv_ref[...])

    if latent:
        pl.when(first)(build)
    else:
        build()

    q = q_ref[...] * ATTN_SCALE
    tq = q.shape[0]
    rows = A_GROUP * tq
    if latent:
        n = pl.program_id(1)
        span = 3 * A_BLOCK
        start = pl.multiple_of(jnp.clip((n - 1) * A_BLOCK, 0, SEQ - span), A_BLOCK)
        qpos = n * A_BLOCK + lax.broadcasted_iota(jnp.int32, (tq, span), 0)
        kpos = start + lax.broadcasted_iota(jnp.int32, (tq, span), 1)
        band = jnp.where(jnp.abs(qpos - kpos) <= A_WINDOW, 0.0, NEG).astype(F32)
        band = jnp.concatenate([band] * A_GROUP, axis=0)
    row_head = lax.broadcasted_iota(jnp.int32, (rows, 1), 0) // tq
    lane_head = _lane_group((tq, gw), HEAD_DIM)
    for g in range(A_KV_HEADS):
        gs = slice(g * gw, (g + 1) * gw)
        qg = q[:, gs]
        zero = jnp.zeros_like(qg)
        q_stack = jnp.concatenate([jnp.where(lane_head == r, qg, zero) for r in range(A_GROUP)], axis=0)
        sink = jnp.zeros((rows, 1), F32)
        for r in range(A_GROUP):
            sink = jnp.where(row_head == r, sink_ref[g * A_GROUP + r], sink)
        segs = [(kc_scr[:, gs], vc_scr[:, gs], None)]
        if latent:
            segs.append((k_scr[pl.ds(start, span), gs], v_scr[pl.ds(start, span), gs], band))
        o, l = _softmax_attend(q_stack, segs, extra_logit=sink)
        o = o / l
        og = jnp.zeros((tq, gw), F32)
        for r in range(A_GROUP):
            og = jnp.where(lane_head == r, o[r * tq:(r + 1) * tq], og)
        o_ref[:, gs] = og.astype(BF16)


def _win_gqa(z, zc, sink, *, latent):
    kvb = 2 * A_KV_HEADS * HEAD_DIM
    kv_col = T_AKV * Z_TILE // kvb
    smem = pl.BlockSpec(memory_space=pltpu.SMEM)
    rep = lambda n: pltpu.VMEM((n, A_HEADS * HEAD_DIM), BF16)
    if latent:
        nq = SEQ // A_BLOCK
        grid = (BATCH, nq)
        in_specs = [
            smem,
            pl.BlockSpec((A_BLOCK, Z_TILE), lambda b, n: (b * nq + n, T_AQ)),
            pl.BlockSpec((SEQ, kvb), lambda b, n: (b, kv_col)),
            pl.BlockSpec((CTX_LEN, kvb), lambda b, n: (b, kv_col)),
        ]
        out_specs = pl.BlockSpec((A_BLOCK, Z_TILE), lambda b, n: (b * nq + n, 0))
        args = (sink, z, z, zc)
        rows = BATCH * SEQ
        sem = ("parallel", "arbitrary")
        scratch = [rep(SEQ), rep(SEQ), rep(CTX_LEN), rep(CTX_LEN)]
    else:
        grid = (BATCH,)
        in_specs = [
            smem,
            pl.BlockSpec((CTX_LEN, Z_TILE), lambda b: (b, T_AQ)),
            pl.BlockSpec((CTX_LEN, kvb), lambda b: (b, kv_col)),
        ]
        out_specs = pl.BlockSpec((CTX_LEN, Z_TILE), lambda b: (b, 0))
        args = (sink, zc, zc)
        rows = BATCH * CTX_LEN
        sem = ("parallel",)
        scratch = [rep(CTX_LEN), rep(CTX_LEN)]
    return pl.pallas_call(
        functools.partial(_win_gqa_kernel, latent=latent),
        grid=grid,
        in_specs=in_specs,
        out_specs=out_specs,
        out_shape=jax.ShapeDtypeStruct((rows, BRANCH_W), BF16),
        scratch_shapes=scratch,
        compiler_params=_cparams(sem),
        name="win_gqa" if latent else "win_gqa_ctx",
    )(*args)


N_ROW_OFFSETS = 2 * B_WIN_ROWS - 1
N_COL_OFFSETS = 2 * B_WIN_COLS - 1
N_PAIR = N_ROW_OFFSETS + 1
NBR_QROWS = 4
NBR_SLAB = NBR_QROWS + B_WIN_ROWS
NBR_GROUPS = GRID_H // NBR_QROWS
NBR_CASES = 3


def _nbr_case_geometry(case):
    if case == 0:
        return B_WIN_ROWS - 1, lambda a, c: 0 <= c < B_WIN_ROWS
    if case == 1:
        return B_WIN_ROWS - 1 - NBR_QROWS, lambda a, c: 0 <= c - a < B_WIN_ROWS
    return B_WIN_ROWS - 1 - 2 * NBR_QROWS, lambda a, c: NBR_QROWS <= c < NBR_SLAB


def _rpb_table_kernel(rpb_ref, o_ref, pair_scr):
    h = pl.program_id(0)
    shape = (GRID_W, LANES)
    qcol = lax.broadcasted_iota(jnp.int32, shape, 0)
    lane = lax.broadcasted_iota(jnp.int32, shape, 1)
    kcol = lane & (GRID_W - 1)
    hi = lane >= GRID_W
    cs = jnp.clip(qcol - B_WIN_COLS // 2, 0, GRID_W - B_WIN_COLS)
    in_window = (kcol >= cs) & (kcol < cs + B_WIN_COLS)
    ci = kcol - qcol + (B_WIN_COLS - 1)
    neg = jnp.full(shape, NEG, F32)

    def rpb(d, c):
        return rpb_ref[(h * N_ROW_OFFSETS + d) * N_COL_OFFSETS + c]

    for p in range(N_PAIR):
        acc = jnp.zeros(shape, F32)
        for c in range(N_COL_OFFSETS):
            hit = ci == c
            if p - 1 >= 0:
                acc = jnp.where(hit & ~hi, rpb(p - 1, c), acc)
            if p < N_ROW_OFFSETS:
                acc = jnp.where(hit & hi, rpb(p, c), acc)
        pair_scr[p] = jnp.where(in_window, acc, NEG)

    for case in range(NBR_CASES):
        shift, in_rows = _nbr_case_geometry(case)
        for a in range(NBR_QROWS):
            for cp in range(NBR_SLAB // 2):
                c = 2 * cp
                lo_ok, hi_ok = in_rows(a, c), in_rows(a, c + 1)
                if lo_ok or hi_ok:
                    piece = pair_scr[c - a + shift + 1]
                    if not lo_ok:
                        piece = jnp.where(hi, piece, neg)
                    if not hi_ok:
                        piece = jnp.where(hi, neg, piece)
                else:
                    piece = neg
                o_ref[case, 0, a * GRID_W:(a + 1) * GRID_W, cp * LANES:(cp + 1) * LANES] = piece


def _rpb_table(rpb):
    nq, nk = NBR_QROWS * GRID_W, NBR_SLAB * GRID_W
    return pl.pallas_call(
        _rpb_table_kernel,
        grid=(B_HEADS,),
        in_specs=[pl.BlockSpec(memory_space=pltpu.SMEM)],
        out_specs=pl.BlockSpec((NBR_CASES, 1, nq, nk), lambda h: (0, h, 0, 0)),
        out_shape=jax.ShapeDtypeStruct((NBR_CASES, B_HEADS, nq, nk), F32),
        scratch_shapes=[pltpu.VMEM((N_PAIR, GRID_W, LANES), F32)],
        compiler_params=_cparams(("parallel",)),
        name="rpb_table",
    )(rpb.reshape(-1))


def _pair_attend(q, lane_hi, segs_of_head):
    zero = jnp.zeros_like(q)
    halves = []
    for hh in range(2):
        qm = jnp.where(lane_hi, q, zero) if hh else jnp.where(lane_hi, zero, q)
        o, l = _softmax_attend(qm, segs_of_head(hh))
        halves.append(o / l)
    return jnp.where(lane_hi, halves[1], halves[0])


def _nbr_kernel(q_ref, k_ref, v_ref, kc_ref, vc_ref, bias_ref, o_ref):
    g = pl.program_id(1)
    r0 = jnp.clip(NBR_QROWS * g - B_WIN_ROWS // 2, 0, GRID_H - NBR_SLAB)
    base = pl.multiple_of(r0 * GRID_W, GRID_W)
    nk = NBR_SLAB * GRID_W
    q = q_ref[...] * ATTN_SCALE
    lane_hi = lax.broadcasted_iota(jnp.int32, (q.shape[0], LANES), 1) >= HEAD_DIM
    for pr in range(B_HEADS // 2):
        sl = slice(pr * LANES, (pr + 1) * LANES)
        kp = k_ref[pl.ds(base, nk), sl]
        vp = v_ref[pl.ds(base, nk), sl]
        kcp = kc_ref[:, sl]
        vcp = vc_ref[:, sl]
        segs = lambda hh: [(kp, vp, bias_ref[0, 2 * pr + hh]), (kcp, vcp, None)]
        o_ref[:, sl] = _pair_attend(q[:, sl], lane_hi, segs).astype(BF16)


def _nbr_attn(z, zc, table):
    nq, nk = NBR_QROWS * GRID_W, NBR_SLAB * GRID_W
    case = lambda g: jnp.where(g == 0, 0, jnp.where(g == NBR_GROUPS - 1, 2, 1))
    return pl.pallas_call(
        _nbr_kernel,
        grid=(BATCH, NBR_GROUPS),
        in_specs=[
            pl.BlockSpec((nq, Z_TILE), lambda b, g: (b * NBR_GROUPS + g, T_BQ)),
            pl.BlockSpec((SEQ, Z_TILE), lambda b, g: (b, T_BK)),
            pl.BlockSpec((SEQ, Z_TILE), lambda b, g: (b, T_BV)),
            pl.BlockSpec((CTX_LEN, Z_TILE), lambda b, g: (b, T_BK)),
            pl.BlockSpec((CTX_LEN, Z_TILE), lambda b, g: (b, T_BV)),
            pl.BlockSpec((1, B_HEADS, nq, nk), lambda b, g: (case(g), 0, 0, 0)),
        ],
        out_specs=pl.BlockSpec((nq, Z_TILE), lambda b, g: (b * NBR_GROUPS + g, 0)),
        out_shape=jax.ShapeDtypeStruct((BATCH * SEQ, BRANCH_W), BF16),
        compiler_params=_cparams(("parallel", "arbitrary")),
        name="nbr_attn",
    )(z, z, z, zc, zc, table)


def _ctx_mha_kernel(q_ref, k_ref, v_ref, o_ref):
    q = q_ref[...] * ATTN_SCALE
    lane_hi = lax.broadcasted_iota(jnp.int32, (q.shape[0], LANES), 1) >= HEAD_DIM
    for pr in range(B_HEADS // 2):
        sl = slice(pr * LANES, (pr + 1) * LANES)
        segs = lambda hh: [(k_ref[:, sl], v_ref[:, sl], None)]
        o_ref[:, sl] = _pair_attend(q[:, sl], lane_hi, segs).astype(BF16)


def _ctx_mha(zc):
    return pl.pallas_call(
        _ctx_mha_kernel,
        grid=(BATCH,),
        in_specs=[
            pl.BlockSpec((CTX_LEN, Z_TILE), lambda b: (b, T_BQ)),
            pl.BlockSpec((CTX_LEN, Z_TILE), lambda b: (b, T_BK)),
            pl.BlockSpec((CTX_LEN, Z_TILE), lambda b: (b, T_BV)),
        ],
        out_specs=pl.BlockSpec((CTX_LEN, Z_TILE), lambda b: (b, 0)),
        out_shape=jax.ShapeDtypeStruct((BATCH * CTX_LEN, BRANCH_W), BF16),
        compiler_params=_cparams(("parallel",)),
        name="nbr_attn_ctx",
    )(zc, zc, zc)


def _pool_kernel(u_ref, w_ref, scale_ref, o_ref):
    n = u_ref.shape[0]
    t = lax.broadcasted_iota(jnp.int32, (n, C_GROUP_DIM), 0)

    def down(a, k):
        return jnp.where(t >= k, pltpu.roll(a, k, axis=0), 0.0)

    def up(a, k):
        return jnp.where(t < n - k, pltpu.roll(a, n - k, axis=0), 0.0)

    for g, w in enumerate(C_WINDOWS):
        sl = slice(g * C_GROUP_DIM, (g + 1) * C_GROUP_DIM)
        u = u_ref[:, sl].astype(F32)
        half = w // 2
        back = u
        fwd = u
        k = 1
        while k < half:
            back = back + down(back, k)
            fwd = fwd + up(fwd, k)
            k *= 2
        total = down(back, 1) + fwd
        lo = jnp.maximum(t - half, 0)
        hi = jnp.minimum(t + half - 1, n - 1)
        cnt = (hi - lo + 1).astype(F32)
        pooled = total / cnt - u
        y = _dot(pooled.astype(BF16), w_ref[0, g]) * scale_ref[:, sl]
        o_ref[:, sl] = y.astype(BF16)


def _pool(z, c_w, c_scale, *, layer, n):
    rows = z.shape[0]
    return pl.pallas_call(
        _pool_kernel,
        grid=(rows // n,),
        in_specs=[
            pl.BlockSpec((n, Z_TILE), lambda b: (b, T_CU)),
            pl.BlockSpec((1, len(C_WINDOWS), C_GROUP_DIM, C_GROUP_DIM), lambda b: (layer, 0, 0, 0)),
            pl.BlockSpec((1, BRANCH_W), lambda b: (0, 0)),
        ],
        out_specs=pl.BlockSpec((n, BRANCH_W), lambda b: (b, 0)),
        out_shape=jax.ShapeDtypeStruct((rows, BRANCH_W), BF16),
        compiler_params=_cparams(("parallel",)),
        name=f"pool_{n}",
    )(z, c_w, c_scale)


def _diff_kernel(lam_ref, q_ref, *rest, lam_init, latent):
    if latent:
        k_ref, v_ref, kc_ref, vc_ref, g_ref, o_ref = rest
    else:
        kc_ref, vc_ref, g_ref, o_ref = rest
    dl = lam_ref[...]
    lam = (jnp.exp(jnp.sum(dl[0:1] * dl[1:2], axis=-1, keepdims=True))
           - jnp.exp(jnp.sum(dl[2:3] * dl[3:4], axis=-1, keepdims=True)) + lam_init)
    q = q_ref[...] * ATTN_SCALE
    tq = q.shape[0]
    hw = 2 * HEAD_DIM
    lane_hi = lax.broadcasted_iota(jnp.int32, (tq, hw), 1) >= HEAD_DIM
    for h in range(D_HEADS):
        sl = slice(h * hw, (h + 1) * hw)
        qh = q[:, sl]
        zero = jnp.zeros_like(qh)
        q_stack = jnp.concatenate([jnp.where(lane_hi, zero, qh), jnp.where(lane_hi, qh, zero)], axis=0)
        segs = [(kc_ref[:, sl], vc_ref[:, sl], None)]
        if latent:
            segs.append((k_ref[:, sl], v_ref[:, sl], None))
        o, l = _softmax_attend(q_stack, segs)
        o = o / l
        od = o[:tq] - lam * o[tq:]
        ms = jnp.mean(od * od, axis=-1, keepdims=True)
        y = od * lax.rsqrt(ms + EPS) * g_ref[...] * (1.0 - lam_init)
        o_ref[:, sl] = y.astype(BF16)


def _diff_attn(z, zc, d_lambda, subln, *, lam_init, latent):
    full = lambda shape: pl.BlockSpec(shape, lambda *_: (0,) * len(shape))
    if latent:
        tq = 256
        nq = SEQ // tq
        grid = (BATCH, nq)
        in_specs = [
            full((4, HEAD_DIM)),
            pl.BlockSpec((tq, Z_TILE), lambda b, n: (b * nq + n, T_DQ)),
            pl.BlockSpec((SEQ, Z_TILE), lambda b, n: (b, T_DK)),
            pl.BlockSpec((SEQ, Z_TILE), lambda b, n: (b, T_DV)),
            pl.BlockSpec((CTX_LEN, Z_TILE), lambda b, n: (b, T_DK)),
            pl.BlockSpec((CTX_LEN, Z_TILE), lambda b, n: (b, T_DV)),
            full((1, 2 * HEAD_DIM)),
        ]
        out_specs = pl.BlockSpec((tq, Z_TILE), lambda b, n: (b * nq + n, 0))
        args = (d_lambda, z, z, z, zc, zc, subln)
        rows = BATCH * SEQ
        sem = ("parallel", "arbitrary")
    else:
        grid = (BATCH,)
        in_specs = [
            full((4, HEAD_DIM)),
            pl.BlockSpec((CTX_LEN, Z_TILE), lambda b: (b, T_DQ)),
            pl.BlockSpec((CTX_LEN, Z_TILE), lambda b: (b, T_DK)),
            pl.BlockSpec((CTX_LEN, Z_TILE), lambda b: (b, T_DV)),
            full((1, 2 * HEAD_DIM)),
        ]
        out_specs = pl.BlockSpec((CTX_LEN, Z_TILE), lambda b: (b, 0))
        args = (d_lambda, zc, zc, zc, subln)
        rows = BATCH * CTX_LEN
        sem = ("parallel",)
    return pl.pallas_call(
        functools.partial(_diff_kernel, lam_init=lam_init, latent=latent),
        grid=grid,
        in_specs=in_specs,
        out_specs=out_specs,
        out_shape=jax.ShapeDtypeStruct((rows, BRANCH_W), BF16),
        compiler_params=_cparams(sem),
        name="diff_attn" if latent else "diff_attn_ctx",
    )(*args)


def _merge_kernel(*refs):
    y_refs = refs[:N_BRANCH]
    g_refs = refs[N_BRANCH:2 * N_BRANCH]
    wb_ref, wo_ref, x_ref, gt_ref, gn_ref, shift_ref, scale_ref, xo_ref, h_ref = refs[2 * N_BRANCH:]
    mixed = None
    for k in range(N_BRANCH):
        proj = _dot(y_refs[k][...], wb_ref[0, k]) * g_refs[k][...].astype(F32)
        mixed = proj if mixed is None else mixed + proj
    out = _dot(mixed.astype(BF16), wo_ref[0])
    xo_ref[...] = x_ref[...] + gt_ref[0] * out
    _norm_modulate_rows(xo_ref, gn_ref, shift_ref, scale_ref, h_ref)


def _merge(ys, z, w_branch, w_out, x, gn2, mod3, *, layer, tm, mod_row):
    m = x.shape[0]
    row = lambda i: (i, 0)
    mod = lambda chunk: pl.BlockSpec((1, 1, D_MODEL), lambda i: (mod_row(i), 0, chunk))
    resident = pl.Buffered(1)
    return pl.pallas_call(
        _merge_kernel,
        grid=(m // tm,),
        in_specs=[pl.BlockSpec((tm, BRANCH_W), row)] * N_BRANCH
        + [pl.BlockSpec((tm, D_MODEL), lambda i, k=k: (i, k)) for k in range(N_BRANCH)]
        + [
            pl.BlockSpec((1, N_BRANCH, BRANCH_W, D_MODEL), lambda i: (layer, 0, 0, 0), pipeline_mode=resident),
            pl.BlockSpec((1, D_MODEL, D_MODEL), lambda i: (layer, 0, 0), pipeline_mode=resident),
            pl.BlockSpec((tm, D_MODEL), row),
            mod(2),
            pl.BlockSpec((1, D_MODEL), lambda i: (0, 0)),
            mod(3),
            mod(4),
        ],
        out_specs=[pl.BlockSpec((tm, D_MODEL), row), pl.BlockSpec((tm, D_MODEL), row)],
        out_shape=[jax.ShapeDtypeStruct((m, D_MODEL), F32), jax.ShapeDtypeStruct((m, D_MODEL), BF16)],
        compiler_params=_cparams(("parallel",)),
        name=f"merge_{m}",
    )(*ys, z, z, z, z, w_branch, w_out, x, mod3, gn2, mod3, mod3)


def _mlp_kernel(h_ref, x_ref, gt_ref, w1_ref, w2_ref, *rest, emit_next):
    if emit_next:
        gn_ref, shift_ref, scale_ref, xo_ref, hn_ref = rest
    else:
        (xo_ref,) = rest
    f = pl.program_id(1)

    @pl.when(f == 0)
    def _():
        xo_ref[...] = jnp.zeros_like(xo_ref)

    a = jnp.maximum(_dot(h_ref[...], w1_ref[0]), 0.0)
    xo_ref[...] += _dot((a * a).astype(BF16), w2_ref[0])

    @pl.when(f == pl.num_programs(1) - 1)
    def _():
        xo_ref[...] = x_ref[...] + gt_ref[0] * xo_ref[...]
        if emit_next:
            _norm_modulate_rows(xo_ref, gn_ref, shift_ref, scale_ref, hn_ref)


def _mlp(h, x, mod3, w1, w2, next_norm, *, layer, tm, tf, mod_row):
    m = x.shape[0]
    emit_next = next_norm is not None
    row = lambda i, f: (i, 0)
    in_specs = [
        pl.BlockSpec((tm, D_MODEL), row),
        pl.BlockSpec((tm, D_MODEL), row),
        pl.BlockSpec((1, 1, D_MODEL), lambda i, f: (mod_row(i), 0, 5)),
        pl.BlockSpec((1, D_MODEL, tf), lambda i, f: (layer, 0, f)),
        pl.BlockSpec((1, tf, D_MODEL), lambda i, f: (layer, f, 0)),
    ]
    args = [h, x, mod3, w1, w2]
    out_specs = [pl.BlockSpec((tm, D_MODEL), row)]
    out_shape = [jax.ShapeDtypeStruct((m, D_MODEL), F32)]
    if emit_next:
        gn_next, mod3_next = next_norm
        in_specs += [
            pl.BlockSpec((1, D_MODEL), lambda i, f: (0, 0)),
            pl.BlockSpec((1, 1, D_MODEL), lambda i, f: (mod_row(i), 0, 0)),
            pl.BlockSpec((1, 1, D_MODEL), lambda i, f: (mod_row(i), 0, 1)),
        ]
        args += [gn_next, mod3_next, mod3_next]
        out_specs.append(pl.BlockSpec((tm, D_MODEL), row))
        out_shape.append(jax.ShapeDtypeStruct((m, D_MODEL), BF16))
    outs = pl.pallas_call(
        functools.partial(_mlp_kernel, emit_next=emit_next),
        grid=(m // tm, D_FF // tf),
        in_specs=in_specs,
        out_specs=out_specs,
        out_shape=out_shape,
        compiler_params=_cparams(("parallel", "arbitrary")),
        name=f"mlp_{m}",
    )(*args)
    return outs if emit_next else (outs[0], None)


def _rope_tables():
    t = jnp.arange(SEQ)
    row = (t // GRID_W).astype(F32)
    col = (t % GRID_W).astype(F32)
    n_freq = HEAD_DIM // 4
    inv = ROPE_BASE ** (-jnp.arange(n_freq, dtype=F32) / n_freq)
    ar = row[:, None] * inv
    ac = col[:, None] * inv
    cos_h = jnp.concatenate([jnp.cos(ar), jnp.cos(ar), jnp.cos(ac), jnp.cos(ac)], axis=-1)
    sin_h = jnp.concatenate([-jnp.sin(ar), jnp.sin(ar), -jnp.sin(ac), jnp.sin(ac)], axis=-1)
    return jnp.tile(cos_h, (1, LANES // HEAD_DIM)), jnp.tile(sin_h, (1, LANES // HEAD_DIM))


def _bias_and_gain_rows(b_gate_l, qk_gain_l):
    head_cols = Z_WIDTH - N_BRANCH * D_MODEL
    bias = jnp.concatenate([b_gate_l, jnp.zeros((head_cols,), F32)])[None, :]
    ones = jnp.ones((Z_TILE,), F32)
    rep = lambda g: jnp.tile(g, Z_TILE // HEAD_DIM)
    akv_gain = jnp.concatenate([jnp.tile(qk_gain_l[0, 1], A_KV_HEADS), jnp.ones((Z_TILE - A_KV_HEADS * HEAD_DIM,), F32)])
    gain = jnp.concatenate([
        jnp.ones((N_BRANCH * D_MODEL,), F32),
        rep(qk_gain_l[0, 0]), akv_gain,
        rep(qk_gain_l[1, 0]), rep(qk_gain_l[1, 1]), ones, ones,
        rep(qk_gain_l[2, 0]), rep(qk_gain_l[2, 1]), ones,
    ])[None, :]
    return bias, gain


def kernel(x, c, ctx, c_ctx, w_ada, b_ada, g_norm1, g_norm2, w_in, b_gate, qk_gain, a_sink, b_rpb,
           c_w, c_scale, d_lambda, d_subln, w_branch, w_out, w_ff1, w_ff2):
    cos_t, sin_t = _rope_tables()
    lane = jnp.arange(LANES)
    bd = (lane[:, None] // HEAD_DIM == lane[None, :] // HEAD_DIM).astype(BF16)

    cc = jnp.concatenate([c, c_ctx[None, :], jnp.zeros((8 - BATCH - 1, D_MODEL), F32)], axis=0)
    mod_all = _modulation(cc, w_ada, b_ada)

    w_p = _w_in_prep(w_in)
    wb = w_branch.astype(BF16)
    wo = w_out.astype(BF16)
    w1 = w_ff1.astype(BF16)
    w2 = w_ff2.astype(BF16)
    cw = c_w.astype(BF16)

    xl = x.reshape(BATCH * SEQ, D_MODEL)
    xc = ctx.reshape(BATCH * CTX_LEN, D_MODEL)
    ctx_row = lambda i: BATCH
    batch_row = lambda tm: (lambda i: i // (SEQ // tm))
    mods = [mod_all[l].reshape(8, 1, 6 * D_MODEL) for l in range(DEPTH)]
    gn1 = [g_norm1[l][None, :] for l in range(DEPTH)]

    tm_in, tm_merge, tm_mlp, tf_mlp = 1024, 256, 512, 1024
    hl = _norm_mod(xl, gn1[0], mods[0], tm=tm_mlp, chunk=0, mod_row=batch_row(tm_mlp))
    hc = _norm_mod(xc, gn1[0], mods[0], tm=tm_mlp, chunk=0, mod_row=ctx_row)

    for l in range(DEPTH):
        last = l == DEPTH - 1
        lam_init = 0.8 - 0.6 * math.exp(-0.3 * l)
        mod3 = mods[l]
        bias_p, gain_p = _bias_and_gain_rows(b_gate[l], qk_gain[l])
        gn2 = g_norm2[l][None, :]
        cs = c_scale[l][None, :]
        subln = d_subln[l][None, :]
        next_norm = None if last else (gn1[l + 1], mods[l + 1])

        z = _inproj(hl, w_p, bias_p, gain_p, cos_t, sin_t, bd, layer=l, tm=tm_in, rope=True)
        zc = _inproj(hc, w_p, bias_p, gain_p, cos_t, sin_t, bd, layer=l, tm=BATCH * CTX_LEN, rope=False)

        table = _rpb_table(b_rpb[l])
        ya = _win_gqa(z, zc, a_sink[l], latent=True)
        yb = _nbr_attn(z, zc, table)
        yc = _pool(z, cw, cs, layer=l, n=SEQ)
        yd = _diff_attn(z, zc, d_lambda[l], subln, lam_init=lam_init, latent=True)
        xl, h2 = _merge((ya, yb, yc, yd), z, wb, wo, xl, gn2, mod3,
                        layer=l, tm=tm_merge, mod_row=batch_row(tm_merge))
        xl, hl = _mlp(h2, xl, mod3, w1, w2, next_norm, layer=l, tm=tm_mlp, tf=tf_mlp, mod_row=batch_row(tm_mlp))
        if not last:
            yac = _win_gqa(z, zc, a_sink[l], latent=False)
            ybc = _ctx_mha(zc)
            ycc = _pool(zc, cw, cs, layer=l, n=CTX_LEN)
            ydc = _diff_attn(z, zc, d_lambda[l], subln, lam_init=lam_init, latent=False)
            xc, h2c = _merge((yac, ybc, ycc, ydc), zc, wb, wo, xc, gn2, mod3,
                             layer=l, tm=tm_merge, mod_row=ctx_row)
            xc, hc = _mlp(h2c, xc, mod3, w1, w2, next_norm, layer=l, tm=tm_mlp, tf=tf_mlp, mod_row=ctx_row)

    return xl.reshape(BATCH, SEQ, D_MODEL)
```

```python
import functools
import math

import jax
import jax.numpy as jnp
from jax import lax
from jax.experimental import pallas as pl
from jax.experimental.pallas import tpu as pltpu

D_MODEL = 2048
BATCH = 4
SEQ = 2048
DEPTH = 2
GRID_W = 64
GRID_H = SEQ // GRID_W
CTX_LEN = 256
HEAD_DIM = 64
N_BRANCH = 4
BRANCH_W = D_MODEL // N_BRANCH
A_HEADS = 8
A_KV_HEADS = 2
A_GROUP = A_HEADS // A_KV_HEADS
A_WINDOW = 128
A_BLOCK = 128
B_HEADS = 8
B_WIN_ROWS = 8
B_WIN_COLS = 16
C_WINDOWS = (2, 4, 8, 16)
C_GROUP_DIM = 128
D_HEADS = 4
D_FF = 4 * D_MODEL
ROPE_BASE = 10000.0
EPS = 1e-6
NEG = -1e30
ATTN_SCALE = HEAD_DIM ** -0.5

F32 = jnp.float32
BF16 = jnp.bfloat16

V7X_VMEM_LIMIT_BYTES = 56 * 1024 * 1024
LANES = 128

LATENT_ROWS = BATCH * SEQ
CTX_ROWS = BATCH * CTX_LEN
ALL_ROWS = LATENT_ROWS + CTX_ROWS

Z_TILE = 512
H_AQ, H_AKV, H_BQ, H_BK, H_BV, H_CU, H_DQ, H_DK, H_DV = range(9)
HEAD_TILES = 9
W_SRC = 256
GATE_COLS = N_BRANCH * D_MODEL
GATE_SRC0 = (12544 - GATE_COLS) // W_SRC
TM_PROJ = 1024


def _cparams(sem):
    return pltpu.CompilerParams(dimension_semantics=sem, vmem_limit_bytes=V7X_VMEM_LIMIT_BYTES)


def _dot_nt(a, b):
    return lax.dot_general(a, b, (((1,), (1,)), ((), ())), preferred_element_type=F32)


def _dot(a, b):
    return jnp.dot(a, b, preferred_element_type=F32)


def _ctx_blocks(block_rows):
    return LATENT_ROWS // block_rows


def _mod_kernel(c_ref, w_ref, b_ref, o_ref):
    c = c_ref[...]
    s = c * (0.5 * (jnp.tanh(0.5 * c) + 1.0))
    o_ref[0] = _dot(s, w_ref[0]) + b_ref[0]


def _modulation(cc, w_ada, b_ada):
    tn = 1024
    n = 6 * D_MODEL
    return pl.pallas_call(
        _mod_kernel,
        grid=(DEPTH, n // tn),
        in_specs=[
            pl.BlockSpec((8, D_MODEL), lambda l, j: (0, 0)),
            pl.BlockSpec((1, D_MODEL, tn), lambda l, j: (l, 0, j)),
            pl.BlockSpec((1, 1, tn), lambda l, j: (l, 0, j)),
        ],
        out_specs=pl.BlockSpec((1, 8, tn), lambda l, j: (l, 0, j)),
        out_shape=jax.ShapeDtypeStruct((DEPTH, 8, n), F32),
        compiler_params=_cparams(("parallel", "parallel")),
        name="modulation",
    )(cc, w_ada, b_ada.reshape(DEPTH, 1, n))


NORM_ROWS = 16


def _norm_modulate_rows(x_ref, gn_ref, shift_ref, scale_ref, h_ref):
    n = x_ref.shape[0]
    gs = gn_ref[...] * (1.0 + scale_ref[0])
    shift = shift_ref[0]

    def body(c, carry):
        rows = pl.ds(pl.multiple_of(c * NORM_ROWS, NORM_ROWS), NORM_ROWS)
        x = x_ref[rows, :]
        inv = lax.rsqrt(jnp.mean(x * x, axis=-1, keepdims=True) + EPS)
        h_ref[rows, :] = ((x * inv) * gs + shift).astype(BF16)
        return carry

    lax.fori_loop(0, n // NORM_ROWS, body, 0, unroll=8)


def _norm_mod_kernel(x_ref, gn_ref, shift_ref, scale_ref, *rest):
    _norm_modulate_rows(x_ref, gn_ref, shift_ref, scale_ref, rest[-1])


def _h_out(tm, row0, prev):
    spec = pl.BlockSpec((tm, D_MODEL), lambda i, *_: (row0 // tm + i, 0))
    shape = jax.ShapeDtypeStruct((ALL_ROWS, D_MODEL), BF16)
    return spec, shape


def _norm_mod(x, gn, mod3, prev, *, tm, row0, mod_row):
    m = x.shape[0]
    in_specs = [
        pl.BlockSpec((tm, D_MODEL), lambda i: (i, 0)),
        pl.BlockSpec((1, D_MODEL), lambda i: (0, 0)),
        pl.BlockSpec((1, 1, D_MODEL), lambda i: (mod_row(i), 0, 0)),
        pl.BlockSpec((1, 1, D_MODEL), lambda i: (mod_row(i), 0, 1)),
    ]
    args = [x, gn, mod3, mod3]
    aliases = {}
    if prev is not None:
        in_specs.append(pl.BlockSpec(memory_space=pl.ANY))
        args.append(prev)
        aliases = {len(args) - 1: 0}
    spec, shape = _h_out(tm, row0, prev)
    return pl.pallas_call(
        _norm_mod_kernel,
        grid=(m // tm,),
        in_specs=in_specs,
        out_specs=spec,
        out_shape=shape,
        input_output_aliases=aliases,
        compiler_params=_cparams(("parallel",)),
        name=f"norm_mod_{m}",
    )(*args)


def _swap16(y):
    lane = lax.broadcasted_iota(jnp.int32, y.shape, 1)
    fwd = pltpu.roll(y, LANES - 16, axis=1)
    bwd = pltpu.roll(y, 16, axis=1)
    return jnp.where((lane & 16) == 0, fwd, bwd)


def _cast_weight_blocks(w_refs, w_scr):
    @pl.when(pl.program_id(1) == 0)
    def _():
        for c, w_ref in enumerate(w_refs):
            w_scr[:, c * W_SRC:(c + 1) * W_SRC] = w_ref[0].astype(BF16)


def _gate_kernel(h_ref, *rest):
    *w_refs, bias_ref, g_ref, w_scr = rest
    _cast_weight_blocks(w_refs, w_scr)
    t = _dot(h_ref[...], w_scr[...]) + bias_ref[...]
    g_ref[...] = (0.5 * (jnp.tanh(0.5 * t) + 1.0)).astype(BF16)


def _gate_proj(h, w_in, bias, *, layer, tn):
    n_src = tn // W_SRC
    w_spec = lambda c: pl.BlockSpec((1, D_MODEL, W_SRC), lambda n, i: (layer, 0, GATE_SRC0 + n * n_src + c))
    return pl.pallas_call(
        _gate_kernel,
        grid=(GATE_COLS // tn, ALL_ROWS // TM_PROJ),
        in_specs=[pl.BlockSpec((TM_PROJ, D_MODEL), lambda n, i: (i, 0))]
        + [w_spec(c) for c in range(n_src)]
        + [pl.BlockSpec((1, tn), lambda n, i: (0, n))],
        out_specs=pl.BlockSpec((TM_PROJ, tn), lambda n, i: (i, n)),
        out_shape=jax.ShapeDtypeStruct((ALL_ROWS, GATE_COLS), BF16),
        scratch_shapes=[pltpu.VMEM((D_MODEL, tn), BF16)],
        compiler_params=_cparams(("parallel", "arbitrary")),
        name="gate_proj",
    )(h, *([w_in] * n_src), bias)


def _head_src_block(t, half):
    return jnp.where(t == H_AQ, half, jnp.where(t == H_AKV, 2, 2 * t - 1 + half))


def _head_kernel(h_ref, wa_ref, wb_ref, gain_ref, cos_ref, sin_ref, bd_ref, z_ref, w_scr):
    t = pl.program_id(0)
    _cast_weight_blocks((wa_ref, wb_ref), w_scr)

    def project():
        return _dot(h_ref[...], w_scr[...])

    def head_norm(a, gain, with_rope):
        ms = _dot((a * a).astype(BF16), bd_ref[...]) * (1.0 / HEAD_DIM)
        y = a * lax.rsqrt(ms + EPS) * gain
        if with_rope:
            y = y * cos_ref[...] + _swap16(y) * sin_ref[...]
        return y

    def norm_tile(n_norm_chunks, with_rope):
        acc = project()
        for c in range(Z_TILE // LANES):
            sl = slice(c * LANES, (c + 1) * LANES)
            a = acc[:, sl]
            if c < n_norm_chunks:
                a = head_norm(a, gain_ref[:, sl], with_rope)
            z_ref[:, sl] = a.astype(BF16)

    @pl.when((t == H_AQ) | (t == H_DQ) | (t == H_DK))
    def _():
        norm_tile(4, True)

    @pl.when(t == H_AKV)
    def _():
        norm_tile(1, True)

    @pl.when((t == H_BQ) | (t == H_BK))
    def _():
        norm_tile(4, False)

    @pl.when((t == H_BV) | (t == H_CU) | (t == H_DV))
    def _():
        z_ref[...] = project().astype(BF16)


def _head_proj(h, w_in, gain, cos_t, sin_t, bd, *, layer):
    seq_tiles = SEQ // TM_PROJ
    rope_block = lambda i: jnp.where(i < LATENT_ROWS // TM_PROJ, i % seq_tiles, seq_tiles)
    w_spec = lambda half: pl.BlockSpec((1, D_MODEL, W_SRC), lambda t, i: (layer, 0, _head_src_block(t, half)))
    return pl.pallas_call(
        _head_kernel,
        grid=(HEAD_TILES, ALL_ROWS // TM_PROJ),
        in_specs=[
            pl.BlockSpec((TM_PROJ, D_MODEL), lambda t, i: (i, 0)),
            w_spec(0), w_spec(1),
            pl.BlockSpec((1, Z_TILE), lambda t, i: (0, t)),
            pl.BlockSpec((TM_PROJ, LANES), lambda t, i: (rope_block(i), 0)),
            pl.BlockSpec((TM_PROJ, LANES), lambda t, i: (rope_block(i), 0)),
            pl.BlockSpec((LANES, LANES), lambda t, i: (0, 0)),
        ],
        out_specs=pl.BlockSpec((None, TM_PROJ, Z_TILE), lambda t, i: (t, i, 0)),
        out_shape=jax.ShapeDtypeStruct((HEAD_TILES, ALL_ROWS, Z_TILE), BF16),
        scratch_shapes=[pltpu.VMEM((D_MODEL, Z_TILE), BF16)],
        compiler_params=_cparams(("parallel", "arbitrary")),
        name="head_proj",
    )(h, w_in, w_in, gain, cos_t, sin_t, bd)


def _softmax_attend(q, segs, extra_logit=None):
    scores = []
    for k, _, bias in segs:
        s = _dot_nt(q, k)
        if bias is not None:
            s = s + bias
        scores.append(s)
    m = scores[0].max(axis=-1, keepdims=True)
    for s in scores[1:]:
        m = jnp.maximum(m, s.max(axis=-1, keepdims=True))
    if extra_logit is not None:
        m = jnp.maximum(m, extra_logit)
    l = None
    o = None
    for s, (_, v, _) in zip(scores, segs):
        p = jnp.exp(s - m)
        ls = p.sum(axis=-1, keepdims=True)
        os_ = _dot(p.astype(BF16), v)
        l = ls if l is None else l + ls
        o = os_ if o is None else o + os_
    if extra_logit is not None:
        l = l + jnp.exp(extra_logit - m)
    return o, l


def _lane_group(shape, width):
    return lax.broadcasted_iota(jnp.int32, shape, 1) // width


def _zspec(tile, rows, index_rows, width=Z_TILE):
    return pl.BlockSpec((None, rows, width), lambda *ids: (tile, index_rows(*ids), 0))


def _repeat_kv_heads(kv):
    pieces = [kv[:, i * HEAD_DIM:(i + 1) * HEAD_DIM] for i in range(2 * A_KV_HEADS)]
    k = jnp.concatenate([pieces[g] for g in range(A_KV_HEADS) for _ in range(A_GROUP)], axis=1)
    v = jnp.concatenate([pieces[A_KV_HEADS + g] for g in range(A_KV_HEADS) for _ in range(A_GROUP)], axis=1)
    return k, v


def _win_gqa_kernel(sink_ref, q_ref, *rest, latent):
    if latent:
        kv_ref, ckv_ref, o_ref, k_scr, v_scr, kc_scr, vc_scr = rest
    else:
        ckv_ref, o_ref, kc_scr, vc_scr = rest
    gw = A_GROUP * HEAD_DIM

    def build():
        kc_scr[...], vc_scr[...] = _repeat_kv_heads(ckv_ref[...])
        if latent:
            k_scr[...], v_scr[...] = _repeat_kv_heads(kv_ref[...])

    if latent:
        pl.when(pl.program_id(1) == 0)(build)
    else:
        build()

    q = q_ref[...] * ATTN_SCALE
    tq = q.shape[0]
    rows = A_GROUP * tq
    if latent:
        n = pl.program_id(1)
        span = 3 * A_BLOCK
        start = pl.multiple_of(jnp.clip((n - 1) * A_BLOCK, 0, SEQ - span), A_BLOCK)
        qpos = n * A_BLOCK + lax.broadcasted_iota(jnp.int32, (tq, span), 0)
        kpos = start + lax.broadcasted_iota(jnp.int32, (tq, span), 1)
        band = jnp.where(jnp.abs(qpos - kpos) <= A_WINDOW, 0.0, NEG).astype(F32)
        band = jnp.concatenate([band] * A_GROUP, axis=0)
    row_head = lax.broadcasted_iota(jnp.int32, (rows, 1), 0) // tq
    lane_head = _lane_group((tq, gw), HEAD_DIM)
    for g in range(A_KV_HEADS):
        gs = slice(g * gw, (g + 1) * gw)
        qg = q[:, gs]
        zero = jnp.zeros_like(qg)
        q_stack = jnp.concatenate([jnp.where(lane_head == r, qg, zero) for r in range(A_GROUP)], axis=0)
        sink = jnp.zeros((rows, 1), F32)
        for r in range(A_GROUP):
            sink = jnp.where(row_head == r, sink_ref[g * A_GROUP + r], sink)
        segs = [(kc_scr[:, gs], vc_scr[:, gs], None)]
        if latent:
            segs.append((k_scr[pl.ds(start, span), gs], v_scr[pl.ds(start, span), gs], band))
        o, l = _softmax_attend(q_stack, segs, extra_logit=sink)
        o = o / l
        og = jnp.zeros((tq, gw), F32)
        for r in range(A_GROUP):
            og = jnp.where(lane_head == r, o[r * tq:(r + 1) * tq], og)
        o_ref[:, gs] = og.astype(BF16)


def _win_gqa(zh, sink, *, latent):
    kvb = 2 * A_KV_HEADS * HEAD_DIM
    c0 = _ctx_blocks(CTX_LEN)
    smem = pl.BlockSpec(memory_space=pltpu.SMEM)
    rep = lambda n: pltpu.VMEM((n, A_HEADS * HEAD_DIM), BF16)
    if latent:
        nq = SEQ // A_BLOCK
        grid = (BATCH, nq)
        in_specs = [
            smem,
            _zspec(H_AQ, A_BLOCK, lambda b, n: b * nq + n),
            _zspec(H_AKV, SEQ, lambda b, n: b, kvb),
            _zspec(H_AKV, CTX_LEN, lambda b, n: c0 + b, kvb),
        ]
        out_specs = pl.BlockSpec((A_BLOCK, Z_TILE), lambda b, n: (b * nq + n, 0))
        args = (sink, zh, zh, zh)
        rows = LATENT_ROWS
        sem = ("parallel", "arbitrary")
        scratch = [rep(SEQ), rep(SEQ), rep(CTX_LEN), rep(CTX_LEN)]
    else:
        grid = (BATCH,)
        in_specs = [
            smem,
            _zspec(H_AQ, CTX_LEN, lambda b: c0 + b),
            _zspec(H_AKV, CTX_LEN, lambda b: c0 + b, kvb),
        ]
        out_specs = pl.BlockSpec((CTX_LEN, Z_TILE), lambda b: (b, 0))
        args = (sink, zh, zh)
        rows = CTX_ROWS
        sem = ("parallel",)
        scratch = [rep(CTX_LEN), rep(CTX_LEN)]
    return pl.pallas_call(
        functools.partial(_win_gqa_kernel, latent=latent),
        grid=grid,
        in_specs=in_specs,
        out_specs=out_specs,
        out_shape=jax.ShapeDtypeStruct((rows, BRANCH_W), BF16),
        scratch_shapes=scratch,
        compiler_params=_cparams(sem),
        name="win_gqa" if latent else "win_gqa_ctx",
    )(*args)


N_ROW_OFFSETS = 2 * B_WIN_ROWS - 1
N_COL_OFFSETS = 2 * B_WIN_COLS - 1
N_PAIR = N_ROW_OFFSETS + 1
NBR_QROWS = 4
NBR_SLAB = NBR_QROWS + B_WIN_ROWS
NBR_GROUPS = GRID_H // NBR_QROWS
NBR_CASES = 3


def _nbr_case_geometry(case):
    if case == 0:
        return B_WIN_ROWS - 1, lambda a, c: 0 <= c < B_WIN_ROWS
    if case == 1:
        return B_WIN_ROWS - 1 - NBR_QROWS, lambda a, c: 0 <= c - a < B_WIN_ROWS
    return B_WIN_ROWS - 1 - 2 * NBR_QROWS, lambda a, c: NBR_QROWS <= c < NBR_SLAB


def _rpb_table_kernel(rpb_ref, o_ref, pair_scr):
    h = pl.program_id(0)
    shape = (GRID_W, LANES)
    qcol = lax.broadcasted_iota(jnp.int32, shape, 0)
    lane = lax.broadcasted_iota(jnp.int32, shape, 1)
    kcol = lane & (GRID_W - 1)
    hi = lane >= GRID_W
    cs = jnp.clip(qcol - B_WIN_COLS // 2, 0, GRID_W - B_WIN_COLS)
    in_window = (kcol >= cs) & (kcol < cs + B_WIN_COLS)
    ci = kcol - qcol + (B_WIN_COLS - 1)
    neg = jnp.full(shape, NEG, F32)

    def rpb(d, c):
        return rpb_ref[(h * N_ROW_OFFSETS + d) * N_COL_OFFSETS + c]

    for p in range(N_PAIR):
        acc = jnp.zeros(shape, F32)
        for c in range(N_COL_OFFSETS):
            hit = ci == c
            if p - 1 >= 0:
                acc = jnp.where(hit & ~hi, rpb(p - 1, c), acc)
            if p < N_ROW_OFFSETS:
                acc = jnp.where(hit & hi, rpb(p, c), acc)
        pair_scr[p] = jnp.where(in_window, acc, NEG)

    for case in range(NBR_CASES):
        shift, in_rows = _nbr_case_geometry(case)
        for a in range(NBR_QROWS):
            for cp in range(NBR_SLAB // 2):
                c = 2 * cp
                lo_ok, hi_ok = in_rows(a, c), in_rows(a, c + 1)
                if lo_ok or hi_ok:
                    piece = pair_scr[c - a + shift + 1]
                    if not lo_ok:
                        piece = jnp.where(hi, piece, neg)
                    if not hi_ok:
                        piece = jnp.where(hi, neg, piece)
                else:
                    piece = neg
                o_ref[case, 0, a * GRID_W:(a + 1) * GRID_W, cp * LANES:(cp + 1) * LANES] = piece


def _rpb_table(rpb):
    nq, nk = NBR_QROWS * GRID_W, NBR_SLAB * GRID_W
    return pl.pallas_call(
        _rpb_table_kernel,
        grid=(B_HEADS,),
        in_specs=[pl.BlockSpec(memory_space=pltpu.SMEM)],
        out_specs=pl.BlockSpec((NBR_CASES, 1, nq, nk), lambda h: (0, h, 0, 0)),
        out_shape=jax.ShapeDtypeStruct((NBR_CASES, B_HEADS, nq, nk), F32),
        scratch_shapes=[pltpu.VMEM((N_PAIR, GRID_W, LANES), F32)],
        compiler_params=_cparams(("parallel",)),
        name="rpb_table",
    )(rpb.reshape(-1))


def _pair_attend(q, lane_hi, segs_of_head):
    zero = jnp.zeros_like(q)
    halves = []
    for hh in range(2):
        qm = jnp.where(lane_hi, q, zero) if hh else jnp.where(lane_hi, zero, q)
        o, l = _softmax_attend(qm, segs_of_head(hh))
        halves.append(o / l)
    return jnp.where(lane_hi, halves[1], halves[0])


def _nbr_kernel(q_ref, k_ref, v_ref, kc_ref, vc_ref, bias_ref, o_ref):
    g = pl.program_id(1)
    r0 = jnp.clip(NBR_QROWS * g - B_WIN_ROWS // 2, 0, GRID_H - NBR_SLAB)
    base = pl.multiple_of(r0 * GRID_W, GRID_W)
    nk = NBR_SLAB * GRID_W
    q = q_ref[...] * ATTN_SCALE
    lane_hi = lax.broadcasted_iota(jnp.int32, (q.shape[0], LANES), 1) >= HEAD_DIM
    for pr in range(B_HEADS // 2):
        sl = slice(pr * LANES, (pr + 1) * LANES)
        kp = k_ref[pl.ds(base, nk), sl]
        vp = v_ref[pl.ds(base, nk), sl]
        kcp = kc_ref[:, sl]
        vcp = vc_ref[:, sl]
        segs = lambda hh: [(kp, vp, bias_ref[0, 2 * pr + hh]), (kcp, vcp, None)]
        o_ref[:, sl] = _pair_attend(q[:, sl], lane_hi, segs).astype(BF16)


def _nbr_attn(zh, table):
    nq, nk = NBR_QROWS * GRID_W, NBR_SLAB * GRID_W
    c0 = _ctx_blocks(CTX_LEN)
    case = lambda g: jnp.where(g == 0, 0, jnp.where(g == NBR_GROUPS - 1, 2, 1))
    return pl.pallas_call(
        _nbr_kernel,
        grid=(BATCH, NBR_GROUPS),
        in_specs=[
            _zspec(H_BQ, nq, lambda b, g: b * NBR_GROUPS + g),
            _zspec(H_BK, SEQ, lambda b, g: b),
            _zspec(H_BV, SEQ, lambda b, g: b),
            _zspec(H_BK, CTX_LEN, lambda b, g: c0 + b),
            _zspec(H_BV, CTX_LEN, lambda b, g: c0 + b),
            pl.BlockSpec((1, B_HEADS, nq, nk), lambda b, g: (case(g), 0, 0, 0)),
        ],
        out_specs=pl.BlockSpec((nq, Z_TILE), lambda b, g: (b * NBR_GROUPS + g, 0)),
        out_shape=jax.ShapeDtypeStruct((LATENT_ROWS, BRANCH_W), BF16),
        compiler_params=_cparams(("parallel", "arbitrary")),
        name="nbr_attn",
    )(zh, zh, zh, zh, zh, table)


def _ctx_mha_kernel(q_ref, k_ref, v_ref, o_ref):
    q = q_ref[...] * ATTN_SCALE
    lane_hi = lax.broadcasted_iota(jnp.int32, (q.shape[0], LANES), 1) >= HEAD_DIM
    for pr in range(B_HEADS // 2):
        sl = slice(pr * LANES, (pr + 1) * LANES)
        segs = lambda hh: [(k_ref[:, sl], v_ref[:, sl], None)]
        o_ref[:, sl] = _pair_attend(q[:, sl], lane_hi, segs).astype(BF16)


def _ctx_mha(zh):
    c0 = _ctx_blocks(CTX_LEN)
    rows = lambda b: c0 + b
    return pl.pallas_call(
        _ctx_mha_kernel,
        grid=(BATCH,),
        in_specs=[_zspec(H_BQ, CTX_LEN, rows), _zspec(H_BK, CTX_LEN, rows), _zspec(H_BV, CTX_LEN, rows)],
        out_specs=pl.BlockSpec((CTX_LEN, Z_TILE), lambda b: (b, 0)),
        out_shape=jax.ShapeDtypeStruct((CTX_ROWS, BRANCH_W), BF16),
        compiler_params=_cparams(("parallel",)),
        name="nbr_attn_ctx",
    )(zh, zh, zh)


def _pool_kernel(u_ref, w_ref, scale_ref, o_ref):
    n = u_ref.shape[0]
    t = lax.broadcasted_iota(jnp.int32, (n, C_GROUP_DIM), 0)

    def down(a, k):
        return jnp.where(t >= k, pltpu.roll(a, k, axis=0), 0.0)

    def up(a, k):
        return jnp.where(t < n - k, pltpu.roll(a, n - k, axis=0), 0.0)

    for g, w in enumerate(C_WINDOWS):
        sl = slice(g * C_GROUP_DIM, (g + 1) * C_GROUP_DIM)
        u = u_ref[:, sl].astype(F32)
        half = w // 2
        back = u
        fwd = u
        k = 1
        while k < half:
            back = back + down(back, k)
            fwd = fwd + up(fwd, k)
            k *= 2
        total = down(back, 1) + fwd
        lo = jnp.maximum(t - half, 0)
        hi = jnp.minimum(t + half - 1, n - 1)
        cnt = (hi - lo + 1).astype(F32)
        pooled = total / cnt - u
        y = _dot(pooled.astype(BF16), w_ref[0, g]) * scale_ref[:, sl]
        o_ref[:, sl] = y.astype(BF16)


def _pool(zh, c_w, c_scale, *, layer, n, latent):
    rows = LATENT_ROWS if latent else CTX_ROWS
    b0 = 0 if latent else _ctx_blocks(n)
    return pl.pallas_call(
        _pool_kernel,
        grid=(rows // n,),
        in_specs=[
            _zspec(H_CU, n, lambda b: b0 + b),
            pl.BlockSpec((1, len(C_WINDOWS), C_GROUP_DIM, C_GROUP_DIM), lambda b: (layer, 0, 0, 0)),
            pl.BlockSpec((1, BRANCH_W), lambda b: (0, 0)),
        ],
        out_specs=pl.BlockSpec((n, BRANCH_W), lambda b: (b, 0)),
        out_shape=jax.ShapeDtypeStruct((rows, BRANCH_W), BF16),
        compiler_params=_cparams(("parallel",)),
        name=f"pool_{n}",
    )(zh, c_w, c_scale)


def _diff_kernel(lam_ref, q_ref, *rest, lam_init, latent):
    if latent:
        k_ref, v_ref, kc_ref, vc_ref, g_ref, o_ref = rest
    else:
        kc_ref, vc_ref, g_ref, o_ref = rest
    dl = lam_ref[...]
    lam = (jnp.exp(jnp.sum(dl[0:1] * dl[1:2], axis=-1, keepdims=True))
           - jnp.exp(jnp.sum(dl[2:3] * dl[3:4], axis=-1, keepdims=True)) + lam_init)
    q = q_ref[...] * ATTN_SCALE
    tq = q.shape[0]
    hw = 2 * HEAD_DIM
    lane_hi = lax.broadcasted_iota(jnp.int32, (tq, hw), 1) >= HEAD_DIM
    for h in range(D_HEADS):
        sl = slice(h * hw, (h + 1) * hw)
        qh = q[:, sl]
        zero = jnp.zeros_like(qh)
        q_stack = jnp.concatenate([jnp.where(lane_hi, zero, qh), jnp.where(lane_hi, qh, zero)], axis=0)
        segs = [(kc_ref[:, sl], vc_ref[:, sl], None)]
        if latent:
            segs.append((k_ref[:, sl], v_ref[:, sl], None))
        o, l = _softmax_attend(q_stack, segs)
        o = o / l
        od = o[:tq] - lam * o[tq:]
        ms = jnp.mean(od * od, axis=-1, keepdims=True)
        y = od * lax.rsqrt(ms + EPS) * g_ref[...] * (1.0 - lam_init)
        o_ref[:, sl] = y.astype(BF16)


def _diff_attn(zh, d_lambda, subln, *, lam_init, latent):
    full = lambda shape: pl.BlockSpec(shape, lambda *_: (0,) * len(shape))
    c0 = _ctx_blocks(CTX_LEN)
    if latent:
        tq = 256
        nq = SEQ // tq
        grid = (BATCH, nq)
        in_specs = [
            full((4, HEAD_DIM)),
            _zspec(H_DQ, tq, lambda b, n: b * nq + n),
            _zspec(H_DK, SEQ, lambda b, n: b),
            _zspec(H_DV, SEQ, lambda b, n: b),
            _zspec(H_DK, CTX_LEN, lambda b, n: c0 + b),
            _zspec(H_DV, CTX_LEN, lambda b, n: c0 + b),
            full((1, 2 * HEAD_DIM)),
        ]
        out_specs = pl.BlockSpec((tq, Z_TILE), lambda b, n: (b * nq + n, 0))
        args = (d_lambda, zh, zh, zh, zh, zh, subln)
        rows = LATENT_ROWS
        sem = ("parallel", "arbitrary")
    else:
        grid = (BATCH,)
        crow = lambda b: c0 + b
        in_specs = [
            full((4, HEAD_DIM)),
            _zspec(H_DQ, CTX_LEN, crow),
            _zspec(H_DK, CTX_LEN, crow),
            _zspec(H_DV, CTX_LEN, crow),
            full((1, 2 * HEAD_DIM)),
        ]
        out_specs = pl.BlockSpec((CTX_LEN, Z_TILE), lambda b: (b, 0))
        args = (d_lambda, zh, zh, zh, subln)
        rows = CTX_ROWS
        sem = ("parallel",)
    return pl.pallas_call(
        functools.partial(_diff_kernel, lam_init=lam_init, latent=latent),
        grid=grid,
        in_specs=in_specs,
        out_specs=out_specs,
        out_shape=jax.ShapeDtypeStruct((rows, BRANCH_W), BF16),
        compiler_params=_cparams(sem),
        name="diff_attn" if latent else "diff_attn_ctx",
    )(*args)


def _merge_kernel(*refs):
    y_refs = refs[:N_BRANCH]
    g_refs = refs[N_BRANCH:2 * N_BRANCH]
    wb_ref, wo_ref, x_ref, gt_ref, gn_ref, shift_ref, scale_ref, xo_ref, h_ref = refs[2 * N_BRANCH:]
    mixed = None
    for k in range(N_BRANCH):
        proj = _dot(y_refs[k][...], wb_ref[0, k]) * g_refs[k][...].astype(F32)
        mixed = proj if mixed is None else mixed + proj
    out = _dot(mixed.astype(BF16), wo_ref[0])
    xo_ref[...] = x_ref[...] + gt_ref[0] * out
    _norm_modulate_rows(xo_ref, gn_ref, shift_ref, scale_ref, h_ref)


def _merge(ys, gates, w_branch, w_out, x, gn2, mod3, *, layer, tm, row0, mod_row):
    m = x.shape[0]
    row = lambda i: (i, 0)
    g0 = row0 // tm
    mod = lambda chunk: pl.BlockSpec((1, 1, D_MODEL), lambda i: (mod_row(i), 0, chunk))
    resident = pl.Buffered(1)
    return pl.pallas_call(
        _merge_kernel,
        grid=(m // tm,),
        in_specs=[pl.BlockSpec((tm, BRANCH_W), row)] * N_BRANCH
        + [pl.BlockSpec((tm, D_MODEL), lambda i, k=k: (g0 + i, k)) for k in range(N_BRANCH)]
        + [
            pl.BlockSpec((1, N_BRANCH, BRANCH_W, D_MODEL), lambda i: (layer, 0, 0, 0), pipeline_mode=resident),
            pl.BlockSpec((1, D_MODEL, D_MODEL), lambda i: (layer, 0, 0), pipeline_mode=resident),
            pl.BlockSpec((tm, D_MODEL), row),
            mod(2),
            pl.BlockSpec((1, D_MODEL), lambda i: (0, 0)),
            mod(3),
            mod(4),
        ],
        out_specs=[pl.BlockSpec((tm, D_MODEL), row), pl.BlockSpec((tm, D_MODEL), row)],
        out_shape=[jax.ShapeDtypeStruct((m, D_MODEL), F32), jax.ShapeDtypeStruct((m, D_MODEL), BF16)],
        compiler_params=_cparams(("parallel",)),
        name=f"merge_{m}",
    )(*ys, *([gates] * N_BRANCH), w_branch, w_out, x, mod3, gn2, mod3, mod3)


def _mlp_kernel(h_ref, x_ref, gt_ref, w1_ref, w2_ref, *rest, emit_next):
    if emit_next:
        gn_ref, shift_ref, scale_ref, *_, xo_ref, hn_ref = rest
    else:
        (xo_ref,) = rest
    f = pl.program_id(1)

    @pl.when(f == 0)
    def _():
        xo_ref[...] = jnp.zeros_like(xo_ref)

    a = jnp.maximum(_dot(h_ref[...], w1_ref[0]), 0.0)
    xo_ref[...] += _dot((a * a).astype(BF16), w2_ref[0])

    @pl.when(f == pl.num_programs(1) - 1)
    def _():
        xo_ref[...] = x_ref[...] + gt_ref[0] * xo_ref[...]
        if emit_next:
            _norm_modulate_rows(xo_ref, gn_ref, shift_ref, scale_ref, hn_ref)


def _mlp(h, x, mod3, w1, w2, next_norm, *, layer, tm, tf, row0, mod_row):
    m = x.shape[0]
    emit_next = next_norm is not None
    row = lambda i, f: (i, 0)
    in_specs = [
        pl.BlockSpec((tm, D_MODEL), row),
        pl.BlockSpec((tm, D_MODEL), row),
        pl.BlockSpec((1, 1, D_MODEL), lambda i, f: (mod_row(i), 0, 5)),
        pl.BlockSpec((1, D_MODEL, tf), lambda i, f: (layer, 0, f)),
        pl.BlockSpec((1, tf, D_MODEL), lambda i, f: (layer, f, 0)),
    ]
    args = [h, x, mod3, w1, w2]
    out_specs = [pl.BlockSpec((tm, D_MODEL), row)]
    out_shape = [jax.ShapeDtypeStruct((m, D_MODEL), F32)]
    aliases = {}
    if emit_next:
        gn_next, mod3_next, prev = next_norm
        in_specs += [
            pl.BlockSpec((1, D_MODEL), lambda i, f: (0, 0)),
            pl.BlockSpec((1, 1, D_MODEL), lambda i, f: (mod_row(i), 0, 0)),
            pl.BlockSpec((1, 1, D_MODEL), lambda i, f: (mod_row(i), 0, 1)),
        ]
        args += [gn_next, mod3_next, mod3_next]
        if prev is not None:
            in_specs.append(pl.BlockSpec(memory_space=pl.ANY))
            args.append(prev)
            aliases = {len(args) - 1: 1}
        spec, shape = _h_out(tm, row0, prev)
        out_specs.append(spec)
        out_shape.append(shape)
    outs = pl.pallas_call(
        functools.partial(_mlp_kernel, emit_next=emit_next),
        grid=(m // tm, D_FF // tf),
        in_specs=in_specs,
        out_specs=out_specs,
        out_shape=out_shape,
        input_output_aliases=aliases,
        compiler_params=_cparams(("parallel", "arbitrary")),
        name=f"mlp_{m}",
    )(*args)
    return outs if emit_next else (outs[0], None)


def _rope_tables():
    t = jnp.arange(SEQ)
    row = (t // GRID_W).astype(F32)
    col = (t % GRID_W).astype(F32)
    n_freq = HEAD_DIM // 4
    inv = ROPE_BASE ** (-jnp.arange(n_freq, dtype=F32) / n_freq)
    ar = row[:, None] * inv
    ac = col[:, None] * inv
    cos_h = jnp.concatenate([jnp.cos(ar), jnp.cos(ar), jnp.cos(ac), jnp.cos(ac)], axis=-1)
    sin_h = jnp.concatenate([-jnp.sin(ar), jnp.sin(ar), -jnp.sin(ac), jnp.sin(ac)], axis=-1)
    reps = LANES // HEAD_DIM
    cos_t = jnp.concatenate([jnp.tile(cos_h, (1, reps)), jnp.ones((TM_PROJ, LANES), F32)], axis=0)
    sin_t = jnp.concatenate([jnp.tile(sin_h, (1, reps)), jnp.zeros((TM_PROJ, LANES), F32)], axis=0)
    return cos_t, sin_t


def _head_gain_row(qk_gain_l):
    ones = jnp.ones((Z_TILE,), F32)
    rep = lambda g: jnp.tile(g, Z_TILE // HEAD_DIM)
    akv = jnp.concatenate([jnp.tile(qk_gain_l[0, 1], A_KV_HEADS), jnp.ones((Z_TILE - A_KV_HEADS * HEAD_DIM,), F32)])
    return jnp.concatenate([
        rep(qk_gain_l[0, 0]), akv,
        rep(qk_gain_l[1, 0]), rep(qk_gain_l[1, 1]), ones, ones,
        rep(qk_gain_l[2, 0]), rep(qk_gain_l[2, 1]), ones,
    ])[None, :]


def kernel(x, c, ctx, c_ctx, w_ada, b_ada, g_norm1, g_norm2, w_in, b_gate, qk_gain, a_sink, b_rpb,
           c_w, c_scale, d_lambda, d_subln, w_branch, w_out, w_ff1, w_ff2):
    cos_t, sin_t = _rope_tables()
    lane = jnp.arange(LANES)
    bd = (lane[:, None] // HEAD_DIM == lane[None, :] // HEAD_DIM).astype(BF16)

    cc = jnp.concatenate([c, c_ctx[None, :], jnp.zeros((8 - BATCH - 1, D_MODEL), F32)], axis=0)
    mod_all = _modulation(cc, w_ada, b_ada)

    wb = w_branch.astype(BF16)
    wo = w_out.astype(BF16)
    w1 = w_ff1.astype(BF16)
    w2 = w_ff2.astype(BF16)
    cw = c_w.astype(BF16)

    xl = x.reshape(LATENT_ROWS, D_MODEL)
    xc = ctx.reshape(CTX_ROWS, D_MODEL)
    ctx_row = lambda i: BATCH
    batch_row = lambda tm: (lambda i: i // (SEQ // tm))
    mods = [mod_all[l].reshape(8, 1, 6 * D_MODEL) for l in range(DEPTH)]
    gn1 = [g_norm1[l][None, :] for l in range(DEPTH)]

    tm_merge, tm_mlp, tf_mlp, tn_gate = 256, 512, 1024, 1024
    h = _norm_mod(xl, gn1[0], mods[0], None, tm=tm_mlp, row0=0, mod_row=batch_row(tm_mlp))
    h = _norm_mod(xc, gn1[0], mods[0], h, tm=tm_mlp, row0=LATENT_ROWS, mod_row=ctx_row)

    for l in range(DEPTH):
        last = l == DEPTH - 1
        lam_init = 0.8 - 0.6 * math.exp(-0.3 * l)
        mod3 = mods[l]
        gn2 = g_norm2[l][None, :]
        cs = c_scale[l][None, :]
        subln = d_subln[l][None, :]

        gates = _gate_proj(h, w_in, b_gate[l][None, :], layer=l, tn=tn_gate)
        zh = _head_proj(h, w_in, _head_gain_row(qk_gain[l]), cos_t, sin_t, bd, layer=l)

        table = _rpb_table(b_rpb[l])
        ya = _win_gqa(zh, a_sink[l], latent=True)
        yb = _nbr_attn(zh, table)
        yc = _pool(zh, cw, cs, layer=l, n=SEQ, latent=True)
        yd = _diff_attn(zh, d_lambda[l], subln, lam_init=lam_init, latent=True)
        xl, h2 = _merge((ya, yb, yc, yd), gates, wb, wo, xl, gn2, mod3,
                        layer=l, tm=tm_merge, row0=0, mod_row=batch_row(tm_merge))
        next_norm = None if last else (gn1[l + 1], mods[l + 1], None)
        xl, h = _mlp(h2, xl, mod3, w1, w2, next_norm,
                     layer=l, tm=tm_mlp, tf=tf_mlp, row0=0, mod_row=batch_row(tm_mlp))
        if not last:
            yac = _win_gqa(zh, a_sink[l], latent=False)
            ybc = _ctx_mha(zh)
            ycc = _pool(zh, cw, cs, layer=l, n=CTX_LEN, latent=False)
            ydc = _diff_attn(zh, d_lambda[l], subln, lam_init=lam_init, latent=False)
            xc, h2c = _merge((yac, ybc, ycc, ydc), gates, wb, wo, xc, gn2, mod3,
                             layer=l, tm=tm_merge, row0=LATENT_ROWS, mod_row=ctx_row)
            xc, h = _mlp(h2c, xc, mod3, w1, w2, (gn1[l + 1], mods[l + 1], h),
                         layer=l, tm=tm_mlp, tf=tf_mlp, row0=LATENT_ROWS, mod_row=ctx_row)

    return xl.reshape(BATCH, SEQ, D_MODEL)
```

```python
import functools
import math

import jax
import jax.numpy as jnp
from jax import lax
from jax.experimental import pallas as pl
from jax.experimental.pallas import tpu as pltpu

D_MODEL = 2048
BATCH = 4
SEQ = 2048
DEPTH = 2
GRID_W = 64
GRID_H = SEQ // GRID_W
CTX_LEN = 256
HEAD_DIM = 64
N_BRANCH = 4
BRANCH_W = D_MODEL // N_BRANCH
A_HEADS = 8
A_KV_HEADS = 2
A_GROUP = A_HEADS // A_KV_HEADS
A_WINDOW = 128
A_BLOCK = 128
B_HEADS = 8
B_WIN_ROWS = 8
B_WIN_COLS = 16
C_WINDOWS = (2, 4, 8, 16)
C_GROUP_DIM = 128
D_HEADS = 4
D_FF = 4 * D_MODEL
ROPE_BASE = 10000.0
EPS = 1e-6
NEG = -1e30
ATTN_SCALE = HEAD_DIM ** -0.5

F32 = jnp.float32
BF16 = jnp.bfloat16

V7X_VMEM_LIMIT_BYTES = 56 * 1024 * 1024
LANES = 128

LATENT_ROWS = BATCH * SEQ
CTX_ROWS = BATCH * CTX_LEN
ALL_ROWS = LATENT_ROWS + CTX_ROWS

Z_TILE = 512
H_AQ, H_AKV, H_BQ, H_BK, H_BV, H_CU, H_DQ, H_DK, H_DV = range(9)
HEAD_TILES = 9
W_SRC = 256
GATE_COLS = N_BRANCH * D_MODEL
GATE_SRC0 = (12544 - GATE_COLS) // W_SRC
TM_PROJ = 1024
HEAD_SUM_LANES = 256


def _cparams(sem):
    return pltpu.CompilerParams(dimension_semantics=sem, vmem_limit_bytes=V7X_VMEM_LIMIT_BYTES)


def _dot_nt(a, b):
    return lax.dot_general(a, b, (((1,), (1,)), ((), ())), preferred_element_type=F32)


def _dot(a, b):
    return jnp.dot(a, b, preferred_element_type=F32)


def _ctx_blocks(block_rows):
    return LATENT_ROWS // block_rows


def _mod_kernel(c_ref, w_ref, b_ref, o_ref):
    c = c_ref[...]
    s = c * (0.5 * (jnp.tanh(0.5 * c) + 1.0))
    o_ref[0] = _dot(s, w_ref[0]) + b_ref[0]


def _modulation(cc, w_ada, b_ada):
    tn = 1024
    n = 6 * D_MODEL
    return pl.pallas_call(
        _mod_kernel,
        grid=(DEPTH, n // tn),
        in_specs=[
            pl.BlockSpec((8, D_MODEL), lambda l, j: (0, 0)),
            pl.BlockSpec((1, D_MODEL, tn), lambda l, j: (l, 0, j)),
            pl.BlockSpec((1, 1, tn), lambda l, j: (l, 0, j)),
        ],
        out_specs=pl.BlockSpec((1, 8, tn), lambda l, j: (l, 0, j)),
        out_shape=jax.ShapeDtypeStruct((DEPTH, 8, n), F32),
        compiler_params=_cparams(("parallel", "parallel")),
        name="modulation",
    )(cc, w_ada, b_ada.reshape(DEPTH, 1, n))


NORM_ROWS = 16


def _norm_modulate_rows(x_ref, gn_ref, shift_ref, scale_ref, h_ref):
    n = x_ref.shape[0]
    gs = gn_ref[...] * (1.0 + scale_ref[0])
    shift = shift_ref[0]

    def body(c, carry):
        rows = pl.ds(pl.multiple_of(c * NORM_ROWS, NORM_ROWS), NORM_ROWS)
        x = x_ref[rows, :]
        inv = lax.rsqrt(jnp.mean(x * x, axis=-1, keepdims=True) + EPS)
        h_ref[rows, :] = ((x * inv) * gs + shift).astype(BF16)
        return carry

    lax.fori_loop(0, n // NORM_ROWS, body, 0, unroll=8)


def _norm_mod_kernel(x_ref, gn_ref, shift_ref, scale_ref, *rest):
    _norm_modulate_rows(x_ref, gn_ref, shift_ref, scale_ref, rest[-1])


def _h_out(tm, row0, prev):
    spec = pl.BlockSpec((tm, D_MODEL), lambda i, *_: (row0 // tm + i, 0))
    shape = jax.ShapeDtypeStruct((ALL_ROWS, D_MODEL), BF16)
    return spec, shape


def _norm_mod(x, gn, mod3, prev, *, tm, row0, mod_row):
    m = x.shape[0]
    in_specs = [
        pl.BlockSpec((tm, D_MODEL), lambda i: (i, 0)),
        pl.BlockSpec((1, D_MODEL), lambda i: (0, 0)),
        pl.BlockSpec((1, 1, D_MODEL), lambda i: (mod_row(i), 0, 0)),
        pl.BlockSpec((1, 1, D_MODEL), lambda i: (mod_row(i), 0, 1)),
    ]
    args = [x, gn, mod3, mod3]
    aliases = {}
    if prev is not None:
        in_specs.append(pl.BlockSpec(memory_space=pl.ANY))
        args.append(prev)
        aliases = {len(args) - 1: 0}
    spec, shape = _h_out(tm, row0, prev)
    return pl.pallas_call(
        _norm_mod_kernel,
        grid=(m // tm,),
        in_specs=in_specs,
        out_specs=spec,
        out_shape=shape,
        input_output_aliases=aliases,
        compiler_params=_cparams(("parallel",)),
        name=f"norm_mod_{m}",
    )(*args)


def _swap16(y):
    lane = lax.broadcasted_iota(jnp.int32, y.shape, 1)
    fwd = pltpu.roll(y, LANES - 16, axis=1)
    bwd = pltpu.roll(y, 16, axis=1)
    return jnp.where((lane & 16) == 0, fwd, bwd)


def _cast_weight_blocks(w_refs, w_scr):
    @pl.when(pl.program_id(1) == 0)
    def _():
        for c, w_ref in enumerate(w_refs):
            w_scr[:, c * W_SRC:(c + 1) * W_SRC] = w_ref[0].astype(BF16)


def _gate_kernel(h_ref, *rest):
    *w_refs, bias_ref, g_ref, w_scr = rest
    _cast_weight_blocks(w_refs, w_scr)
    t = _dot(h_ref[...], w_scr[...]) + bias_ref[...]
    g_ref[...] = (0.5 * (jnp.tanh(0.5 * t) + 1.0)).astype(BF16)


def _gate_proj(h, w_in, bias, *, layer, tn):
    n_src = tn // W_SRC
    w_spec = lambda c: pl.BlockSpec((1, D_MODEL, W_SRC), lambda n, i: (layer, 0, GATE_SRC0 + n * n_src + c))
    return pl.pallas_call(
        _gate_kernel,
        grid=(GATE_COLS // tn, ALL_ROWS // TM_PROJ),
        in_specs=[pl.BlockSpec((TM_PROJ, D_MODEL), lambda n, i: (i, 0))]
        + [w_spec(c) for c in range(n_src)]
        + [pl.BlockSpec((1, tn), lambda n, i: (0, n))],
        out_specs=pl.BlockSpec((TM_PROJ, tn), lambda n, i: (i, n)),
        out_shape=jax.ShapeDtypeStruct((ALL_ROWS, GATE_COLS), BF16),
        scratch_shapes=[pltpu.VMEM((D_MODEL, tn), BF16)],
        compiler_params=_cparams(("parallel", "arbitrary")),
        name="gate_proj",
    )(h, *([w_in] * n_src), bias)


def _head_src_block(t, half):
    return jnp.where(t == H_AQ, half, jnp.where(t == H_AKV, 2, 2 * t - 1 + half))


def _head_kernel(h_ref, wa_ref, wb_ref, gain_ref, cos_ref, sin_ref, bd_ref, z_ref, w_scr):
    t = pl.program_id(0)
    _cast_weight_blocks((wa_ref, wb_ref), w_scr)

    def project():
        return _dot(h_ref[...], w_scr[...])

    def norm_tile(n_norm_chunks, with_rope):
        wide = bd_ref.shape[0]
        half_rows = TM_PROJ // 2
        for r0 in range(0, TM_PROJ, half_rows):
            rows = slice(r0, r0 + half_rows)
            acc = _dot(h_ref[rows, :], w_scr[...])
            for c2 in range(Z_TILE // wide):
                a = acc[:, c2 * wide:(c2 + 1) * wide]
                normed = a
                if c2 * (wide // LANES) < n_norm_chunks:
                    ms = _dot((a * a).astype(BF16), bd_ref[...]) * (1.0 / HEAD_DIM)
                    normed = a * lax.rsqrt(ms + EPS) * gain_ref[:, c2 * wide:(c2 + 1) * wide]
                for c in range(wide // LANES):
                    chunk = c2 * (wide // LANES) + c
                    y = (normed if chunk < n_norm_chunks else a)[:, c * LANES:(c + 1) * LANES]
                    if with_rope and chunk < n_norm_chunks:
                        y = y * cos_ref[rows, :] + _swap16(y) * sin_ref[rows, :]
                    z_ref[rows, chunk * LANES:(chunk + 1) * LANES] = y.astype(BF16)

    @pl.when((t == H_AQ) | (t == H_DQ) | (t == H_DK))
    def _():
        norm_tile(4, True)

    @pl.when(t == H_AKV)
    def _():
        norm_tile(1, True)

    @pl.when((t == H_BQ) | (t == H_BK))
    def _():
        norm_tile(4, False)

    @pl.when((t == H_BV) | (t == H_CU) | (t == H_DV))
    def _():
        z_ref[...] = project().astype(BF16)


def _head_proj(h, w_in, gain, cos_t, sin_t, bd, *, layer):
    seq_tiles = SEQ // TM_PROJ
    rope_block = lambda i: jnp.where(i < LATENT_ROWS // TM_PROJ, i % seq_tiles, seq_tiles)
    w_spec = lambda half: pl.BlockSpec((1, D_MODEL, W_SRC), lambda t, i: (layer, 0, _head_src_block(t, half)))
    return pl.pallas_call(
        _head_kernel,
        grid=(HEAD_TILES, ALL_ROWS // TM_PROJ),
        in_specs=[
            pl.BlockSpec((TM_PROJ, D_MODEL), lambda t, i: (i, 0)),
            w_spec(0), w_spec(1),
            pl.BlockSpec((1, Z_TILE), lambda t, i: (0, t)),
            pl.BlockSpec((TM_PROJ, LANES), lambda t, i: (rope_block(i), 0)),
            pl.BlockSpec((TM_PROJ, LANES), lambda t, i: (rope_block(i), 0)),
            pl.BlockSpec(bd.shape, lambda t, i: (0, 0)),
        ],
        out_specs=pl.BlockSpec((None, TM_PROJ, Z_TILE), lambda t, i: (t, i, 0)),
        out_shape=jax.ShapeDtypeStruct((HEAD_TILES, ALL_ROWS, Z_TILE), BF16),
        scratch_shapes=[pltpu.VMEM((D_MODEL, Z_TILE), BF16)],
        compiler_params=_cparams(("parallel", "arbitrary")),
        name="head_proj",
    )(h, w_in, w_in, gain, cos_t, sin_t, bd)


def _softmax_attend(q, segs, extra_logit=None):
    scores = []
    for k, _, bias in segs:
        s = _dot_nt(q, k)
        if bias is not None:
            s = s + bias
        scores.append(s)
    m = scores[0].max(axis=-1, keepdims=True)
    for s in scores[1:]:
        m = jnp.maximum(m, s.max(axis=-1, keepdims=True))
    if extra_logit is not None:
        m = jnp.maximum(m, extra_logit)
    l = None
    o = None
    for s, (_, v, _) in zip(scores, segs):
        p = jnp.exp(s - m)
        ls = p.sum(axis=-1, keepdims=True)
        os_ = _dot(p.astype(BF16), v)
        l = ls if l is None else l + ls
        o = os_ if o is None else o + os_
    if extra_logit is not None:
        l = l + jnp.exp(extra_logit - m)
    return o, l


def _lane_group(shape, width):
    return lax.broadcasted_iota(jnp.int32, shape, 1) // width


def _zspec(tile, rows, index_rows, width=Z_TILE):
    return pl.BlockSpec((None, rows, width), lambda *ids: (tile, index_rows(*ids), 0))


def _repeat_kv_heads(kv):
    pieces = [kv[:, i * HEAD_DIM:(i + 1) * HEAD_DIM] for i in range(2 * A_KV_HEADS)]
    k = jnp.concatenate([pieces[g] for g in range(A_KV_HEADS) for _ in range(A_GROUP)], axis=1)
    v = jnp.concatenate([pieces[A_KV_HEADS + g] for g in range(A_KV_HEADS) for _ in range(A_GROUP)], axis=1)
    return k, v


WIN_BLOCKS_PER_STEP = 8


def _win_gqa_kernel(sink_ref, q_ref, *rest, latent):
    if latent:
        kv_ref, ckv_ref, o_ref, k_scr, v_scr, kc_scr, vc_scr = rest
    else:
        ckv_ref, o_ref, kc_scr, vc_scr = rest
    gw = A_GROUP * HEAD_DIM

    def build():
        kc_scr[...], vc_scr[...] = _repeat_kv_heads(ckv_ref[...])
        if latent:
            k_scr[...], v_scr[...] = _repeat_kv_heads(kv_ref[...])

    if latent:
        pl.when(pl.program_id(1) == 0)(build)
    else:
        build()

    tq = A_BLOCK if latent else q_ref.shape[0]
    n_sub = q_ref.shape[0] // tq
    rows = A_GROUP * tq
    row_head = lax.broadcasted_iota(jnp.int32, (rows, 1), 0) // tq
    lane_head = _lane_group((tq, gw), HEAD_DIM)
    for j in range(n_sub):
        q = q_ref[j * tq:(j + 1) * tq, :] * ATTN_SCALE
        if latent:
            n = pl.program_id(1) * n_sub + j
            span = 3 * A_BLOCK
            start = pl.multiple_of(jnp.clip((n - 1) * A_BLOCK, 0, SEQ - span), A_BLOCK)
            qpos = n * A_BLOCK + lax.broadcasted_iota(jnp.int32, (tq, span), 0)
            kpos = start + lax.broadcasted_iota(jnp.int32, (tq, span), 1)
            band = jnp.where(jnp.abs(qpos - kpos) <= A_WINDOW, 0.0, NEG).astype(F32)
            band = jnp.concatenate([band] * A_GROUP, axis=0)
        for g in range(A_KV_HEADS):
            gs = slice(g * gw, (g + 1) * gw)
            qg = q[:, gs]
            zero = jnp.zeros_like(qg)
            q_stack = jnp.concatenate([jnp.where(lane_head == r, qg, zero) for r in range(A_GROUP)], axis=0)
            sink = jnp.zeros((rows, 1), F32)
            for r in range(A_GROUP):
                sink = jnp.where(row_head == r, sink_ref[g * A_GROUP + r], sink)
            segs = [(kc_scr[:, gs], vc_scr[:, gs], None)]
            if latent:
                segs.append((k_scr[pl.ds(start, span), gs], v_scr[pl.ds(start, span), gs], band))
            o, l = _softmax_attend(q_stack, segs, extra_logit=sink)
            o = o / l
            og = jnp.zeros((tq, gw), F32)
            for r in range(A_GROUP):
                og = jnp.where(lane_head == r, o[r * tq:(r + 1) * tq], og)
            o_ref[j * tq:(j + 1) * tq, gs] = og.astype(BF16)


def _win_gqa(zh, sink, *, latent):
    kvb = 2 * A_KV_HEADS * HEAD_DIM
    c0 = _ctx_blocks(CTX_LEN)
    smem = pl.BlockSpec(memory_space=pltpu.SMEM)
    rep = lambda n: pltpu.VMEM((n, A_HEADS * HEAD_DIM), BF16)
    if latent:
        tq = WIN_BLOCKS_PER_STEP * A_BLOCK
        nq = SEQ // tq
        grid = (BATCH, nq)
        in_specs = [
            smem,
            _zspec(H_AQ, tq, lambda b, n: b * nq + n),
            _zspec(H_AKV, SEQ, lambda b, n: b, kvb),
            _zspec(H_AKV, CTX_LEN, lambda b, n: c0 + b, kvb),
        ]
        out_specs = pl.BlockSpec((tq, Z_TILE), lambda b, n: (b * nq + n, 0))
        args = (sink, zh, zh, zh)
        rows = LATENT_ROWS
        sem = ("parallel", "arbitrary")
        scratch = [rep(SEQ), rep(SEQ), rep(CTX_LEN), rep(CTX_LEN)]
    else:
        grid = (BATCH,)
        in_specs = [
            smem,
            _zspec(H_AQ, CTX_LEN, lambda b: c0 + b),
            _zspec(H_AKV, CTX_LEN, lambda b: c0 + b, kvb),
        ]
        out_specs = pl.BlockSpec((CTX_LEN, Z_TILE), lambda b: (b, 0))
        args = (sink, zh, zh)
        rows = CTX_ROWS
        sem = ("parallel",)
        scratch = [rep(CTX_LEN), rep(CTX_LEN)]
    return pl.pallas_call(
        functools.partial(_win_gqa_kernel, latent=latent),
        grid=grid,
        in_specs=in_specs,
        out_specs=out_specs,
        out_shape=jax.ShapeDtypeStruct((rows, BRANCH_W), BF16),
        scratch_shapes=scratch,
        compiler_params=_cparams(sem),
        name="win_gqa" if latent else "win_gqa_ctx",
    )(*args)


N_ROW_OFFSETS = 2 * B_WIN_ROWS - 1
N_COL_OFFSETS = 2 * B_WIN_COLS - 1
N_PAIR = N_ROW_OFFSETS + 1
NBR_QROWS = 4
NBR_SLAB = NBR_QROWS + B_WIN_ROWS
NBR_GROUPS = GRID_H // NBR_QROWS
NBR_CASES = 3


def _nbr_case_geometry(case):
    if case == 0:
        return B_WIN_ROWS - 1, lambda a, c: 0 <= c < B_WIN_ROWS
    if case == 1:
        return B_WIN_ROWS - 1 - NBR_QROWS, lambda a, c: 0 <= c - a < B_WIN_ROWS
    return B_WIN_ROWS - 1 - 2 * NBR_QROWS, lambda a, c: NBR_QROWS <= c < NBR_SLAB


def _rpb_table_kernel(rpb_ref, o_ref, pair_scr):
    h = pl.program_id(0)
    shape = (GRID_W, LANES)
    qcol = lax.broadcasted_iota(jnp.int32, shape, 0)
    lane = lax.broadcasted_iota(jnp.int32, shape, 1)
    kcol = lane & (GRID_W - 1)
    hi = lane >= GRID_W
    cs = jnp.clip(qcol - B_WIN_COLS // 2, 0, GRID_W - B_WIN_COLS)
    in_window = (kcol >= cs) & (kcol < cs + B_WIN_COLS)
    ci = kcol - qcol + (B_WIN_COLS - 1)
    neg = jnp.full(shape, NEG, F32)

    def rpb(d, c):
        return rpb_ref[(h * N_ROW_OFFSETS + d) * N_COL_OFFSETS + c]

    for p in range(N_PAIR):
        acc = jnp.zeros(shape, F32)
        for c in range(N_COL_OFFSETS):
            hit = ci == c
            if p - 1 >= 0:
                acc = jnp.where(hit & ~hi, rpb(p - 1, c), acc)
            if p < N_ROW_OFFSETS:
                acc = jnp.where(hit & hi, rpb(p, c), acc)
        pair_scr[p] = jnp.where(in_window, acc, NEG)

    for case in range(NBR_CASES):
        shift, in_rows = _nbr_case_geometry(case)
        for a in range(NBR_QROWS):
            for cp in range(NBR_SLAB // 2):
                c = 2 * cp
                lo_ok, hi_ok = in_rows(a, c), in_rows(a, c + 1)
                if lo_ok or hi_ok:
                    piece = pair_scr[c - a + shift + 1]
                    if not lo_ok:
                        piece = jnp.where(hi, piece, neg)
                    if not hi_ok:
                        piece = jnp.where(hi, neg, piece)
                else:
                    piece = neg
                o_ref[case, 0, a * GRID_W:(a + 1) * GRID_W, cp * LANES:(cp + 1) * LANES] = piece


def _rpb_table(rpb):
    nq, nk = NBR_QROWS * GRID_W, NBR_SLAB * GRID_W
    return pl.pallas_call(
        _rpb_table_kernel,
        grid=(B_HEADS,),
        in_specs=[pl.BlockSpec(memory_space=pltpu.SMEM)],
        out_specs=pl.BlockSpec((NBR_CASES, 1, nq, nk), lambda h: (0, h, 0, 0)),
        out_shape=jax.ShapeDtypeStruct((NBR_CASES, B_HEADS, nq, nk), F32),
        scratch_shapes=[pltpu.VMEM((N_PAIR, GRID_W, LANES), F32)],
        compiler_params=_cparams(("parallel",)),
        name="rpb_table",
    )(rpb.reshape(-1))


def _pair_attend(q, lane_hi, segs_of_head):
    zero = jnp.zeros_like(q)
    halves = []
    for hh in range(2):
        qm = jnp.where(lane_hi, q, zero) if hh else jnp.where(lane_hi, zero, q)
        o, l = _softmax_attend(qm, segs_of_head(hh))
        halves.append(o / l)
    return jnp.where(lane_hi, halves[1], halves[0])


NBR_GROUPS_PER_STEP = 2


def _nbr_kernel(q_ref, k_ref, v_ref, kc_ref, vc_ref, bias_ref, o_ref):
    nq = NBR_QROWS * GRID_W
    nk = NBR_SLAB * GRID_W
    lane_hi = lax.broadcasted_iota(jnp.int32, (nq, LANES), 1) >= HEAD_DIM
    for j in range(NBR_GROUPS_PER_STEP):
        g = pl.program_id(1) * NBR_GROUPS_PER_STEP + j
        case = jnp.where(g == 0, 0, jnp.where(g == NBR_GROUPS - 1, 2, 1))
        r0 = jnp.clip(NBR_QROWS * g - B_WIN_ROWS // 2, 0, GRID_H - NBR_SLAB)
        base = pl.multiple_of(r0 * GRID_W, GRID_W)
        rows = slice(j * nq, (j + 1) * nq)
        for pr in range(B_HEADS // 2):
            sl = slice(pr * LANES, (pr + 1) * LANES)
            kp = k_ref[pl.ds(base, nk), sl]
            vp = v_ref[pl.ds(base, nk), sl]
            kcp = kc_ref[:, sl]
            vcp = vc_ref[:, sl]
            segs = lambda hh: [(kp, vp, bias_ref[case, 2 * pr + hh]), (kcp, vcp, None)]
            q = q_ref[rows, sl] * ATTN_SCALE
            o_ref[rows, sl] = _pair_attend(q, lane_hi, segs).astype(BF16)


def _nbr_attn(zh, table):
    nq = NBR_GROUPS_PER_STEP * NBR_QROWS * GRID_W
    steps = NBR_GROUPS // NBR_GROUPS_PER_STEP
    c0 = _ctx_blocks(CTX_LEN)
    return pl.pallas_call(
        _nbr_kernel,
        grid=(BATCH, steps),
        in_specs=[
            _zspec(H_BQ, nq, lambda b, s: b * steps + s),
            _zspec(H_BK, SEQ, lambda b, s: b),
            _zspec(H_BV, SEQ, lambda b, s: b),
            _zspec(H_BK, CTX_LEN, lambda b, s: c0 + b),
            _zspec(H_BV, CTX_LEN, lambda b, s: c0 + b),
            pl.BlockSpec(table.shape, lambda b, s: (0, 0, 0, 0), pipeline_mode=pl.Buffered(1)),
        ],
        out_specs=pl.BlockSpec((nq, Z_TILE), lambda b, s: (b * steps + s, 0)),
        out_shape=jax.ShapeDtypeStruct((LATENT_ROWS, BRANCH_W), BF16),
        compiler_params=_cparams(("parallel", "arbitrary")),
        name="nbr_attn",
    )(zh, zh, zh, zh, zh, table)


def _ctx_mha_kernel(q_ref, k_ref, v_ref, o_ref):
    q = q_ref[...] * ATTN_SCALE
    lane_hi = lax.broadcasted_iota(jnp.int32, (q.shape[0], LANES), 1) >= HEAD_DIM
    for pr in range(B_HEADS // 2):
        sl = slice(pr * LANES, (pr + 1) * LANES)
        segs = lambda hh: [(k_ref[:, sl], v_ref[:, sl], None)]
        o_ref[:, sl] = _pair_attend(q[:, sl], lane_hi, segs).astype(BF16)


def _ctx_mha(zh):
    c0 = _ctx_blocks(CTX_LEN)
    rows = lambda b: c0 + b
    return pl.pallas_call(
        _ctx_mha_kernel,
        grid=(BATCH,),
        in_specs=[_zspec(H_BQ, CTX_LEN, rows), _zspec(H_BK, CTX_LEN, rows), _zspec(H_BV, CTX_LEN, rows)],
        out_specs=pl.BlockSpec((CTX_LEN, Z_TILE), lambda b: (b, 0)),
        out_shape=jax.ShapeDtypeStruct((CTX_ROWS, BRANCH_W), BF16),
        compiler_params=_cparams(("parallel",)),
        name="nbr_attn_ctx",
    )(zh, zh, zh)


def _pool_kernel(u_ref, w_ref, scale_ref, o_ref):
    n = u_ref.shape[0]
    t = lax.broadcasted_iota(jnp.int32, (n, C_GROUP_DIM), 0)

    def down(a, k):
        return jnp.where(t >= k, pltpu.roll(a, k, axis=0), 0.0)

    def up(a, k):
        return jnp.where(t < n - k, pltpu.roll(a, n - k, axis=0), 0.0)

    for g, w in enumerate(C_WINDOWS):
        sl = slice(g * C_GROUP_DIM, (g + 1) * C_GROUP_DIM)
        u = u_ref[:, sl].astype(F32)
        half = w // 2
        back = u
        fwd = u
        k = 1
        while k < half:
            back = back + down(back, k)
            fwd = fwd + up(fwd, k)
            k *= 2
        total = down(back, 1) + fwd
        lo = jnp.maximum(t - half, 0)
        hi = jnp.minimum(t + half - 1, n - 1)
        cnt = (hi - lo + 1).astype(F32)
        pooled = total / cnt - u
        y = _dot(pooled.astype(BF16), w_ref[0, g]) * scale_ref[:, sl]
        o_ref[:, sl] = y.astype(BF16)


def _pool(zh, c_w, c_scale, *, layer, n, latent):
    rows = LATENT_ROWS if latent else CTX_ROWS
    b0 = 0 if latent else _ctx_blocks(n)
    return pl.pallas_call(
        _pool_kernel,
        grid=(rows // n,),
        in_specs=[
            _zspec(H_CU, n, lambda b: b0 + b),
            pl.BlockSpec((1, len(C_WINDOWS), C_GROUP_DIM, C_GROUP_DIM), lambda b: (layer, 0, 0, 0)),
            pl.BlockSpec((1, BRANCH_W), lambda b: (0, 0)),
        ],
        out_specs=pl.BlockSpec((n, BRANCH_W), lambda b: (b, 0)),
        out_shape=jax.ShapeDtypeStruct((rows, BRANCH_W), BF16),
        compiler_params=_cparams(("parallel",)),
        name=f"pool_{n}",
    )(zh, c_w, c_scale)


DIFF_ROWS = 256


def _diff_kernel(lam_ref, q_ref, *rest, lam_init, latent):
    if latent:
        k_ref, v_ref, kc_ref, vc_ref, g_ref, o_ref = rest
    else:
        kc_ref, vc_ref, g_ref, o_ref = rest
    dl = lam_ref[...]
    lam = (jnp.exp(jnp.sum(dl[0:1] * dl[1:2], axis=-1, keepdims=True))
           - jnp.exp(jnp.sum(dl[2:3] * dl[3:4], axis=-1, keepdims=True)) + lam_init)
    hw = 2 * HEAD_DIM
    tq = min(q_ref.shape[0], DIFF_ROWS)
    lane_hi = lax.broadcasted_iota(jnp.int32, (tq, hw), 1) >= HEAD_DIM
    for r0 in range(0, q_ref.shape[0], tq):
        rows = slice(r0, r0 + tq)
        for h in range(D_HEADS):
            sl = slice(h * hw, (h + 1) * hw)
            qh = q_ref[rows, sl] * ATTN_SCALE
            zero = jnp.zeros_like(qh)
            q_stack = jnp.concatenate([jnp.where(lane_hi, zero, qh), jnp.where(lane_hi, qh, zero)], axis=0)
            segs = [(kc_ref[:, sl], vc_ref[:, sl], None)]
            if latent:
                segs.append((k_ref[:, sl], v_ref[:, sl], None))
            o, l = _softmax_attend(q_stack, segs)
            o = o / l
            od = o[:tq] - lam * o[tq:]
            ms = jnp.mean(od * od, axis=-1, keepdims=True)
            y = od * lax.rsqrt(ms + EPS) * g_ref[...] * (1.0 - lam_init)
            o_ref[rows, sl] = y.astype(BF16)


def _diff_attn(zh, d_lambda, subln, *, lam_init, latent):
    full = lambda shape: pl.BlockSpec(shape, lambda *_: (0,) * len(shape))
    c0 = _ctx_blocks(CTX_LEN)
    if latent:
        tq = 2 * DIFF_ROWS
        nq = SEQ // tq
        grid = (BATCH, nq)
        in_specs = [
            full((4, HEAD_DIM)),
            _zspec(H_DQ, tq, lambda b, n: b * nq + n),
            _zspec(H_DK, SEQ, lambda b, n: b),
            _zspec(H_DV, SEQ, lambda b, n: b),
            _zspec(H_DK, CTX_LEN, lambda b, n: c0 + b),
            _zspec(H_DV, CTX_LEN, lambda b, n: c0 + b),
            full((1, 2 * HEAD_DIM)),
        ]
        out_specs = pl.BlockSpec((tq, Z_TILE), lambda b, n: (b * nq + n, 0))
        args = (d_lambda, zh, zh, zh, zh, zh, subln)
        rows = LATENT_ROWS
        sem = ("parallel", "arbitrary")
    else:
        grid = (BATCH,)
        crow = lambda b: c0 + b
        in_specs = [
            full((4, HEAD_DIM)),
            _zspec(H_DQ, CTX_LEN, crow),
            _zspec(H_DK, CTX_LEN, crow),
            _zspec(H_DV, CTX_LEN, crow),
            full((1, 2 * HEAD_DIM)),
        ]
        out_specs = pl.BlockSpec((CTX_LEN, Z_TILE), lambda b: (b, 0))
        args = (d_lambda, zh, zh, zh, subln)
        rows = CTX_ROWS
        sem = ("parallel",)
    return pl.pallas_call(
        functools.partial(_diff_kernel, lam_init=lam_init, latent=latent),
        grid=grid,
        in_specs=in_specs,
        out_specs=out_specs,
        out_shape=jax.ShapeDtypeStruct((rows, BRANCH_W), BF16),
        compiler_params=_cparams(sem),
        name="diff_attn" if latent else "diff_attn_ctx",
    )(*args)


def _merge_kernel(*refs):
    y_refs = refs[:N_BRANCH]
    g_refs = refs[N_BRANCH:2 * N_BRANCH]
    wb_ref, wo_ref, x_ref, gt_ref, gn_ref, shift_ref, scale_ref, xo_ref, h_ref = refs[2 * N_BRANCH:]
    mixed = None
    for k in range(N_BRANCH):
        proj = _dot(y_refs[k][...], wb_ref[0, k]) * g_refs[k][...].astype(F32)
        mixed = proj if mixed is None else mixed + proj
    out = _dot(mixed.astype(BF16), wo_ref[0])
    xo_ref[...] = x_ref[...] + gt_ref[0] * out
    _norm_modulate_rows(xo_ref, gn_ref, shift_ref, scale_ref, h_ref)


def _merge(ys, gates, w_branch, w_out, x, gn2, mod3, *, layer, tm, row0, mod_row):
    m = x.shape[0]
    row = lambda i: (i, 0)
    g0 = row0 // tm
    mod = lambda chunk: pl.BlockSpec((1, 1, D_MODEL), lambda i: (mod_row(i), 0, chunk))
    resident = pl.Buffered(1)
    return pl.pallas_call(
        _merge_kernel,
        grid=(m // tm,),
        in_specs=[pl.BlockSpec((tm, BRANCH_W), row)] * N_BRANCH
        + [pl.BlockSpec((tm, D_MODEL), lambda i, k=k: (g0 + i, k)) for k in range(N_BRANCH)]
        + [
            pl.BlockSpec((1, N_BRANCH, BRANCH_W, D_MODEL), lambda i: (layer, 0, 0, 0), pipeline_mode=resident),
            pl.BlockSpec((1, D_MODEL, D_MODEL), lambda i: (layer, 0, 0), pipeline_mode=resident),
            pl.BlockSpec((tm, D_MODEL), row),
            mod(2),
            pl.BlockSpec((1, D_MODEL), lambda i: (0, 0)),
            mod(3),
            mod(4),
        ],
        out_specs=[pl.BlockSpec((tm, D_MODEL), row), pl.BlockSpec((tm, D_MODEL), row)],
        out_shape=[jax.ShapeDtypeStruct((m, D_MODEL), F32), jax.ShapeDtypeStruct((m, D_MODEL), BF16)],
        compiler_params=_cparams(("parallel",)),
        name=f"merge_{m}",
    )(*ys, *([gates] * N_BRANCH), w_branch, w_out, x, mod3, gn2, mod3, mod3)


def _mlp_kernel(h_ref, x_ref, gt_ref, w1_ref, w2_ref, *rest, emit_next):
    if emit_next:
        gn_ref, shift_ref, scale_ref, *_, xo_ref, hn_ref = rest
    else:
        (xo_ref,) = rest
    f = pl.program_id(1)

    @pl.when(f == 0)
    def _():
        xo_ref[...] = jnp.zeros_like(xo_ref)

    a = jnp.maximum(_dot(h_ref[...], w1_ref[0]), 0.0)
    xo_ref[...] += _dot((a * a).astype(BF16), w2_ref[0])

    @pl.when(f == pl.num_programs(1) - 1)
    def _():
        xo_ref[...] = x_ref[...] + gt_ref[0] * xo_ref[...]
        if emit_next:
            _norm_modulate_rows(xo_ref, gn_ref, shift_ref, scale_ref, hn_ref)


def _mlp(h, x, mod3, w1, w2, next_norm, *, layer, tm, tf, row0, mod_row):
    m = x.shape[0]
    emit_next = next_norm is not None
    row = lambda i, f: (i, 0)
    in_specs = [
        pl.BlockSpec((tm, D_MODEL), row),
        pl.BlockSpec((tm, D_MODEL), row),
        pl.BlockSpec((1, 1, D_MODEL), lambda i, f: (mod_row(i), 0, 5)),
        pl.BlockSpec((1, D_MODEL, tf), lambda i, f: (layer, 0, f)),
        pl.BlockSpec((1, tf, D_MODEL), lambda i, f: (layer, f, 0)),
    ]
    args = [h, x, mod3, w1, w2]
    out_specs = [pl.BlockSpec((tm, D_MODEL), row)]
    out_shape = [jax.ShapeDtypeStruct((m, D_MODEL), F32)]
    aliases = {}
    if emit_next:
        gn_next, mod3_next, prev = next_norm
        in_specs += [
            pl.BlockSpec((1, D_MODEL), lambda i, f: (0, 0)),
            pl.BlockSpec((1, 1, D_MODEL), lambda i, f: (mod_row(i), 0, 0)),
            pl.BlockSpec((1, 1, D_MODEL), lambda i, f: (mod_row(i), 0, 1)),
        ]
        args += [gn_next, mod3_next, mod3_next]
        if prev is not None:
            in_specs.append(pl.BlockSpec(memory_space=pl.ANY))
            args.append(prev)
            aliases = {len(args) - 1: 1}
        spec, shape = _h_out(tm, row0, prev)
        out_specs.append(spec)
        out_shape.append(shape)
    outs = pl.pallas_call(
        functools.partial(_mlp_kernel, emit_next=emit_next),
        grid=(m // tm, D_FF // tf),
        in_specs=in_specs,
        out_specs=out_specs,
        out_shape=out_shape,
        input_output_aliases=aliases,
        compiler_params=_cparams(("parallel", "arbitrary")),
        name=f"mlp_{m}",
    )(*args)
    return outs if emit_next else (outs[0], None)


def _rope_tables():
    t = jnp.arange(SEQ)
    row = (t // GRID_W).astype(F32)
    col = (t % GRID_W).astype(F32)
    n_freq = HEAD_DIM // 4
    inv = ROPE_BASE ** (-jnp.arange(n_freq, dtype=F32) / n_freq)
    ar = row[:, None] * inv
    ac = col[:, None] * inv
    cos_h = jnp.concatenate([jnp.cos(ar), jnp.cos(ar), jnp.cos(ac), jnp.cos(ac)], axis=-1)
    sin_h = jnp.concatenate([-jnp.sin(ar), jnp.sin(ar), -jnp.sin(ac), jnp.sin(ac)], axis=-1)
    reps = LANES // HEAD_DIM
    cos_t = jnp.concatenate([jnp.tile(cos_h, (1, reps)), jnp.ones((TM_PROJ, LANES), F32)], axis=0)
    sin_t = jnp.concatenate([jnp.tile(sin_h, (1, reps)), jnp.zeros((TM_PROJ, LANES), F32)], axis=0)
    return cos_t, sin_t


def _head_gain_row(qk_gain_l):
    ones = jnp.ones((Z_TILE,), F32)
    rep = lambda g: jnp.tile(g, Z_TILE // HEAD_DIM)
    akv = jnp.concatenate([jnp.tile(qk_gain_l[0, 1], A_KV_HEADS), jnp.ones((Z_TILE - A_KV_HEADS * HEAD_DIM,), F32)])
    return jnp.concatenate([
        rep(qk_gain_l[0, 0]), akv,
        rep(qk_gain_l[1, 0]), rep(qk_gain_l[1, 1]), ones, ones,
        rep(qk_gain_l[2, 0]), rep(qk_gain_l[2, 1]), ones,
    ])[None, :]


def kernel(x, c, ctx, c_ctx, w_ada, b_ada, g_norm1, g_norm2, w_in, b_gate, qk_gain, a_sink, b_rpb,
           c_w, c_scale, d_lambda, d_subln, w_branch, w_out, w_ff1, w_ff2):
    cos_t, sin_t = _rope_tables()
    lane = jnp.arange(HEAD_SUM_LANES)
    bd = (lane[:, None] // HEAD_DIM == lane[None, :] // HEAD_DIM).astype(BF16)

    cc = jnp.concatenate([c, c_ctx[None, :], jnp.zeros((8 - BATCH - 1, D_MODEL), F32)], axis=0)
    mod_all = _modulation(cc, w_ada, b_ada)

    wb = w_branch.astype(BF16)
    wo = w_out.astype(BF16)
    w1 = w_ff1.astype(BF16)
    w2 = w_ff2.astype(BF16)
    cw = c_w.astype(BF16)

    xl = x.reshape(LATENT_ROWS, D_MODEL)
    xc = ctx.reshape(CTX_ROWS, D_MODEL)
    ctx_row = lambda i: BATCH
    batch_row = lambda tm: (lambda i: i // (SEQ // tm))
    mods = [mod_all[l].reshape(8, 1, 6 * D_MODEL) for l in range(DEPTH)]
    gn1 = [g_norm1[l][None, :] for l in range(DEPTH)]

    tm_merge, tm_mlp, tf_mlp, tn_gate = 256, 512, 1024, 1024
    h = _norm_mod(xl, gn1[0], mods[0], None, tm=tm_mlp, row0=0, mod_row=batch_row(tm_mlp))
    h = _norm_mod(xc, gn1[0], mods[0], h, tm=tm_mlp, row0=LATENT_ROWS, mod_row=ctx_row)

    for l in range(DEPTH):
        last = l == DEPTH - 1
        lam_init = 0.8 - 0.6 * math.exp(-0.3 * l)
        mod3 = mods[l]
        gn2 = g_norm2[l][None, :]
        cs = c_scale[l][None, :]
        subln = d_subln[l][None, :]

        gates = _gate_proj(h, w_in, b_gate[l][None, :], layer=l, tn=tn_gate)
        zh = _head_proj(h, w_in, _head_gain_row(qk_gain[l]), cos_t, sin_t, bd, layer=l)

        table = _rpb_table(b_rpb[l])
        ya = _win_gqa(zh, a_sink[l], latent=True)
        yb = _nbr_attn(zh, table)
        yc = _pool(zh, cw, cs, layer=l, n=SEQ, latent=True)
        yd = _diff_attn(zh, d_lambda[l], subln, lam_init=lam_init, latent=True)
        xl, h2 = _merge((ya, yb, yc, yd), gates, wb, wo, xl, gn2, mod3,
                        layer=l, tm=tm_merge, row0=0, mod_row=batch_row(tm_merge))
        next_norm = None if last else (gn1[l + 1], mods[l + 1], None)
        xl, h = _mlp(h2, xl, mod3, w1, w2, next_norm,
                     layer=l, tm=tm_mlp, tf=tf_mlp, row0=0, mod_row=batch_row(tm_mlp))
        if not last:
            yac = _win_gqa(zh, a_sink[l], latent=False)
            ybc = _ctx_mha(zh)
            ycc = _pool(zh, cw, cs, layer=l, n=CTX_LEN, latent=False)
            ydc = _diff_attn(zh, d_lambda[l], subln, lam_init=lam_init, latent=False)
            xc, h2c = _merge((yac, ybc, ycc, ydc), gates, wb, wo, xc, gn2, mod3,
                             layer=l, tm=tm_merge, row0=LATENT_ROWS, mod_row=ctx_row)
            xc, h = _mlp(h2c, xc, mod3, w1, w2, (gn1[l + 1], mods[l + 1], h),
                         layer=l, tm=tm_mlp, tf=tf_mlp, row0=LATENT_ROWS, mod_row=ctx_row)

    return xl.reshape(BATCH, SEQ, D_MODEL)
```

```python
import functools
import math

import jax
import jax.numpy as jnp
from jax import lax
from jax.experimental import pallas as pl
from jax.experimental.pallas import tpu as pltpu

D_MODEL = 2048
BATCH = 4
SEQ = 2048
DEPTH = 2
GRID_W = 64
GRID_H = SEQ // GRID_W
CTX_LEN = 256
HEAD_DIM = 64
N_BRANCH = 4
BRANCH_W = D_MODEL // N_BRANCH
A_HEADS = 8
A_KV_HEADS = 2
A_GROUP = A_HEADS // A_KV_HEADS
A_WINDOW = 128
A_BLOCK = 128
B_HEADS = 8
B_WIN_ROWS = 8
B_WIN_COLS = 16
C_WINDOWS = (2, 4, 8, 16)
C_GROUP_DIM = 128
D_HEADS = 4
D_FF = 4 * D_MODEL
ROPE_BASE = 10000.0
EPS = 1e-6
NEG = -1e30
ATTN_SCALE = HEAD_DIM ** -0.5

F32 = jnp.float32
BF16 = jnp.bfloat16

V7X_VMEM_LIMIT_BYTES = 56 * 1024 * 1024
LANES = 128

LATENT_ROWS = BATCH * SEQ
CTX_ROWS = BATCH * CTX_LEN
ALL_ROWS = LATENT_ROWS + CTX_ROWS

Z_TILE = 512
H_AQ, H_AKV, H_BQ, H_BK, H_BV, H_CU, H_DQ, H_DK, H_DV = range(9)
HEAD_TILES = 9
W_SRC = 256
GATE_COLS = N_BRANCH * D_MODEL
GATE_SRC0 = (12544 - GATE_COLS) // W_SRC
TM_PROJ = 1024
HEAD_SUM_LANES = 256


def _cparams(sem):
    return pltpu.CompilerParams(dimension_semantics=sem, vmem_limit_bytes=V7X_VMEM_LIMIT_BYTES)


def _dot_nt(a, b):
    return lax.dot_general(a, b, (((1,), (1,)), ((), ())), preferred_element_type=F32)


def _dot(a, b):
    return jnp.dot(a, b, preferred_element_type=F32)


def _ctx_blocks(block_rows):
    return LATENT_ROWS // block_rows


def _mod_kernel(c_ref, w_ref, b_ref, o_ref):
    c = c_ref[...]
    s = c * (0.5 * (jnp.tanh(0.5 * c) + 1.0))
    o_ref[0] = _dot(s, w_ref[0]) + b_ref[0]


def _modulation(cc, w_ada, b_ada):
    tn = 1024
    n = 6 * D_MODEL
    return pl.pallas_call(
        _mod_kernel,
        grid=(DEPTH, n // tn),
        in_specs=[
            pl.BlockSpec((8, D_MODEL), lambda l, j: (0, 0)),
            pl.BlockSpec((1, D_MODEL, tn), lambda l, j: (l, 0, j)),
            pl.BlockSpec((1, 1, tn), lambda l, j: (l, 0, j)),
        ],
        out_specs=pl.BlockSpec((1, 8, tn), lambda l, j: (l, 0, j)),
        out_shape=jax.ShapeDtypeStruct((DEPTH, 8, n), F32),
        compiler_params=_cparams(("parallel", "parallel")),
        name="modulation",
    )(cc, w_ada, b_ada.reshape(DEPTH, 1, n))


NORM_ROWS = 16


def _norm_modulate_rows(x_ref, gn_ref, shift_ref, scale_ref, h_ref):
    n = x_ref.shape[0]
    gs = gn_ref[...] * (1.0 + scale_ref[0])
    shift = shift_ref[0]

    def body(c, carry):
        rows = pl.ds(pl.multiple_of(c * NORM_ROWS, NORM_ROWS), NORM_ROWS)
        x = x_ref[rows, :]
        inv = lax.rsqrt(jnp.mean(x * x, axis=-1, keepdims=True) + EPS)
        h_ref[rows, :] = ((x * inv) * gs + shift).astype(BF16)
        return carry

    lax.fori_loop(0, n // NORM_ROWS, body, 0, unroll=8)


def _norm_mod_kernel(x_ref, gn_ref, shift_ref, scale_ref, h_ref):
    _norm_modulate_rows(x_ref, gn_ref, shift_ref, scale_ref, h_ref)


def _norm_mod(x, gn, mod3, *, tm, mod_row):
    m = x.shape[0]
    return pl.pallas_call(
        _norm_mod_kernel,
        grid=(m // tm,),
        in_specs=[
            pl.BlockSpec((tm, D_MODEL), lambda i: (i, 0)),
            pl.BlockSpec((1, D_MODEL), lambda i: (0, 0)),
            pl.BlockSpec((1, 1, D_MODEL), lambda i: (mod_row(i), 0, 0)),
            pl.BlockSpec((1, 1, D_MODEL), lambda i: (mod_row(i), 0, 1)),
        ],
        out_specs=pl.BlockSpec((tm, D_MODEL), lambda i: (i, 0)),
        out_shape=jax.ShapeDtypeStruct((m, D_MODEL), BF16),
        compiler_params=_cparams(("parallel",)),
        name=f"norm_mod_{m}",
    )(x, gn, mod3, mod3)


def _swap16(y):
    lane = lax.broadcasted_iota(jnp.int32, y.shape, 1)
    fwd = pltpu.roll(y, LANES - 16, axis=1)
    bwd = pltpu.roll(y, 16, axis=1)
    return jnp.where((lane & 16) == 0, fwd, bwd)


def _cast_weight_blocks(w_refs, w_scr):
    @pl.when(pl.program_id(1) == 0)
    def _():
        for c, w_ref in enumerate(w_refs):
            w_scr[:, c * W_SRC:(c + 1) * W_SRC] = w_ref[0].astype(BF16)


LATENT_TILES = LATENT_ROWS // TM_PROJ
CAST_CHUNKS = 64


def _row_tile(hl_ref, hc_ref, rows=slice(None)):
    if hc_ref is None:
        return hl_ref[rows, :]
    return jnp.where(pl.program_id(1) >= LATENT_TILES, hc_ref[rows, :], hl_ref[rows, :])


def _h_specs(with_ctx):
    specs = [pl.BlockSpec((TM_PROJ, D_MODEL), lambda n, i: (jnp.minimum(i, LATENT_TILES - 1), 0))]
    if with_ctx:
        specs.append(pl.BlockSpec((TM_PROJ, D_MODEL), lambda n, i: (0, 0), pipeline_mode=pl.Buffered(1)))
    return specs


def _side_cast_specs(arrays, *, layer, row_tiles):
    chunk = lambda n, i: jnp.minimum(n * row_tiles + i, CAST_CHUNKS - 1)
    in_specs, out_specs, out_shape = [], [], []
    for a in arrays:
        rows, cols = a.shape[1] // CAST_CHUNKS, a.shape[2]
        in_specs.append(pl.BlockSpec((1, rows, cols), lambda n, i: (layer, chunk(n, i), 0)))
        out_specs.append(pl.BlockSpec((1, rows, cols), lambda n, i: (0, chunk(n, i), 0)))
        out_shape.append(jax.ShapeDtypeStruct((1,) + a.shape[1:], BF16))
    return in_specs, out_specs, out_shape


def _gate_kernel(*refs, with_ctx):
    hl_ref = refs[0]
    hc_ref = refs[1] if with_ctx else None
    *w_refs, bias_ref, f1_ref, f2_ref, g_ref, f1o_ref, f2o_ref, w_scr = refs[1 + with_ctx:]
    _cast_weight_blocks(w_refs, w_scr)
    f1o_ref[...] = f1_ref[...].astype(BF16)
    f2o_ref[...] = f2_ref[...].astype(BF16)
    t = _dot(_row_tile(hl_ref, hc_ref), w_scr[...]) + bias_ref[...]
    g_ref[...] = (0.5 * (jnp.tanh(0.5 * t) + 1.0)).astype(BF16)


def _gate_proj(hl, hc, w_in, bias, w_ff1, w_ff2, *, layer, tn):
    with_ctx = hc is not None
    row_tiles = LATENT_TILES + with_ctx
    n_src = tn // W_SRC
    w_spec = lambda c: pl.BlockSpec((1, D_MODEL, W_SRC), lambda n, i: (layer, 0, GATE_SRC0 + n * n_src + c))
    cast_in, cast_out, cast_shape = _side_cast_specs((w_ff1, w_ff2), layer=layer, row_tiles=row_tiles)
    h_args = (hl, hc) if with_ctx else (hl,)
    return pl.pallas_call(
        functools.partial(_gate_kernel, with_ctx=with_ctx),
        grid=(GATE_COLS // tn, row_tiles),
        in_specs=_h_specs(with_ctx)
        + [w_spec(c) for c in range(n_src)]
        + [pl.BlockSpec((1, tn), lambda n, i: (0, n))]
        + cast_in,
        out_specs=[pl.BlockSpec((TM_PROJ, tn), lambda n, i: (i, n))] + cast_out,
        out_shape=[jax.ShapeDtypeStruct((row_tiles * TM_PROJ, GATE_COLS), BF16)] + cast_shape,
        scratch_shapes=[pltpu.VMEM((D_MODEL, tn), BF16)],
        compiler_params=_cparams(("arbitrary", "arbitrary")),
        name="gate_proj",
    )(*h_args, *([w_in] * n_src), bias, w_ff1, w_ff2)


def _head_src_block(t, half):
    return jnp.where(t == H_AQ, half, jnp.where(t == H_AKV, 2, 2 * t - 1 + half))


def _head_kernel(hl_ref, hc_ref, wa_ref, wb_ref, gain_ref, cos_ref, sin_ref, bd_ref, s1_ref, s2_ref,
                 z_ref, s1o_ref, s2o_ref, w_scr, *, ctx_queries):
    t = pl.program_id(0)
    _cast_weight_blocks((wa_ref, wb_ref), w_scr)
    s1o_ref[...] = s1_ref[...].astype(BF16)
    s2o_ref[...] = s2_ref[...].astype(BF16)
    is_ctx = pl.program_id(1) >= LATENT_TILES
    used = True if ctx_queries else ~is_ctx

    def project():
        return _dot(_row_tile(hl_ref, hc_ref), w_scr[...])

    def norm_tile(n_norm_chunks, with_rope):
        wide = bd_ref.shape[0]
        half_rows = TM_PROJ // 2
        for r0 in range(0, TM_PROJ, half_rows):
            rows = slice(r0, r0 + half_rows)
            acc = _dot(_row_tile(hl_ref, hc_ref, rows), w_scr[...])
            for c2 in range(Z_TILE // wide):
                a = acc[:, c2 * wide:(c2 + 1) * wide]
                normed = a
                if c2 * (wide // LANES) < n_norm_chunks:
                    ms = _dot((a * a).astype(BF16), bd_ref[...]) * (1.0 / HEAD_DIM)
                    normed = a * lax.rsqrt(ms + EPS) * gain_ref[:, c2 * wide:(c2 + 1) * wide]
                for c in range(wide // LANES):
                    chunk = c2 * (wide // LANES) + c
                    y = (normed if chunk < n_norm_chunks else a)[:, c * LANES:(c + 1) * LANES]
                    if with_rope and chunk < n_norm_chunks:
                        y = y * cos_ref[rows, :] + _swap16(y) * sin_ref[rows, :]
                    z_ref[rows, chunk * LANES:(chunk + 1) * LANES] = y.astype(BF16)

    is_query = (t == H_AQ) | (t == H_BQ) | (t == H_DQ) | (t == H_CU)

    @pl.when(((t == H_AQ) | (t == H_DQ)) & used | (t == H_DK))
    def _():
        norm_tile(4, True)

    @pl.when(t == H_AKV)
    def _():
        norm_tile(1, True)

    @pl.when((t == H_BQ) & used | (t == H_BK))
    def _():
        norm_tile(4, False)

    @pl.when((t == H_BV) | (t == H_CU) & used | (t == H_DV))
    def _():
        z_ref[...] = project().astype(BF16)

    if not ctx_queries:
        @pl.when(is_query & is_ctx)
        def _():
            z_ref[...] = jnp.zeros_like(z_ref)


def _head_proj(hl, hc, w_in, gain, cos_t, sin_t, bd, w_branch, w_out, *, layer, ctx_queries):
    seq_tiles = SEQ // TM_PROJ
    row_tiles = LATENT_TILES + 1
    rope_block = lambda i: jnp.where(i < LATENT_TILES, i % seq_tiles, seq_tiles)
    w_spec = lambda half: pl.BlockSpec((1, D_MODEL, W_SRC), lambda t, i: (layer, 0, _head_src_block(t, half)))
    cast_in, cast_out, cast_shape = _side_cast_specs((w_branch, w_out), layer=layer, row_tiles=row_tiles)
    return pl.pallas_call(
        functools.partial(_head_kernel, ctx_queries=ctx_queries),
        grid=(HEAD_TILES, row_tiles),
        in_specs=_h_specs(True) + [
            w_spec(0), w_spec(1),
            pl.BlockSpec((1, Z_TILE), lambda t, i: (0, t)),
            pl.BlockSpec((TM_PROJ, LANES), lambda t, i: (rope_block(i), 0)),
            pl.BlockSpec((TM_PROJ, LANES), lambda t, i: (rope_block(i), 0)),
            pl.BlockSpec(bd.shape, lambda t, i: (0, 0)),
        ] + cast_in,
        out_specs=[pl.BlockSpec((None, TM_PROJ, Z_TILE), lambda t, i: (t, i, 0))] + cast_out,
        out_shape=[jax.ShapeDtypeStruct((HEAD_TILES, ALL_ROWS, Z_TILE), BF16)] + cast_shape,
        scratch_shapes=[pltpu.VMEM((D_MODEL, Z_TILE), BF16)],
        compiler_params=_cparams(("arbitrary", "arbitrary")),
        name="head_proj",
    )(hl, hc, w_in, w_in, gain, cos_t, sin_t, bd, w_branch, w_out)


def _softmax_attend(q, segs, extra_logit=None):
    scores = []
    for k, _, bias in segs:
        s = _dot_nt(q, k)
        if bias is not None:
            s = s + bias
        scores.append(s)
    m = scores[0].max(axis=-1, keepdims=True)
    for s in scores[1:]:
        m = jnp.maximum(m, s.max(axis=-1, keepdims=True))
    if extra_logit is not None:
        m = jnp.maximum(m, extra_logit)
    l = None
    o = None
    for s, (_, v, _) in zip(scores, segs):
        p = jnp.exp(s - m)
        ls = p.sum(axis=-1, keepdims=True)
        os_ = _dot(p.astype(BF16), v)
        l = ls if l is None else l + ls
        o = os_ if o is None else o + os_
    if extra_logit is not None:
        l = l + jnp.exp(extra_logit - m)
    return o, l


def _lane_group(shape, width):
    return lax.broadcasted_iota(jnp.int32, shape, 1) // width


def _zspec(tile, rows, index_rows, width=Z_TILE):
    return pl.BlockSpec((None, rows, width), lambda *ids: (tile, index_rows(*ids), 0))


def _repeat_kv_heads(kv):
    pieces = [kv[:, i * HEAD_DIM:(i + 1) * HEAD_DIM] for i in range(2 * A_KV_HEADS)]
    k = jnp.concatenate([pieces[g] for g in range(A_KV_HEADS) for _ in range(A_GROUP)], axis=1)
    v = jnp.concatenate([pieces[A_KV_HEADS + g] for g in range(A_KV_HEADS) for _ in range(A_GROUP)], axis=1)
    return k, v


WIN_BLOCKS_PER_STEP = 8


def _win_gqa_kernel(sink_ref, q_ref, *rest, latent):
    if latent:
        kv_ref, ckv_ref, o_ref, k_scr, v_scr, kc_scr, vc_scr = rest
    else:
        ckv_ref, o_ref, kc_scr, vc_scr = rest
    gw = A_GROUP * HEAD_DIM

    def build():
        kc_scr[...], vc_scr[...] = _repeat_kv_heads(ckv_ref[...])
        if latent:
            k_scr[...], v_scr[...] = _repeat_kv_heads(kv_ref[...])

    if latent:
        pl.when(pl.program_id(1) == 0)(build)
    else:
        build()

    tq = A_BLOCK if latent else q_ref.shape[0]
    n_sub = q_ref.shape[0] // tq
    rows = A_GROUP * tq
    row_head = lax.broadcasted_iota(jnp.int32, (rows, 1), 0) // tq
    lane_head = _lane_group((tq, gw), HEAD_DIM)
    for j in range(n_sub):
        q = q_ref[j * tq:(j + 1) * tq, :] * ATTN_SCALE
        if latent:
            n = pl.program_id(1) * n_sub + j
            span = 3 * A_BLOCK
            start = pl.multiple_of(jnp.clip((n - 1) * A_BLOCK, 0, SEQ - span), A_BLOCK)
            qpos = n * A_BLOCK + lax.broadcasted_iota(jnp.int32, (tq, span), 0)
            kpos = start + lax.broadcasted_iota(jnp.int32, (tq, span), 1)
            band = jnp.where(jnp.abs(qpos - kpos) <= A_WINDOW, 0.0, NEG).astype(F32)
            band = jnp.concatenate([band] * A_GROUP, axis=0)
        for g in range(A_KV_HEADS):
            gs = slice(g * gw, (g + 1) * gw)
            qg = q[:, gs]
            zero = jnp.zeros_like(qg)
            q_stack = jnp.concatenate([jnp.where(lane_head == r, qg, zero) for r in range(A_GROUP)], axis=0)
            sink = jnp.zeros((rows, 1), F32)
            for r in range(A_GROUP):
                sink = jnp.where(row_head == r, sink_ref[g * A_GROUP + r], sink)
            segs = [(kc_scr[:, gs], vc_scr[:, gs], None)]
            if latent:
                segs.append((k_scr[pl.ds(start, span), gs], v_scr[pl.ds(start, span), gs], band))
            o, l = _softmax_attend(q_stack, segs, extra_logit=sink)
            o = o / l
            og = jnp.zeros((tq, gw), F32)
            for r in range(A_GROUP):
                og = jnp.where(lane_head == r, o[r * tq:(r + 1) * tq], og)
            o_ref[j * tq:(j + 1) * tq, gs] = og.astype(BF16)


def _win_gqa(zh, sink, *, latent):
    kvb = 2 * A_KV_HEADS * HEAD_DIM
    c0 = _ctx_blocks(CTX_LEN)
    smem = pl.BlockSpec(memory_space=pltpu.SMEM)
    rep = lambda n: pltpu.VMEM((n, A_HEADS * HEAD_DIM), BF16)
    if latent:
        tq = WIN_BLOCKS_PER_STEP * A_BLOCK
        nq = SEQ // tq
        grid = (BATCH, nq)
        in_specs = [
            smem,
            _zspec(H_AQ, tq, lambda b, n: b * nq + n),
            _zspec(H_AKV, SEQ, lambda b, n: b, kvb),
            _zspec(H_AKV, CTX_LEN, lambda b, n: c0 + b, kvb),
        ]
        out_specs = pl.BlockSpec((tq, Z_TILE), lambda b, n: (b * nq + n, 0))
        args = (sink, zh, zh, zh)
        rows = LATENT_ROWS
        sem = ("parallel", "arbitrary")
        scratch = [rep(SEQ), rep(SEQ), rep(CTX_LEN), rep(CTX_LEN)]
    else:
        grid = (BATCH,)
        in_specs = [
            smem,
            _zspec(H_AQ, CTX_LEN, lambda b: c0 + b),
            _zspec(H_AKV, CTX_LEN, lambda b: c0 + b, kvb),
        ]
        out_specs = pl.BlockSpec((CTX_LEN, Z_TILE), lambda b: (b, 0))
        args = (sink, zh, zh)
        rows = CTX_ROWS
        sem = ("parallel",)
        scratch = [rep(CTX_LEN), rep(CTX_LEN)]
    return pl.pallas_call(
        functools.partial(_win_gqa_kernel, latent=latent),
        grid=grid,
        in_specs=in_specs,
        out_specs=out_specs,
        out_shape=jax.ShapeDtypeStruct((rows, BRANCH_W), BF16),
        scratch_shapes=scratch,
        compiler_params=_cparams(sem),
        name="win_gqa" if latent else "win_gqa_ctx",
    )(*args)


N_ROW_OFFSETS = 2 * B_WIN_ROWS - 1
N_COL_OFFSETS = 2 * B_WIN_COLS - 1
N_PAIR = N_ROW_OFFSETS + 1
NBR_QROWS = 4
NBR_SLAB = NBR_QROWS + B_WIN_ROWS
NBR_GROUPS = GRID_H // NBR_QROWS
NBR_CASES = 3


def _nbr_case_geometry(case):
    if case == 0:
        return B_WIN_ROWS - 1, lambda a, c: 0 <= c < B_WIN_ROWS
    if case == 1:
        return B_WIN_ROWS - 1 - NBR_QROWS, lambda a, c: 0 <= c - a < B_WIN_ROWS
    return B_WIN_ROWS - 1 - 2 * NBR_QROWS, lambda a, c: NBR_QROWS <= c < NBR_SLAB


def _rpb_table_kernel(rpb_ref, o_ref, pair_scr):
    h = pl.program_id(0)
    shape = (GRID_W, LANES)
    qcol = lax.broadcasted_iota(jnp.int32, shape, 0)
    lane = lax.broadcasted_iota(jnp.int32, shape, 1)
    kcol = lane & (GRID_W - 1)
    hi = lane >= GRID_W
    cs = jnp.clip(qcol - B_WIN_COLS // 2, 0, GRID_W - B_WIN_COLS)
    in_window = (kcol >= cs) & (kcol < cs + B_WIN_COLS)
    ci = kcol - qcol + (B_WIN_COLS - 1)
    neg = jnp.full(shape, NEG, F32)

    def rpb(d, c):
        return rpb_ref[(h * N_ROW_OFFSETS + d) * N_COL_OFFSETS + c]

    for p in range(N_PAIR):
        acc = jnp.zeros(shape, F32)
        for c in range(N_COL_OFFSETS):
            hit = ci == c
            if p - 1 >= 0:
                acc = jnp.where(hit & ~hi, rpb(p - 1, c), acc)
            if p < N_ROW_OFFSETS:
                acc = jnp.where(hit & hi, rpb(p, c), acc)
        pair_scr[p] = jnp.where(in_window, acc, NEG)

    for case in range(NBR_CASES):
        shift, in_rows = _nbr_case_geometry(case)
        for a in range(NBR_QROWS):
            for cp in range(NBR_SLAB // 2):
                c = 2 * cp
                lo_ok, hi_ok = in_rows(a, c), in_rows(a, c + 1)
                if lo_ok or hi_ok:
                    piece = pair_scr[c - a + shift + 1]
                    if not lo_ok:
                        piece = jnp.where(hi, piece, neg)
                    if not hi_ok:
                        piece = jnp.where(hi, neg, piece)
                else:
                    piece = neg
                o_ref[case, 0, a * GRID_W:(a + 1) * GRID_W, cp * LANES:(cp + 1) * LANES] = piece


def _rpb_table(rpb):
    nq, nk = NBR_QROWS * GRID_W, NBR_SLAB * GRID_W
    return pl.pallas_call(
        _rpb_table_kernel,
        grid=(B_HEADS,),
        in_specs=[pl.BlockSpec(memory_space=pltpu.SMEM)],
        out_specs=pl.BlockSpec((NBR_CASES, 1, nq, nk), lambda h: (0, h, 0, 0)),
        out_shape=jax.ShapeDtypeStruct((NBR_CASES, B_HEADS, nq, nk), F32),
        scratch_shapes=[pltpu.VMEM((N_PAIR, GRID_W, LANES), F32)],
        compiler_params=_cparams(("parallel",)),
        name="rpb_table",
    )(rpb.reshape(-1))


def _pair_attend(q, lane_hi, segs_of_head):
    zero = jnp.zeros_like(q)
    halves = []
    for hh in range(2):
        qm = jnp.where(lane_hi, q, zero) if hh else jnp.where(lane_hi, zero, q)
        o, l = _softmax_attend(qm, segs_of_head(hh))
        halves.append(o / l)
    return jnp.where(lane_hi, halves[1], halves[0])


NBR_GROUPS_PER_STEP = 2


def _nbr_kernel(q_ref, k_ref, v_ref, kc_ref, vc_ref, bias_ref, o_ref):
    nq = NBR_QROWS * GRID_W
    nk = NBR_SLAB * GRID_W
    lane_hi = lax.broadcasted_iota(jnp.int32, (nq, LANES), 1) >= HEAD_DIM
    for j in range(NBR_GROUPS_PER_STEP):
        g = pl.program_id(1) * NBR_GROUPS_PER_STEP + j
        case = jnp.where(g == 0, 0, jnp.where(g == NBR_GROUPS - 1, 2, 1))
        r0 = jnp.clip(NBR_QROWS * g - B_WIN_ROWS // 2, 0, GRID_H - NBR_SLAB)
        base = pl.multiple_of(r0 * GRID_W, GRID_W)
        rows = slice(j * nq, (j + 1) * nq)
        for pr in range(B_HEADS // 2):
            sl = slice(pr * LANES, (pr + 1) * LANES)
            kp = k_ref[pl.ds(base, nk), sl]
            vp = v_ref[pl.ds(base, nk), sl]
            kcp = kc_ref[:, sl]
            vcp = vc_ref[:, sl]
            segs = lambda hh: [(kp, vp, bias_ref[case, 2 * pr + hh]), (kcp, vcp, None)]
            q = q_ref[rows, sl] * ATTN_SCALE
            o_ref[rows, sl] = _pair_attend(q, lane_hi, segs).astype(BF16)


def _nbr_attn(zh, table):
    nq = NBR_GROUPS_PER_STEP * NBR_QROWS * GRID_W
    steps = NBR_GROUPS // NBR_GROUPS_PER_STEP
    c0 = _ctx_blocks(CTX_LEN)
    return pl.pallas_call(
        _nbr_kernel,
        grid=(BATCH, steps),
        in_specs=[
            _zspec(H_BQ, nq, lambda b, s: b * steps + s),
            _zspec(H_BK, SEQ, lambda b, s: b),
            _zspec(H_BV, SEQ, lambda b, s: b),
            _zspec(H_BK, CTX_LEN, lambda b, s: c0 + b),
            _zspec(H_BV, CTX_LEN, lambda b, s: c0 + b),
            pl.BlockSpec(table.shape, lambda b, s: (0, 0, 0, 0), pipeline_mode=pl.Buffered(1)),
        ],
        out_specs=pl.BlockSpec((nq, Z_TILE), lambda b, s: (b * steps + s, 0)),
        out_shape=jax.ShapeDtypeStruct((LATENT_ROWS, BRANCH_W), BF16),
        compiler_params=_cparams(("parallel", "arbitrary")),
        name="nbr_attn",
    )(zh, zh, zh, zh, zh, table)


def _ctx_mha_kernel(q_ref, k_ref, v_ref, o_ref):
    q = q_ref[...] * ATTN_SCALE
    lane_hi = lax.broadcasted_iota(jnp.int32, (q.shape[0], LANES), 1) >= HEAD_DIM
    for pr in range(B_HEADS // 2):
        sl = slice(pr * LANES, (pr + 1) * LANES)
        segs = lambda hh: [(k_ref[:, sl], v_ref[:, sl], None)]
        o_ref[:, sl] = _pair_attend(q[:, sl], lane_hi, segs).astype(BF16)


def _ctx_mha(zh):
    c0 = _ctx_blocks(CTX_LEN)
    rows = lambda b: c0 + b
    return pl.pallas_call(
        _ctx_mha_kernel,
        grid=(BATCH,),
        in_specs=[_zspec(H_BQ, CTX_LEN, rows), _zspec(H_BK, CTX_LEN, rows), _zspec(H_BV, CTX_LEN, rows)],
        out_specs=pl.BlockSpec((CTX_LEN, Z_TILE), lambda b: (b, 0)),
        out_shape=jax.ShapeDtypeStruct((CTX_ROWS, BRANCH_W), BF16),
        compiler_params=_cparams(("parallel",)),
        name="nbr_attn_ctx",
    )(zh, zh, zh)


def _pool_kernel(u_ref, w_ref, scale_ref, o_ref):
    n = u_ref.shape[0]
    t = lax.broadcasted_iota(jnp.int32, (n, C_GROUP_DIM), 0)

    def down(a, k):
        return jnp.where(t >= k, pltpu.roll(a, k, axis=0), 0.0)

    def up(a, k):
        return jnp.where(t < n - k, pltpu.roll(a, n - k, axis=0), 0.0)

    for g, w in enumerate(C_WINDOWS):
        sl = slice(g * C_GROUP_DIM, (g + 1) * C_GROUP_DIM)
        u = u_ref[:, sl].astype(F32)
        half = w // 2
        back = u
        fwd = u
        k = 1
        while k < half:
            back = back + down(back, k)
            fwd = fwd + up(fwd, k)
            k *= 2
        total = down(back, 1) + fwd
        lo = jnp.maximum(t - half, 0)
        hi = jnp.minimum(t + half - 1, n - 1)
        cnt = (hi - lo + 1).astype(F32)
        pooled = total / cnt - u
        y = _dot(pooled.astype(BF16), w_ref[0, g]) * scale_ref[:, sl]
        o_ref[:, sl] = y.astype(BF16)


def _pool(zh, c_w, c_scale, *, layer, n, latent):
    rows = LATENT_ROWS if latent else CTX_ROWS
    b0 = 0 if latent else _ctx_blocks(n)
    return pl.pallas_call(
        _pool_kernel,
        grid=(rows // n,),
        in_specs=[
            _zspec(H_CU, n, lambda b: b0 + b),
            pl.BlockSpec((1, len(C_WINDOWS), C_GROUP_DIM, C_GROUP_DIM), lambda b: (layer, 0, 0, 0)),
            pl.BlockSpec((1, BRANCH_W), lambda b: (0, 0)),
        ],
        out_specs=pl.BlockSpec((n, BRANCH_W), lambda b: (b, 0)),
        out_shape=jax.ShapeDtypeStruct((rows, BRANCH_W), BF16),
        compiler_params=_cparams(("parallel",)),
        name=f"pool_{n}",
    )(zh, c_w, c_scale)


DIFF_ROWS = 256


def _diff_kernel(lam_ref, q_ref, *rest, lam_init, latent):
    if latent:
        k_ref, v_ref, kc_ref, vc_ref, g_ref, o_ref = rest
    else:
        kc_ref, vc_ref, g_ref, o_ref = rest
    dl = lam_ref[...]
    lam = (jnp.exp(jnp.sum(dl[0:1] * dl[1:2], axis=-1, keepdims=True))
           - jnp.exp(jnp.sum(dl[2:3] * dl[3:4], axis=-1, keepdims=True)) + lam_init)
    hw = 2 * HEAD_DIM
    tq = min(q_ref.shape[0], DIFF_ROWS)
    lane_hi = lax.broadcasted_iota(jnp.int32, (tq, hw), 1) >= HEAD_DIM
    for r0 in range(0, q_ref.shape[0], tq):
        rows = slice(r0, r0 + tq)
        for h in range(D_HEADS):
            sl = slice(h * hw, (h + 1) * hw)
            qh = q_ref[rows, sl] * ATTN_SCALE
            zero = jnp.zeros_like(qh)
            q_stack = jnp.concatenate([jnp.where(lane_hi, zero, qh), jnp.where(lane_hi, qh, zero)], axis=0)
            segs = [(kc_ref[:, sl], vc_ref[:, sl], None)]
            if latent:
                segs.append((k_ref[:, sl], v_ref[:, sl], None))
            o, l = _softmax_attend(q_stack, segs)
            o = o / l
            od = o[:tq] - lam * o[tq:]
            ms = jnp.mean(od * od, axis=-1, keepdims=True)
            y = od * lax.rsqrt(ms + EPS) * g_ref[...] * (1.0 - lam_init)
            o_ref[rows, sl] = y.astype(BF16)


def _diff_attn(zh, d_lambda, subln, *, lam_init, latent):
    full = lambda shape: pl.BlockSpec(shape, lambda *_: (0,) * len(shape))
    c0 = _ctx_blocks(CTX_LEN)
    if latent:
        tq = 2 * DIFF_ROWS
        nq = SEQ // tq
        grid = (BATCH, nq)
        in_specs = [
            full((4, HEAD_DIM)),
            _zspec(H_DQ, tq, lambda b, n: b * nq + n),
            _zspec(H_DK, SEQ, lambda b, n: b),
            _zspec(H_DV, SEQ, lambda b, n: b),
            _zspec(H_DK, CTX_LEN, lambda b, n: c0 + b),
            _zspec(H_DV, CTX_LEN, lambda b, n: c0 + b),
            full((1, 2 * HEAD_DIM)),
        ]
        out_specs = pl.BlockSpec((tq, Z_TILE), lambda b, n: (b * nq + n, 0))
        args = (d_lambda, zh, zh, zh, zh, zh, subln)
        rows = LATENT_ROWS
        sem = ("parallel", "arbitrary")
    else:
        grid = (BATCH,)
        crow = lambda b: c0 + b
        in_specs = [
            full((4, HEAD_DIM)),
            _zspec(H_DQ, CTX_LEN, crow),
            _zspec(H_DK, CTX_LEN, crow),
            _zspec(H_DV, CTX_LEN, crow),
            full((1, 2 * HEAD_DIM)),
        ]
        out_specs = pl.BlockSpec((CTX_LEN, Z_TILE), lambda b: (b, 0))
        args = (d_lambda, zh, zh, zh, subln)
        rows = CTX_ROWS
        sem = ("parallel",)
    return pl.pallas_call(
        functools.partial(_diff_kernel, lam_init=lam_init, latent=latent),
        grid=grid,
        in_specs=in_specs,
        out_specs=out_specs,
        out_shape=jax.ShapeDtypeStruct((rows, BRANCH_W), BF16),
        compiler_params=_cparams(sem),
        name="diff_attn" if latent else "diff_attn_ctx",
    )(*args)


def _merge_kernel(*refs):
    y_refs = refs[:N_BRANCH]
    g_refs = refs[N_BRANCH:2 * N_BRANCH]
    wb_ref, wo_ref, x_ref, gt_ref, gn_ref, shift_ref, scale_ref, xo_ref, h_ref = refs[2 * N_BRANCH:]
    mixed = None
    for k in range(N_BRANCH):
        proj = _dot(y_refs[k][...], wb_ref[0, k]) * g_refs[k][...].astype(F32)
        mixed = proj if mixed is None else mixed + proj
    out = _dot(mixed.astype(BF16), wo_ref[0])
    xo_ref[...] = x_ref[...] + gt_ref[0] * out
    _norm_modulate_rows(xo_ref, gn_ref, shift_ref, scale_ref, h_ref)


def _merge(ys, gates, w_branch, w_out, x, gn2, mod3, *, tm, row0, mod_row):
    m = x.shape[0]
    row = lambda i: (i, 0)
    g0 = row0 // tm
    mod = lambda chunk: pl.BlockSpec((1, 1, D_MODEL), lambda i: (mod_row(i), 0, chunk))
    resident = pl.Buffered(1)
    return pl.pallas_call(
        _merge_kernel,
        grid=(m // tm,),
        in_specs=[pl.BlockSpec((tm, BRANCH_W), row)] * N_BRANCH
        + [pl.BlockSpec((tm, D_MODEL), lambda i, k=k: (g0 + i, k)) for k in range(N_BRANCH)]
        + [
            pl.BlockSpec((1, N_BRANCH, BRANCH_W, D_MODEL), lambda i: (0, 0, 0, 0), pipeline_mode=resident),
            pl.BlockSpec((1, D_MODEL, D_MODEL), lambda i: (0, 0, 0), pipeline_mode=resident),
            pl.BlockSpec((tm, D_MODEL), row),
            mod(2),
            pl.BlockSpec((1, D_MODEL), lambda i: (0, 0)),
            mod(3),
            mod(4),
        ],
        out_specs=[pl.BlockSpec((tm, D_MODEL), row), pl.BlockSpec((tm, D_MODEL), row)],
        out_shape=[jax.ShapeDtypeStruct((m, D_MODEL), F32), jax.ShapeDtypeStruct((m, D_MODEL), BF16)],
        compiler_params=_cparams(("parallel",)),
        name=f"merge_{m}",
    )(*ys, *([gates] * N_BRANCH), w_branch, w_out, x, mod3, gn2, mod3, mod3)


def _mlp_kernel(h_ref, x_ref, gt_ref, w1_ref, w2_ref, *rest, emit_next):
    if emit_next:
        gn_ref, shift_ref, scale_ref, xo_ref, hn_ref = rest
    else:
        (xo_ref,) = rest
    f = pl.program_id(1)

    @pl.when(f == 0)
    def _():
        xo_ref[...] = jnp.zeros_like(xo_ref)

    a = jnp.maximum(_dot(h_ref[...], w1_ref[0]), 0.0)
    xo_ref[...] += _dot((a * a).astype(BF16), w2_ref[0])

    @pl.when(f == pl.num_programs(1) - 1)
    def _():
        xo_ref[...] = x_ref[...] + gt_ref[0] * xo_ref[...]
        if emit_next:
            _norm_modulate_rows(xo_ref, gn_ref, shift_ref, scale_ref, hn_ref)


def _mlp(h, x, mod3, w1, w2, next_norm, *, tm, tf, mod_row):
    m = x.shape[0]
    emit_next = next_norm is not None
    row = lambda i, f: (i, 0)
    in_specs = [
        pl.BlockSpec((tm, D_MODEL), row),
        pl.BlockSpec((tm, D_MODEL), row),
        pl.BlockSpec((1, 1, D_MODEL), lambda i, f: (mod_row(i), 0, 5)),
        pl.BlockSpec((1, D_MODEL, tf), lambda i, f: (0, 0, f)),
        pl.BlockSpec((1, tf, D_MODEL), lambda i, f: (0, f, 0)),
    ]
    args = [h, x, mod3, w1, w2]
    out_specs = [pl.BlockSpec((tm, D_MODEL), row)]
    out_shape = [jax.ShapeDtypeStruct((m, D_MODEL), F32)]
    if emit_next:
        gn_next, mod3_next = next_norm
        in_specs += [
            pl.BlockSpec((1, D_MODEL), lambda i, f: (0, 0)),
            pl.BlockSpec((1, 1, D_MODEL), lambda i, f: (mod_row(i), 0, 0)),
            pl.BlockSpec((1, 1, D_MODEL), lambda i, f: (mod_row(i), 0, 1)),
        ]
        args += [gn_next, mod3_next, mod3_next]
        out_specs.append(pl.BlockSpec((tm, D_MODEL), row))
        out_shape.append(jax.ShapeDtypeStruct((m, D_MODEL), BF16))
    outs = pl.pallas_call(
        functools.partial(_mlp_kernel, emit_next=emit_next),
        grid=(m // tm, D_FF // tf),
        in_specs=in_specs,
        out_specs=out_specs,
        out_shape=out_shape,
        compiler_params=_cparams(("parallel", "arbitrary")),
        name=f"mlp_{m}",
    )(*args)
    return outs if emit_next else (outs[0], None)


def _rope_tables():
    t = jnp.arange(SEQ)
    row = (t // GRID_W).astype(F32)
    col = (t % GRID_W).astype(F32)
    n_freq = HEAD_DIM // 4
    inv = ROPE_BASE ** (-jnp.arange(n_freq, dtype=F32) / n_freq)
    ar = row[:, None] * inv
    ac = col[:, None] * inv
    cos_h = jnp.concatenate([jnp.cos(ar), jnp.cos(ar), jnp.cos(ac), jnp.cos(ac)], axis=-1)
    sin_h = jnp.concatenate([-jnp.sin(ar), jnp.sin(ar), -jnp.sin(ac), jnp.sin(ac)], axis=-1)
    reps = LANES // HEAD_DIM
    cos_t = jnp.concatenate([jnp.tile(cos_h, (1, reps)), jnp.ones((TM_PROJ, LANES), F32)], axis=0)
    sin_t = jnp.concatenate([jnp.tile(sin_h, (1, reps)), jnp.zeros((TM_PROJ, LANES), F32)], axis=0)
    return cos_t, sin_t


def _head_gain_row(qk_gain_l):
    ones = jnp.ones((Z_TILE,), F32)
    rep = lambda g: jnp.tile(g, Z_TILE // HEAD_DIM)
    akv = jnp.concatenate([jnp.tile(qk_gain_l[0, 1], A_KV_HEADS), jnp.ones((Z_TILE - A_KV_HEADS * HEAD_DIM,), F32)])
    return jnp.concatenate([
        rep(qk_gain_l[0, 0]), akv,
        rep(qk_gain_l[1, 0]), rep(qk_gain_l[1, 1]), ones, ones,
        rep(qk_gain_l[2, 0]), rep(qk_gain_l[2, 1]), ones,
    ])[None, :]


def kernel(x, c, ctx, c_ctx, w_ada, b_ada, g_norm1, g_norm2, w_in, b_gate, qk_gain, a_sink, b_rpb,
           c_w, c_scale, d_lambda, d_subln, w_branch, w_out, w_ff1, w_ff2):
    cos_t, sin_t = _rope_tables()
    lane = jnp.arange(HEAD_SUM_LANES)
    bd = (lane[:, None] // HEAD_DIM == lane[None, :] // HEAD_DIM).astype(BF16)

    cc = jnp.concatenate([c, c_ctx[None, :], jnp.zeros((8 - BATCH - 1, D_MODEL), F32)], axis=0)
    mod_all = _modulation(cc, w_ada, b_ada)

    cw = c_w.astype(BF16)
    w_branch2d = w_branch.reshape(DEPTH, N_BRANCH * BRANCH_W, D_MODEL)

    xl = x.reshape(LATENT_ROWS, D_MODEL)
    xc = ctx.reshape(CTX_ROWS, D_MODEL)
    ctx_row = lambda i: BATCH
    batch_row = lambda tm: (lambda i: i // (SEQ // tm))
    mods = [mod_all[l].reshape(8, 1, 6 * D_MODEL) for l in range(DEPTH)]
    gn1 = [g_norm1[l][None, :] for l in range(DEPTH)]

    tm_merge, tm_mlp, tf_mlp, tn_gate = 256, 512, 1024, 1024
    hl = _norm_mod(xl, gn1[0], mods[0], tm=tm_mlp, mod_row=batch_row(tm_mlp))
    hc = _norm_mod(xc, gn1[0], mods[0], tm=tm_mlp, mod_row=ctx_row)

    for l in range(DEPTH):
        last = l == DEPTH - 1
        lam_init = 0.8 - 0.6 * math.exp(-0.3 * l)
        mod3 = mods[l]
        gn2 = g_norm2[l][None, :]
        cs = c_scale[l][None, :]
        subln = d_subln[l][None, :]
        next_norm = None if last else (gn1[l + 1], mods[l + 1])

        gates, w1, w2 = _gate_proj(hl, None if last else hc, w_in, b_gate[l][None, :], w_ff1, w_ff2,
                                   layer=l, tn=tn_gate)
        zh, wb, wo = _head_proj(hl, hc, w_in, _head_gain_row(qk_gain[l]), cos_t, sin_t, bd, w_branch2d, w_out,
                                layer=l, ctx_queries=not last)
        wb = wb.reshape(1, N_BRANCH, BRANCH_W, D_MODEL)

        table = _rpb_table(b_rpb[l])
        ya = _win_gqa(zh, a_sink[l], latent=True)
        yb = _nbr_attn(zh, table)
        yc = _pool(zh, cw, cs, layer=l, n=SEQ, latent=True)
        yd = _diff_attn(zh, d_lambda[l], subln, lam_init=lam_init, latent=True)
        xl, h2 = _merge((ya, yb, yc, yd), gates, wb, wo, xl, gn2, mod3,
                        tm=tm_merge, row0=0, mod_row=batch_row(tm_merge))
        xl, hl = _mlp(h2, xl, mod3, w1, w2, next_norm, tm=tm_mlp, tf=tf_mlp, mod_row=batch_row(tm_mlp))
        if not last:
            yac = _win_gqa(zh, a_sink[l], latent=False)
            ybc = _ctx_mha(zh)
            ycc = _pool(zh, cw, cs, layer=l, n=CTX_LEN, latent=False)
            ydc = _diff_attn(zh, d_lambda[l], subln, lam_init=lam_init, latent=False)
            xc, h2c = _merge((yac, ybc, ycc, ydc), gates, wb, wo, xc, gn2, mod3,
                             tm=tm_merge, row0=LATENT_ROWS, mod_row=ctx_row)
            xc, hc = _mlp(h2c, xc, mod3, w1, w2, next_norm, tm=tm_mlp, tf=tf_mlp, mod_row=ctx_row)

    return xl.reshape(BATCH, SEQ, D_MODEL)
```

```python
import functools
import math

import jax
import jax.numpy as jnp
from jax import lax
from jax.experimental import pallas as pl
from jax.experimental.pallas import tpu as pltpu

D_MODEL = 2048
BATCH = 4
SEQ = 2048
DEPTH = 2
GRID_W = 64
GRID_H = SEQ // GRID_W
CTX_LEN = 256
HEAD_DIM = 64
N_BRANCH = 4
BRANCH_W = D_MODEL // N_BRANCH
A_HEADS = 8
A_KV_HEADS = 2
A_GROUP = A_HEADS // A_KV_HEADS
A_WINDOW = 128
A_BLOCK = 128
B_HEADS = 8
B_WIN_ROWS = 8
B_WIN_COLS = 16
C_WINDOWS = (2, 4, 8, 16)
C_GROUP_DIM = 128
D_HEADS = 4
D_FF = 4 * D_MODEL
ROPE_BASE = 10000.0
EPS = 1e-6
NEG = -1e30
ATTN_SCALE = HEAD_DIM ** -0.5

F32 = jnp.float32
BF16 = jnp.bfloat16

V7X_VMEM_LIMIT_BYTES = 56 * 1024 * 1024
LANES = 128

LATENT_ROWS = BATCH * SEQ
CTX_ROWS = BATCH * CTX_LEN
ALL_ROWS = LATENT_ROWS + CTX_ROWS

Z_TILE = 512
H_AQ, H_AKV, H_BQ, H_BK, H_BV, H_CU, H_DQ, H_DK, H_DV = range(9)
HEAD_TILES = 9
W_SRC = 256
GATE_COLS = N_BRANCH * D_MODEL
GATE_SRC0 = (12544 - GATE_COLS) // W_SRC
HEAD_SUM_LANES = 256

TM_PROJ = 1024
TN_GATE = 1024
TM_MERGE = 256
TM_MLP = 512
TF_MLP = 1024


def _cparams(sem):
    return pltpu.CompilerParams(dimension_semantics=sem, vmem_limit_bytes=V7X_VMEM_LIMIT_BYTES)


def _dot_nt(a, b):
    return lax.dot_general(a, b, (((1,), (1,)), ((), ())), preferred_element_type=F32)


def _dot(a, b):
    return jnp.dot(a, b, preferred_element_type=F32)


def _ctx_blocks(block_rows):
    return LATENT_ROWS // block_rows


def _mod_kernel(c_ref, w_ref, b_ref, o_ref):
    c = c_ref[...]
    s = c * (0.5 * (jnp.tanh(0.5 * c) + 1.0))
    o_ref[0] = _dot(s, w_ref[0]) + b_ref[0]


def _modulation(cc, w_ada, b_ada):
    tn = 1024
    n = 6 * D_MODEL
    return pl.pallas_call(
        _mod_kernel,
        grid=(DEPTH, n // tn),
        in_specs=[
            pl.BlockSpec((8, D_MODEL), lambda l, j: (0, 0)),
            pl.BlockSpec((1, D_MODEL, tn), lambda l, j: (l, 0, j)),
            pl.BlockSpec((1, 1, tn), lambda l, j: (l, 0, j)),
        ],
        out_specs=pl.BlockSpec((1, 8, tn), lambda l, j: (l, 0, j)),
        out_shape=jax.ShapeDtypeStruct((DEPTH, 8, n), F32),
        compiler_params=_cparams(("parallel", "parallel")),
        name="modulation",
    )(cc, w_ada, b_ada.reshape(DEPTH, 1, n))


NORM_ROWS = 16


def _norm_modulate_rows(x_ref, gn_ref, shift_ref, scale_ref, h_ref):
    n = x_ref.shape[0]
    gs = gn_ref[...] * (1.0 + scale_ref[0])
    shift = shift_ref[0]

    def body(c, carry):
        rows = pl.ds(pl.multiple_of(c * NORM_ROWS, NORM_ROWS), NORM_ROWS)
        x = x_ref[rows, :]
        inv = lax.rsqrt(jnp.mean(x * x, axis=-1, keepdims=True) + EPS)
        h_ref[rows, :] = ((x * inv) * gs + shift).astype(BF16)
        return carry

    lax.fori_loop(0, n // NORM_ROWS, body, 0, unroll=8)


def _norm_mod_kernel(x_ref, gn_ref, shift_ref, scale_ref, h_ref):
    _norm_modulate_rows(x_ref, gn_ref, shift_ref, scale_ref, h_ref)


def _norm_mod(x, gn, mod3, *, tm, mod_row):
    m = x.shape[0]
    return pl.pallas_call(
        _norm_mod_kernel,
        grid=(m // tm,),
        in_specs=[
            pl.BlockSpec((tm, D_MODEL), lambda i: (i, 0)),
            pl.BlockSpec((1, D_MODEL), lambda i: (0, 0)),
            pl.BlockSpec((1, 1, D_MODEL), lambda i: (mod_row(i), 0, 0)),
            pl.BlockSpec((1, 1, D_MODEL), lambda i: (mod_row(i), 0, 1)),
        ],
        out_specs=pl.BlockSpec((tm, D_MODEL), lambda i: (i, 0)),
        out_shape=jax.ShapeDtypeStruct((m, D_MODEL), BF16),
        compiler_params=_cparams(("parallel",)),
        name=f"norm_mod_{m}",
    )(x, gn, mod3, mod3)


def _swap16(y):
    lane = lax.broadcasted_iota(jnp.int32, y.shape, 1)
    fwd = pltpu.roll(y, LANES - 16, axis=1)
    bwd = pltpu.roll(y, 16, axis=1)
    return jnp.where((lane & 16) == 0, fwd, bwd)


def _cast_weight_blocks(w_refs, w_scr):
    @pl.when(pl.program_id(1) == 0)
    def _():
        for c, w_ref in enumerate(w_refs):
            w_scr[:, c * W_SRC:(c + 1) * W_SRC] = w_ref[0].astype(BF16)


LATENT_TILES = LATENT_ROWS // TM_PROJ
CAST_CHUNKS = 64


def _row_tile(hl_ref, hc_ref, rows=slice(None)):
    if hc_ref is None:
        return hl_ref[rows, :]
    return jnp.where(pl.program_id(1) >= LATENT_TILES, hc_ref[rows, :], hl_ref[rows, :])


def _h_specs(with_ctx):
    specs = [pl.BlockSpec((TM_PROJ, D_MODEL), lambda n, i: (jnp.minimum(i, LATENT_TILES - 1), 0))]
    if with_ctx:
        specs.append(pl.BlockSpec((TM_PROJ, D_MODEL), lambda n, i: (0, 0), pipeline_mode=pl.Buffered(1)))
    return specs


def _side_cast_specs(arrays, *, layer, row_tiles):
    chunk = lambda n, i: jnp.minimum(n * row_tiles + i, CAST_CHUNKS - 1)
    in_specs, out_specs, out_shape = [], [], []
    for a in arrays:
        rows, cols = a.shape[1] // CAST_CHUNKS, a.shape[2]
        in_specs.append(pl.BlockSpec((1, rows, cols), lambda n, i: (layer, chunk(n, i), 0)))
        out_specs.append(pl.BlockSpec((1, rows, cols), lambda n, i: (0, chunk(n, i), 0)))
        out_shape.append(jax.ShapeDtypeStruct((1,) + a.shape[1:], BF16))
    return in_specs, out_specs, out_shape


def _gate_kernel(*refs, with_ctx):
    hl_ref = refs[0]
    hc_ref = refs[1] if with_ctx else None
    *w_refs, bias_ref, f1_ref, f2_ref, g_ref, f1o_ref, f2o_ref, w_scr = refs[1 + with_ctx:]
    _cast_weight_blocks(w_refs, w_scr)
    f1o_ref[...] = f1_ref[...].astype(BF16)
    f2o_ref[...] = f2_ref[...].astype(BF16)
    half_rows = TM_PROJ // 2
    for r0 in range(0, TM_PROJ, half_rows):
        rows = slice(r0, r0 + half_rows)
        t = _dot(_row_tile(hl_ref, hc_ref, rows), w_scr[...]) + bias_ref[...]
        g_ref[rows, :] = (0.5 * (jnp.tanh(0.5 * t) + 1.0)).astype(BF16)


def _gate_proj(hl, hc, w_in, bias, w_ff1, w_ff2, *, layer, tn):
    with_ctx = hc is not None
    row_tiles = LATENT_TILES + with_ctx
    n_src = tn // W_SRC
    w_spec = lambda c: pl.BlockSpec((1, D_MODEL, W_SRC), lambda n, i: (layer, 0, GATE_SRC0 + n * n_src + c))
    cast_in, cast_out, cast_shape = _side_cast_specs((w_ff1, w_ff2), layer=layer, row_tiles=row_tiles)
    h_args = (hl, hc) if with_ctx else (hl,)
    return pl.pallas_call(
        functools.partial(_gate_kernel, with_ctx=with_ctx),
        grid=(GATE_COLS // tn, row_tiles),
        in_specs=_h_specs(with_ctx)
        + [w_spec(c) for c in range(n_src)]
        + [pl.BlockSpec((1, tn), lambda n, i: (0, n))]
        + cast_in,
        out_specs=[pl.BlockSpec((TM_PROJ, tn), lambda n, i: (i, n))] + cast_out,
        out_shape=[jax.ShapeDtypeStruct((row_tiles * TM_PROJ, GATE_COLS), BF16)] + cast_shape,
        scratch_shapes=[pltpu.VMEM((D_MODEL, tn), BF16)],
        compiler_params=_cparams(("arbitrary", "arbitrary")),
        name="gate_proj",
    )(*h_args, *([w_in] * n_src), bias, w_ff1, w_ff2)


def _head_src_block(t, half):
    return jnp.where(t == H_AQ, half, jnp.where(t == H_AKV, 2, 2 * t - 1 + half))


def _head_kernel(hl_ref, hc_ref, wa_ref, wb_ref, gain_ref, cos_ref, sin_ref, bd_ref, s1_ref, s2_ref,
                 z_ref, s1o_ref, s2o_ref, w_scr, *, ctx_queries):
    t = pl.program_id(0)
    _cast_weight_blocks((wa_ref, wb_ref), w_scr)
    s1o_ref[...] = s1_ref[...].astype(BF16)
    s2o_ref[...] = s2_ref[...].astype(BF16)
    is_ctx = pl.program_id(1) >= LATENT_TILES
    used = True if ctx_queries else ~is_ctx

    def norm_tile(n_norm_chunks, with_rope):
        wide = bd_ref.shape[0]
        half_rows = TM_PROJ // 2
        for r0 in range(0, TM_PROJ, half_rows):
            rows = slice(r0, r0 + half_rows)
            acc = _dot(_row_tile(hl_ref, hc_ref, rows), w_scr[...])
            for c2 in range(Z_TILE // wide):
                a = acc[:, c2 * wide:(c2 + 1) * wide]
                normed = a
                if c2 * (wide // LANES) < n_norm_chunks:
                    ms = _dot((a * a).astype(BF16), bd_ref[...]) * (1.0 / HEAD_DIM)
                    normed = a * lax.rsqrt(ms + EPS) * gain_ref[:, c2 * wide:(c2 + 1) * wide]
                for c in range(wide // LANES):
                    chunk = c2 * (wide // LANES) + c
                    y = (normed if chunk < n_norm_chunks else a)[:, c * LANES:(c + 1) * LANES]
                    if with_rope and chunk < n_norm_chunks:
                        y = y * cos_ref[rows, :] + _swap16(y) * sin_ref[rows, :]
                    z_ref[rows, chunk * LANES:(chunk + 1) * LANES] = y.astype(BF16)

    is_query = (t == H_AQ) | (t == H_BQ) | (t == H_DQ) | (t == H_CU)

    @pl.when(((t == H_AQ) | (t == H_DQ)) & used | (t == H_DK))
    def _():
        norm_tile(4, True)

    @pl.when(t == H_AKV)
    def _():
        norm_tile(1, True)

    @pl.when((t == H_BQ) & used | (t == H_BK))
    def _():
        norm_tile(4, False)

    @pl.when((t == H_BV) | (t == H_CU) & used | (t == H_DV))
    def _():
        norm_tile(0, False)

    if not ctx_queries:
        @pl.when(is_query & is_ctx)
        def _():
            z_ref[...] = jnp.zeros_like(z_ref)


def _head_proj(hl, hc, w_in, gain, cos_t, sin_t, bd, w_branch, w_out, *, layer, ctx_queries):
    seq_tiles = SEQ // TM_PROJ
    row_tiles = LATENT_TILES + 1
    rope_block = lambda i: jnp.where(i < LATENT_TILES, i % seq_tiles, seq_tiles)
    w_spec = lambda half: pl.BlockSpec((1, D_MODEL, W_SRC), lambda t, i: (layer, 0, _head_src_block(t, half)))
    cast_in, cast_out, cast_shape = _side_cast_specs((w_branch, w_out), layer=layer, row_tiles=row_tiles)
    return pl.pallas_call(
        functools.partial(_head_kernel, ctx_queries=ctx_queries),
        grid=(HEAD_TILES, row_tiles),
        in_specs=_h_specs(True) + [
            w_spec(0), w_spec(1),
            pl.BlockSpec((1, Z_TILE), lambda t, i: (0, t)),
            pl.BlockSpec((TM_PROJ, LANES), lambda t, i: (rope_block(i), 0)),
            pl.BlockSpec((TM_PROJ, LANES), lambda t, i: (rope_block(i), 0)),
            pl.BlockSpec(bd.shape, lambda t, i: (0, 0)),
        ] + cast_in,
        out_specs=[pl.BlockSpec((None, TM_PROJ, Z_TILE), lambda t, i: (t, i, 0))] + cast_out,
        out_shape=[jax.ShapeDtypeStruct((HEAD_TILES, ALL_ROWS, Z_TILE), BF16)] + cast_shape,
        scratch_shapes=[pltpu.VMEM((D_MODEL, Z_TILE), BF16)],
        compiler_params=_cparams(("arbitrary", "arbitrary")),
        name="head_proj",
    )(hl, hc, w_in, w_in, gain, cos_t, sin_t, bd, w_branch, w_out)


def _softmax_attend(q, segs, extra_logit=None):
    scores = []
    for k, _, bias in segs:
        s = _dot_nt(q, k)
        if bias is not None:
            s = s + bias
        scores.append(s)
    m = scores[0].max(axis=-1, keepdims=True)
    for s in scores[1:]:
        m = jnp.maximum(m, s.max(axis=-1, keepdims=True))
    if extra_logit is not None:
        m = jnp.maximum(m, extra_logit)
    l = None
    o = None
    for s, (_, v, _) in zip(scores, segs):
        p = jnp.exp(s - m)
        ls = p.sum(axis=-1, keepdims=True)
        os_ = _dot(p.astype(BF16), v)
        l = ls if l is None else l + ls
        o = os_ if o is None else o + os_
    if extra_logit is not None:
        l = l + jnp.exp(extra_logit - m)
    return o, l


def _lane_group(shape, width):
    return lax.broadcasted_iota(jnp.int32, shape, 1) // width


def _zspec(tile, rows, index_rows, width=Z_TILE):
    return pl.BlockSpec((None, rows, width), lambda *ids: (tile, index_rows(*ids), 0))


def _repeat_kv_heads(kv):
    pieces = [kv[:, i * HEAD_DIM:(i + 1) * HEAD_DIM] for i in range(2 * A_KV_HEADS)]
    k = jnp.concatenate([pieces[g] for g in range(A_KV_HEADS) for _ in range(A_GROUP)], axis=1)
    v = jnp.concatenate([pieces[A_KV_HEADS + g] for g in range(A_KV_HEADS) for _ in range(A_GROUP)], axis=1)
    return k, v


WIN_BLOCKS_PER_STEP = 8


def _win_gqa_kernel(sink_ref, q_ref, *rest, latent):
    if latent:
        kv_ref, ckv_ref, o_ref, k_scr, v_scr, kc_scr, vc_scr = rest
    else:
        ckv_ref, o_ref, kc_scr, vc_scr = rest
    gw = A_GROUP * HEAD_DIM

    def build():
        kc_scr[...], vc_scr[...] = _repeat_kv_heads(ckv_ref[...])
        if latent:
            k_scr[...], v_scr[...] = _repeat_kv_heads(kv_ref[...])

    if latent:
        pl.when(pl.program_id(1) == 0)(build)
    else:
        build()

    tq = A_BLOCK if latent else q_ref.shape[0]
    n_sub = q_ref.shape[0] // tq
    rows = A_GROUP * tq
    row_head = lax.broadcasted_iota(jnp.int32, (rows, 1), 0) // tq
    lane_head = _lane_group((tq, gw), HEAD_DIM)
    for j in range(n_sub):
        q = q_ref[j * tq:(j + 1) * tq, :] * ATTN_SCALE
        if latent:
            n = pl.program_id(1) * n_sub + j
            span = 3 * A_BLOCK
            start = pl.multiple_of(jnp.clip((n - 1) * A_BLOCK, 0, SEQ - span), A_BLOCK)
            qpos = n * A_BLOCK + lax.broadcasted_iota(jnp.int32, (tq, span), 0)
            kpos = start + lax.broadcasted_iota(jnp.int32, (tq, span), 1)
            band = jnp.where(jnp.abs(qpos - kpos) <= A_WINDOW, 0.0, NEG).astype(F32)
            band = jnp.concatenate([band] * A_GROUP, axis=0)
        for g in range(A_KV_HEADS):
            gs = slice(g * gw, (g + 1) * gw)
            qg = q[:, gs]
            zero = jnp.zeros_like(qg)
            q_stack = jnp.concatenate([jnp.where(lane_head == r, qg, zero) for r in range(A_GROUP)], axis=0)
            sink = jnp.zeros((rows, 1), F32)
            for r in range(A_GROUP):
                sink = jnp.where(row_head == r, sink_ref[g * A_GROUP + r], sink)
            segs = [(kc_scr[:, gs], vc_scr[:, gs], None)]
            if latent:
                segs.append((k_scr[pl.ds(start, span), gs], v_scr[pl.ds(start, span), gs], band))
            o, l = _softmax_attend(q_stack, segs, extra_logit=sink)
            o = o / l
            og = jnp.zeros((tq, gw), F32)
            for r in range(A_GROUP):
                og = jnp.where(lane_head == r, o[r * tq:(r + 1) * tq], og)
            o_ref[j * tq:(j + 1) * tq, gs] = og.astype(BF16)


def _win_gqa(zh, sink, *, latent):
    kvb = 2 * A_KV_HEADS * HEAD_DIM
    c0 = _ctx_blocks(CTX_LEN)
    smem = pl.BlockSpec(memory_space=pltpu.SMEM)
    rep = lambda n: pltpu.VMEM((n, A_HEADS * HEAD_DIM), BF16)
    if latent:
        tq = WIN_BLOCKS_PER_STEP * A_BLOCK
        nq = SEQ // tq
        grid = (BATCH, nq)
        in_specs = [
            smem,
            _zspec(H_AQ, tq, lambda b, n: b * nq + n),
            _zspec(H_AKV, SEQ, lambda b, n: b, kvb),
            _zspec(H_AKV, CTX_LEN, lambda b, n: c0 + b, kvb),
        ]
        out_specs = pl.BlockSpec((tq, Z_TILE), lambda b, n: (b * nq + n, 0))
        args = (sink, zh, zh, zh)
        rows = LATENT_ROWS
        sem = ("parallel", "arbitrary")
        scratch = [rep(SEQ), rep(SEQ), rep(CTX_LEN), rep(CTX_LEN)]
    else:
        grid = (BATCH,)
        in_specs = [
            smem,
            _zspec(H_AQ, CTX_LEN, lambda b: c0 + b),
            _zspec(H_AKV, CTX_LEN, lambda b: c0 + b, kvb),
        ]
        out_specs = pl.BlockSpec((CTX_LEN, Z_TILE), lambda b: (b, 0))
        args = (sink, zh, zh)
        rows = CTX_ROWS
        sem = ("parallel",)
        scratch = [rep(CTX_LEN), rep(CTX_LEN)]
    return pl.pallas_call(
        functools.partial(_win_gqa_kernel, latent=latent),
        grid=grid,
        in_specs=in_specs,
        out_specs=out_specs,
        out_shape=jax.ShapeDtypeStruct((rows, BRANCH_W), BF16),
        scratch_shapes=scratch,
        compiler_params=_cparams(sem),
        name="win_gqa" if latent else "win_gqa_ctx",
    )(*args)


N_ROW_OFFSETS = 2 * B_WIN_ROWS - 1
N_COL_OFFSETS = 2 * B_WIN_COLS - 1
N_PAIR = N_ROW_OFFSETS + 1
NBR_QROWS = 4
NBR_SLAB = NBR_QROWS + B_WIN_ROWS
NBR_GROUPS = GRID_H // NBR_QROWS
NBR_CASES = 3


def _nbr_case_geometry(case):
    if case == 0:
        return B_WIN_ROWS - 1, lambda a, c: 0 <= c < B_WIN_ROWS
    if case == 1:
        return B_WIN_ROWS - 1 - NBR_QROWS, lambda a, c: 0 <= c - a < B_WIN_ROWS
    return B_WIN_ROWS - 1 - 2 * NBR_QROWS, lambda a, c: NBR_QROWS <= c < NBR_SLAB


def _rpb_table_kernel(rpb_ref, o_ref, pair_scr):
    h = pl.program_id(0)
    shape = (GRID_W, LANES)
    qcol = lax.broadcasted_iota(jnp.int32, shape, 0)
    lane = lax.broadcasted_iota(jnp.int32, shape, 1)
    kcol = lane & (GRID_W - 1)
    hi = lane >= GRID_W
    cs = jnp.clip(qcol - B_WIN_COLS // 2, 0, GRID_W - B_WIN_COLS)
    in_window = (kcol >= cs) & (kcol < cs + B_WIN_COLS)
    ci = kcol - qcol + (B_WIN_COLS - 1)
    neg = jnp.full(shape, NEG, F32)

    def rpb(d, c):
        return rpb_ref[(h * N_ROW_OFFSETS + d) * N_COL_OFFSETS + c]

    tables = []
    for d in range(N_ROW_OFFSETS):
        acc = neg
        for c in range(N_COL_OFFSETS):
            acc = jnp.where(ci == c, rpb(d, c), acc)
        tables.append(jnp.where(in_window, acc, NEG))
    for p in range(N_PAIR):
        lo = tables[p - 1] if p >= 1 else neg
        up = tables[p] if p < N_ROW_OFFSETS else neg
        pair_scr[p] = jnp.where(hi, up, lo)

    for case in range(NBR_CASES):
        shift, in_rows = _nbr_case_geometry(case)
        for a in range(NBR_QROWS):
            for cp in range(NBR_SLAB // 2):
                c = 2 * cp
                lo_ok, hi_ok = in_rows(a, c), in_rows(a, c + 1)
                if lo_ok or hi_ok:
                    piece = pair_scr[c - a + shift + 1]
                    if not lo_ok:
                        piece = jnp.where(hi, piece, neg)
                    if not hi_ok:
                        piece = jnp.where(hi, neg, piece)
                else:
                    piece = neg
                o_ref[case, 0, a * GRID_W:(a + 1) * GRID_W, cp * LANES:(cp + 1) * LANES] = piece


def _rpb_table(rpb):
    nq, nk = NBR_QROWS * GRID_W, NBR_SLAB * GRID_W
    return pl.pallas_call(
        _rpb_table_kernel,
        grid=(B_HEADS,),
        in_specs=[pl.BlockSpec(memory_space=pltpu.SMEM)],
        out_specs=pl.BlockSpec((NBR_CASES, 1, nq, nk), lambda h: (0, h, 0, 0)),
        out_shape=jax.ShapeDtypeStruct((NBR_CASES, B_HEADS, nq, nk), F32),
        scratch_shapes=[pltpu.VMEM((N_PAIR, GRID_W, LANES), F32)],
        compiler_params=_cparams(("parallel",)),
        name="rpb_table",
    )(rpb.reshape(-1))


def _pair_attend(q, lane_hi, segs_of_head):
    zero = jnp.zeros_like(q)
    halves = []
    for hh in range(2):
        qm = jnp.where(lane_hi, q, zero) if hh else jnp.where(lane_hi, zero, q)
        o, l = _softmax_attend(qm, segs_of_head(hh))
        halves.append(o / l)
    return jnp.where(lane_hi, halves[1], halves[0])


NBR_GROUPS_PER_STEP = 2


def _nbr_kernel(q_ref, k_ref, v_ref, kc_ref, vc_ref, bias_ref, o_ref):
    nq = NBR_QROWS * GRID_W
    nk = NBR_SLAB * GRID_W
    lane_hi = lax.broadcasted_iota(jnp.int32, (nq, LANES), 1) >= HEAD_DIM
    for j in range(NBR_GROUPS_PER_STEP):
        g = pl.program_id(1) * NBR_GROUPS_PER_STEP + j
        case = jnp.where(g == 0, 0, jnp.where(g == NBR_GROUPS - 1, 2, 1))
        r0 = jnp.clip(NBR_QROWS * g - B_WIN_ROWS // 2, 0, GRID_H - NBR_SLAB)
        base = pl.multiple_of(r0 * GRID_W, GRID_W)
        rows = slice(j * nq, (j + 1) * nq)
        for pr in range(B_HEADS // 2):
            sl = slice(pr * LANES, (pr + 1) * LANES)
            kp = k_ref[pl.ds(base, nk), sl]
            vp = v_ref[pl.ds(base, nk), sl]
            kcp = kc_ref[:, sl]
            vcp = vc_ref[:, sl]
            segs = lambda hh: [(kp, vp, bias_ref[case, 2 * pr + hh]), (kcp, vcp, None)]
            q = q_ref[rows, sl] * ATTN_SCALE
            o_ref[rows, sl] = _pair_attend(q, lane_hi, segs).astype(BF16)


def _nbr_attn(zh, table):
    nq = NBR_GROUPS_PER_STEP * NBR_QROWS * GRID_W
    steps = NBR_GROUPS // NBR_GROUPS_PER_STEP
    c0 = _ctx_blocks(CTX_LEN)
    return pl.pallas_call(
        _nbr_kernel,
        grid=(BATCH, steps),
        in_specs=[
            _zspec(H_BQ, nq, lambda b, s: b * steps + s),
            _zspec(H_BK, SEQ, lambda b, s: b),
            _zspec(H_BV, SEQ, lambda b, s: b),
            _zspec(H_BK, CTX_LEN, lambda b, s: c0 + b),
            _zspec(H_BV, CTX_LEN, lambda b, s: c0 + b),
            pl.BlockSpec(table.shape, lambda b, s: (0, 0, 0, 0), pipeline_mode=pl.Buffered(1)),
        ],
        out_specs=pl.BlockSpec((nq, Z_TILE), lambda b, s: (b * steps + s, 0)),
        out_shape=jax.ShapeDtypeStruct((LATENT_ROWS, BRANCH_W), BF16),
        compiler_params=_cparams(("parallel", "arbitrary")),
        name="nbr_attn",
    )(zh, zh, zh, zh, zh, table)


def _ctx_mha_kernel(q_ref, k_ref, v_ref, o_ref):
    q = q_ref[...] * ATTN_SCALE
    lane_hi = lax.broadcasted_iota(jnp.int32, (q.shape[0], LANES), 1) >= HEAD_DIM
    for pr in range(B_HEADS // 2):
        sl = slice(pr * LANES, (pr + 1) * LANES)
        segs = lambda hh: [(k_ref[:, sl], v_ref[:, sl], None)]
        o_ref[:, sl] = _pair_attend(q[:, sl], lane_hi, segs).astype(BF16)


def _ctx_mha(zh):
    c0 = _ctx_blocks(CTX_LEN)
    rows = lambda b: c0 + b
    return pl.pallas_call(
        _ctx_mha_kernel,
        grid=(BATCH,),
        in_specs=[_zspec(H_BQ, CTX_LEN, rows), _zspec(H_BK, CTX_LEN, rows), _zspec(H_BV, CTX_LEN, rows)],
        out_specs=pl.BlockSpec((CTX_LEN, Z_TILE), lambda b: (b, 0)),
        out_shape=jax.ShapeDtypeStruct((CTX_ROWS, BRANCH_W), BF16),
        compiler_params=_cparams(("parallel",)),
        name="nbr_attn_ctx",
    )(zh, zh, zh)


def _pool_kernel(u_ref, w_ref, scale_ref, o_ref):
    n = u_ref.shape[0]
    t = lax.broadcasted_iota(jnp.int32, (n, C_GROUP_DIM), 0)

    def down(a, k):
        return jnp.where(t >= k, pltpu.roll(a, k, axis=0), 0.0)

    def up(a, k):
        return jnp.where(t < n - k, pltpu.roll(a, n - k, axis=0), 0.0)

    for g, w in enumerate(C_WINDOWS):
        sl = slice(g * C_GROUP_DIM, (g + 1) * C_GROUP_DIM)
        u = u_ref[:, sl].astype(F32)
        half = w // 2
        back = u
        fwd = u
        k = 1
        while k < half:
            back = back + down(back, k)
            fwd = fwd + up(fwd, k)
            k *= 2
        total = down(back, 1) + fwd
        lo = jnp.maximum(t - half, 0)
        hi = jnp.minimum(t + half - 1, n - 1)
        cnt = (hi - lo + 1).astype(F32)
        pooled = total / cnt - u
        y = _dot(pooled.astype(BF16), w_ref[0, g]) * scale_ref[:, sl]
        o_ref[:, sl] = y.astype(BF16)


def _pool(zh, c_w, c_scale, *, layer, n, latent):
    rows = LATENT_ROWS if latent else CTX_ROWS
    b0 = 0 if latent else _ctx_blocks(n)
    return pl.pallas_call(
        _pool_kernel,
        grid=(rows // n,),
        in_specs=[
            _zspec(H_CU, n, lambda b: b0 + b),
            pl.BlockSpec((1, len(C_WINDOWS), C_GROUP_DIM, C_GROUP_DIM), lambda b: (layer, 0, 0, 0)),
            pl.BlockSpec((1, BRANCH_W), lambda b: (0, 0)),
        ],
        out_specs=pl.BlockSpec((n, BRANCH_W), lambda b: (b, 0)),
        out_shape=jax.ShapeDtypeStruct((rows, BRANCH_W), BF16),
        compiler_params=_cparams(("parallel",)),
        name=f"pool_{n}",
    )(zh, c_w, c_scale)


DIFF_ROWS = 256


def _diff_kernel(lam_ref, q_ref, *rest, lam_init, latent):
    if latent:
        k_ref, v_ref, kc_ref, vc_ref, g_ref, o_ref = rest
    else:
        kc_ref, vc_ref, g_ref, o_ref = rest
    hw = 2 * HEAD_DIM
    dl = lam_ref[...]
    lam = (jnp.exp(jnp.sum(dl[0:1] * dl[1:2], axis=-1, keepdims=True))
           - jnp.exp(jnp.sum(dl[2:3] * dl[3:4], axis=-1, keepdims=True)) + lam_init)
    tq = min(q_ref.shape[0], DIFF_ROWS)
    lane_hi = lax.broadcasted_iota(jnp.int32, (tq, hw), 1) >= HEAD_DIM
    for r0 in range(0, q_ref.shape[0], tq):
        rows = slice(r0, r0 + tq)
        for h in range(D_HEADS):
            sl = slice(h * hw, (h + 1) * hw)
            qh = q_ref[rows, sl] * ATTN_SCALE
            zero = jnp.zeros_like(qh)
            q_stack = jnp.concatenate([jnp.where(lane_hi, zero, qh), jnp.where(lane_hi, qh, zero)], axis=0)
            segs = [(kc_ref[:, sl], vc_ref[:, sl], None)]
            if latent:
                segs.append((k_ref[:, sl], v_ref[:, sl], None))
            o, l = _softmax_attend(q_stack, segs)
            o = o / l
            od = o[:tq] - lam * o[tq:]
            ms = jnp.mean(od * od, axis=-1, keepdims=True)
            y = od * lax.rsqrt(ms + EPS) * g_ref[...] * (1.0 - lam_init)
            o_ref[rows, sl] = y.astype(BF16)


def _diff_attn(zh, d_lambda, subln, *, lam_init, latent):
    full = lambda shape: pl.BlockSpec(shape, lambda *_: (0,) * len(shape))
    c0 = _ctx_blocks(CTX_LEN)
    if latent:
        tq = 2 * DIFF_ROWS
        nq = SEQ // tq
        grid = (BATCH, nq)
        in_specs = [
            full((4, HEAD_DIM)),
            _zspec(H_DQ, tq, lambda b, n: b * nq + n),
            _zspec(H_DK, SEQ, lambda b, n: b),
            _zspec(H_DV, SEQ, lambda b, n: b),
            _zspec(H_DK, CTX_LEN, lambda b, n: c0 + b),
            _zspec(H_DV, CTX_LEN, lambda b, n: c0 + b),
            full((1, 2 * HEAD_DIM)),
        ]
        out_specs = pl.BlockSpec((tq, Z_TILE), lambda b, n: (b * nq + n, 0))
        args = (d_lambda, zh, zh, zh, zh, zh, subln)
        rows = LATENT_ROWS
        sem = ("parallel", "arbitrary")
    else:
        grid = (BATCH,)
        crow = lambda b: c0 + b
        in_specs = [
            full((4, HEAD_DIM)),
            _zspec(H_DQ, CTX_LEN, crow),
            _zspec(H_DK, CTX_LEN, crow),
            _zspec(H_DV, CTX_LEN, crow),
            full((1, 2 * HEAD_DIM)),
        ]
        out_specs = pl.BlockSpec((CTX_LEN, Z_TILE), lambda b: (b, 0))
        args = (d_lambda, zh, zh, zh, subln)
        rows = CTX_ROWS
        sem = ("parallel",)
    return pl.pallas_call(
        functools.partial(_diff_kernel, lam_init=lam_init, latent=latent),
        grid=grid,
        in_specs=in_specs,
        out_specs=out_specs,
        out_shape=jax.ShapeDtypeStruct((rows, BRANCH_W), BF16),
        compiler_params=_cparams(sem),
        name="diff_attn" if latent else "diff_attn_ctx",
    )(*args)


def _merge_kernel(*refs):
    y_refs = refs[:N_BRANCH]
    g_refs = refs[N_BRANCH:2 * N_BRANCH]
    wb_ref, wo_ref, x_ref, gt_ref, gn_ref, shift_ref, scale_ref, xo_ref, h_ref = refs[2 * N_BRANCH:]
    mixed = None
    for k in range(N_BRANCH):
        proj = _dot(y_refs[k][...], wb_ref[0, k]) * g_refs[k][...].astype(F32)
        mixed = proj if mixed is None else mixed + proj
    out = _dot(mixed.astype(BF16), wo_ref[0])
    xo_ref[...] = x_ref[...] + gt_ref[0] * out
    _norm_modulate_rows(xo_ref, gn_ref, shift_ref, scale_ref, h_ref)


def _merge(ys, gates, w_branch, w_out, x, gn2, mod3, *, tm, row0, mod_row):
    m = x.shape[0]
    row = lambda i: (i, 0)
    g0 = row0 // tm
    mod = lambda chunk: pl.BlockSpec((1, 1, D_MODEL), lambda i: (mod_row(i), 0, chunk))
    resident = pl.Buffered(1)
    return pl.pallas_call(
        _merge_kernel,
        grid=(m // tm,),
        in_specs=[pl.BlockSpec((tm, BRANCH_W), row)] * N_BRANCH
        + [pl.BlockSpec((tm, D_MODEL), lambda i, k=k: (g0 + i, k)) for k in range(N_BRANCH)]
        + [
            pl.BlockSpec((1, N_BRANCH, BRANCH_W, D_MODEL), lambda i: (0, 0, 0, 0), pipeline_mode=resident),
            pl.BlockSpec((1, D_MODEL, D_MODEL), lambda i: (0, 0, 0), pipeline_mode=resident),
            pl.BlockSpec((tm, D_MODEL), row),
            mod(2),
            pl.BlockSpec((1, D_MODEL), lambda i: (0, 0)),
            mod(3),
            mod(4),
        ],
        out_specs=[pl.BlockSpec((tm, D_MODEL), row), pl.BlockSpec((tm, D_MODEL), row)],
        out_shape=[jax.ShapeDtypeStruct((m, D_MODEL), F32), jax.ShapeDtypeStruct((m, D_MODEL), BF16)],
        compiler_params=_cparams(("parallel",)),
        name=f"merge_{m}",
    )(*ys, *([gates] * N_BRANCH), w_branch, w_out, x, mod3, gn2, mod3, mod3)


def _mlp_kernel(h_ref, x_ref, gt_ref, w1_ref, w2_ref, *rest, emit_next):
    if emit_next:
        gn_ref, shift_ref, scale_ref, xo_ref, hn_ref = rest
    else:
        (xo_ref,) = rest
    f = pl.program_id(1)

    @pl.when(f == 0)
    def _():
        xo_ref[...] = jnp.zeros_like(xo_ref)

    half_rows = h_ref.shape[0] // 2
    for r0 in range(0, h_ref.shape[0], half_rows):
        rows = slice(r0, r0 + half_rows)
        a = jnp.maximum(_dot(h_ref[rows, :], w1_ref[0]), 0.0)
        xo_ref[rows, :] += _dot((a * a).astype(BF16), w2_ref[0])

    @pl.when(f == pl.num_programs(1) - 1)
    def _():
        xo_ref[...] = x_ref[...] + gt_ref[0] * xo_ref[...]
        if emit_next:
            _norm_modulate_rows(xo_ref, gn_ref, shift_ref, scale_ref, hn_ref)


def _mlp(h, x, mod3, w1, w2, next_norm, *, tm, tf, mod_row):
    m = x.shape[0]
    emit_next = next_norm is not None
    row = lambda i, f: (i, 0)
    in_specs = [
        pl.BlockSpec((tm, D_MODEL), row),
        pl.BlockSpec((tm, D_MODEL), row),
        pl.BlockSpec((1, 1, D_MODEL), lambda i, f: (mod_row(i), 0, 5)),
        pl.BlockSpec((1, D_MODEL, tf), lambda i, f: (0, 0, f)),
        pl.BlockSpec((1, tf, D_MODEL), lambda i, f: (0, f, 0)),
    ]
    args = [h, x, mod3, w1, w2]
    out_specs = [pl.BlockSpec((tm, D_MODEL), row)]
    out_shape = [jax.ShapeDtypeStruct((m, D_MODEL), F32)]
    if emit_next:
        gn_next, mod3_next = next_norm
        in_specs += [
            pl.BlockSpec((1, D_MODEL), lambda i, f: (0, 0)),
            pl.BlockSpec((1, 1, D_MODEL), lambda i, f: (mod_row(i), 0, 0)),
            pl.BlockSpec((1, 1, D_MODEL), lambda i, f: (mod_row(i), 0, 1)),
        ]
        args += [gn_next, mod3_next, mod3_next]
        out_specs.append(pl.BlockSpec((tm, D_MODEL), row))
        out_shape.append(jax.ShapeDtypeStruct((m, D_MODEL), BF16))
    outs = pl.pallas_call(
        functools.partial(_mlp_kernel, emit_next=emit_next),
        grid=(m // tm, D_FF // tf),
        in_specs=in_specs,
        out_specs=out_specs,
        out_shape=out_shape,
        compiler_params=_cparams(("parallel", "arbitrary")),
        name=f"mlp_{m}",
    )(*args)
    return outs if emit_next else (outs[0], None)


def _rope_tables():
    t = jnp.arange(SEQ)
    row = (t // GRID_W).astype(F32)
    col = (t % GRID_W).astype(F32)
    n_freq = HEAD_DIM // 4
    inv = ROPE_BASE ** (-jnp.arange(n_freq, dtype=F32) / n_freq)
    ar = row[:, None] * inv
    ac = col[:, None] * inv
    cos_h = jnp.concatenate([jnp.cos(ar), jnp.cos(ar), jnp.cos(ac), jnp.cos(ac)], axis=-1)
    sin_h = jnp.concatenate([-jnp.sin(ar), jnp.sin(ar), -jnp.sin(ac), jnp.sin(ac)], axis=-1)
    reps = LANES // HEAD_DIM
    cos_t = jnp.concatenate([jnp.tile(cos_h, (1, reps)), jnp.ones((TM_PROJ, LANES), F32)], axis=0)
    sin_t = jnp.concatenate([jnp.tile(sin_h, (1, reps)), jnp.zeros((TM_PROJ, LANES), F32)], axis=0)
    return cos_t, sin_t


def _head_gain_row(qk_gain_l):
    ones = jnp.ones((Z_TILE,), F32)
    rep = lambda g: jnp.tile(g, Z_TILE // HEAD_DIM)
    akv = jnp.concatenate([jnp.tile(qk_gain_l[0, 1], A_KV_HEADS), jnp.ones((Z_TILE - A_KV_HEADS * HEAD_DIM,), F32)])
    return jnp.concatenate([
        rep(qk_gain_l[0, 0]), akv,
        rep(qk_gain_l[1, 0]), rep(qk_gain_l[1, 1]), ones, ones,
        rep(qk_gain_l[2, 0]), rep(qk_gain_l[2, 1]), ones,
    ])[None, :]


def kernel(x, c, ctx, c_ctx, w_ada, b_ada, g_norm1, g_norm2, w_in, b_gate, qk_gain, a_sink, b_rpb,
           c_w, c_scale, d_lambda, d_subln, w_branch, w_out, w_ff1, w_ff2):
    cos_t, sin_t = _rope_tables()
    lane = jnp.arange(HEAD_SUM_LANES)
    bd = (lane[:, None] // HEAD_DIM == lane[None, :] // HEAD_DIM).astype(BF16)

    cc = jnp.concatenate([c, c_ctx[None, :], jnp.zeros((8 - BATCH - 1, D_MODEL), F32)], axis=0)
    mod_all = _modulation(cc, w_ada, b_ada)

    cw = c_w.astype(BF16)
    w_branch2d = w_branch.reshape(DEPTH, N_BRANCH * BRANCH_W, D_MODEL)

    xl = x.reshape(LATENT_ROWS, D_MODEL)
    xc = ctx.reshape(CTX_ROWS, D_MODEL)
    ctx_row = lambda i: BATCH
    batch_row = lambda tm: (lambda i: i // (SEQ // tm))
    mods = [mod_all[l].reshape(8, 1, 6 * D_MODEL) for l in range(DEPTH)]
    gn1 = [g_norm1[l][None, :] for l in range(DEPTH)]

    tm_merge, tm_mlp, tf_mlp, tn_gate = TM_MERGE, TM_MLP, TF_MLP, TN_GATE
    hl = _norm_mod(xl, gn1[0], mods[0], tm=tm_mlp, mod_row=batch_row(tm_mlp))
    hc = _norm_mod(xc, gn1[0], mods[0], tm=tm_mlp, mod_row=ctx_row)

    for l in range(DEPTH):
        last = l == DEPTH - 1
        lam_init = 0.8 - 0.6 * math.exp(-0.3 * l)
        mod3 = mods[l]
        gn2 = g_norm2[l][None, :]
        cs = c_scale[l][None, :]
        subln = d_subln[l][None, :]
        next_norm = None if last else (gn1[l + 1], mods[l + 1])

        gates, w1, w2 = _gate_proj(hl, None if last else hc, w_in, b_gate[l][None, :], w_ff1, w_ff2,
                                   layer=l, tn=tn_gate)
        zh, wb, wo = _head_proj(hl, hc, w_in, _head_gain_row(qk_gain[l]), cos_t, sin_t, bd, w_branch2d, w_out,
                                layer=l, ctx_queries=not last)
        wb = wb.reshape(1, N_BRANCH, BRANCH_W, D_MODEL)

        table = _rpb_table(b_rpb[l])
        ya = _win_gqa(zh, a_sink[l], latent=True)
        yb = _nbr_attn(zh, table)
        yc = _pool(zh, cw, cs, layer=l, n=SEQ, latent=True)
        yd = _diff_attn(zh, d_lambda[l], subln, lam_init=lam_init, latent=True)
        xl, h2 = _merge((ya, yb, yc, yd), gates, wb, wo, xl, gn2, mod3,
                        tm=tm_merge, row0=0, mod_row=batch_row(tm_merge))
        xl, hl = _mlp(h2, xl, mod3, w1, w2, next_norm, tm=tm_mlp, tf=tf_mlp, mod_row=batch_row(tm_mlp))
        if not last:
            yac = _win_gqa(zh, a_sink[l], latent=False)
            ybc = _ctx_mha(zh)
            ycc = _pool(zh, cw, cs, layer=l, n=CTX_LEN, latent=False)
            ydc = _diff_attn(zh, d_lambda[l], subln, lam_init=lam_init, latent=False)
            xc, h2c = _merge((yac, ybc, ycc, ydc), gates, wb, wo, xc, gn2, mod3,
                             tm=tm_merge, row0=LATENT_ROWS, mod_row=ctx_row)
            xc, hc = _mlp(h2c, xc, mod3, w1, w2, next_norm, tm=tm_mlp, tf=tf_mlp, mod_row=ctx_row)

    return xl.reshape(BATCH, SEQ, D_MODEL)
```

```python
import functools
import math

import jax
import jax.numpy as jnp
from jax import lax
from jax.experimental import pallas as pl
from jax.experimental.pallas import tpu as pltpu

D_MODEL = 2048
BATCH = 4
SEQ = 2048
DEPTH = 2
GRID_W = 64
GRID_H = SEQ // GRID_W
CTX_LEN = 256
HEAD_DIM = 64
N_BRANCH = 4
BRANCH_W = D_MODEL // N_BRANCH
A_HEADS = 8
A_KV_HEADS = 2
A_GROUP = A_HEADS // A_KV_HEADS
A_WINDOW = 128
A_BLOCK = 128
B_HEADS = 8
B_WIN_ROWS = 8
B_WIN_COLS = 16
C_WINDOWS = (2, 4, 8, 16)
C_GROUP_DIM = 128
D_HEADS = 4
D_FF = 4 * D_MODEL
ROPE_BASE = 10000.0
EPS = 1e-6
NEG = -1e30
ATTN_SCALE = HEAD_DIM ** -0.5

F32 = jnp.float32
BF16 = jnp.bfloat16

V7X_VMEM_LIMIT_BYTES = 56 * 1024 * 1024
LANES = 128

LATENT_ROWS = BATCH * SEQ
CTX_ROWS = BATCH * CTX_LEN
ALL_ROWS = LATENT_ROWS + CTX_ROWS

Z_TILE = 512
H_AQ, H_AKV, H_BQ, H_BK, H_BV, H_CU, H_DQ, H_DK, H_DV = range(9)
HEAD_TILES = 9
W_SRC = 256
GATE_COLS = N_BRANCH * D_MODEL
GATE_SRC0 = (12544 - GATE_COLS) // W_SRC
HEAD_SUM_LANES = 256

TM_PROJ = 1024
TN_GATE = 1024
TM_MERGE = 256
TM_MLP = 512
TF_MLP = 1024


def _cparams(sem):
    return pltpu.CompilerParams(dimension_semantics=sem, vmem_limit_bytes=V7X_VMEM_LIMIT_BYTES)


def _dot_nt(a, b):
    return lax.dot_general(a, b, (((1,), (1,)), ((), ())), preferred_element_type=F32)


def _dot(a, b):
    return jnp.dot(a, b, preferred_element_type=F32)


def _ctx_blocks(block_rows):
    return LATENT_ROWS // block_rows


def _mod_kernel(c_ref, w_ref, b_ref, o_ref):
    c = c_ref[...]
    s = c * (0.5 * (jnp.tanh(0.5 * c) + 1.0))
    o_ref[0] = _dot(s, w_ref[0]) + b_ref[0]


def _modulation(cc, w_ada, b_ada):
    tn = 1024
    n = 6 * D_MODEL
    return pl.pallas_call(
        _mod_kernel,
        grid=(DEPTH, n // tn),
        in_specs=[
            pl.BlockSpec((8, D_MODEL), lambda l, j: (0, 0)),
            pl.BlockSpec((1, D_MODEL, tn), lambda l, j: (l, 0, j)),
            pl.BlockSpec((1, 1, tn), lambda l, j: (l, 0, j)),
        ],
        out_specs=pl.BlockSpec((1, 8, tn), lambda l, j: (l, 0, j)),
        out_shape=jax.ShapeDtypeStruct((DEPTH, 8, n), F32),
        compiler_params=_cparams(("parallel", "parallel")),
        name="modulation",
    )(cc, w_ada, b_ada.reshape(DEPTH, 1, n))


NORM_ROWS = 16


def _norm_modulate_rows(x_ref, gn_ref, shift_ref, scale_ref, h_ref):
    n = x_ref.shape[0]
    gs = gn_ref[...] * (1.0 + scale_ref[0])
    shift = shift_ref[0]

    def body(c, carry):
        rows = pl.ds(pl.multiple_of(c * NORM_ROWS, NORM_ROWS), NORM_ROWS)
        x = x_ref[rows, :]
        inv = lax.rsqrt(jnp.mean(x * x, axis=-1, keepdims=True) + EPS)
        h_ref[rows, :] = ((x * inv) * gs + shift).astype(BF16)
        return carry

    lax.fori_loop(0, n // NORM_ROWS, body, 0, unroll=8)


def _norm_mod_kernel(x_ref, gn_ref, shift_ref, scale_ref, h_ref):
    _norm_modulate_rows(x_ref, gn_ref, shift_ref, scale_ref, h_ref)


def _norm_mod(x, gn, mod3, *, tm, mod_row):
    m = x.shape[0]
    return pl.pallas_call(
        _norm_mod_kernel,
        grid=(m // tm,),
        in_specs=[
            pl.BlockSpec((tm, D_MODEL), lambda i: (i, 0)),
            pl.BlockSpec((1, D_MODEL), lambda i: (0, 0)),
            pl.BlockSpec((1, 1, D_MODEL), lambda i: (mod_row(i), 0, 0)),
            pl.BlockSpec((1, 1, D_MODEL), lambda i: (mod_row(i), 0, 1)),
        ],
        out_specs=pl.BlockSpec((tm, D_MODEL), lambda i: (i, 0)),
        out_shape=jax.ShapeDtypeStruct((m, D_MODEL), BF16),
        compiler_params=_cparams(("parallel",)),
        name=f"norm_mod_{m}",
    )(x, gn, mod3, mod3)


def _swap16(y):
    lane = lax.broadcasted_iota(jnp.int32, y.shape, 1)
    fwd = pltpu.roll(y, LANES - 16, axis=1)
    bwd = pltpu.roll(y, 16, axis=1)
    return jnp.where((lane & 16) == 0, fwd, bwd)


def _cast_weight_blocks(w_refs, w_scr):
    @pl.when(pl.program_id(1) == 0)
    def _():
        for c, w_ref in enumerate(w_refs):
            w_scr[:, c * W_SRC:(c + 1) * W_SRC] = w_ref[0].astype(BF16)


LATENT_TILES = LATENT_ROWS // TM_PROJ
CAST_CHUNKS = 64


def _row_tile(hl_ref, hc_ref, rows=slice(None)):
    if hc_ref is None:
        return hl_ref[rows, :]
    return jnp.where(pl.program_id(1) >= LATENT_TILES, hc_ref[rows, :], hl_ref[rows, :])


def _h_specs(with_ctx):
    specs = [pl.BlockSpec((TM_PROJ, D_MODEL), lambda n, i: (jnp.minimum(i, LATENT_TILES - 1), 0))]
    if with_ctx:
        specs.append(pl.BlockSpec((TM_PROJ, D_MODEL), lambda n, i: (0, 0), pipeline_mode=pl.Buffered(1)))
    return specs


def _side_cast_specs(arrays, *, layer, row_tiles):
    chunk = lambda n, i: jnp.minimum(n * row_tiles + i, CAST_CHUNKS - 1)
    in_specs, out_specs, out_shape = [], [], []
    for a in arrays:
        rows, cols = a.shape[1] // CAST_CHUNKS, a.shape[2]
        in_specs.append(pl.BlockSpec((1, rows, cols), lambda n, i: (layer, chunk(n, i), 0)))
        out_specs.append(pl.BlockSpec((1, rows, cols), lambda n, i: (0, chunk(n, i), 0)))
        out_shape.append(jax.ShapeDtypeStruct((1,) + a.shape[1:], BF16))
    return in_specs, out_specs, out_shape


def _gate_kernel(*refs, with_ctx):
    hl_ref = refs[0]
    hc_ref = refs[1] if with_ctx else None
    *w_refs, bias_ref, f1_ref, f2_ref, g_ref, f1o_ref, f2o_ref, w_scr = refs[1 + with_ctx:]
    _cast_weight_blocks(w_refs, w_scr)
    f1o_ref[...] = f1_ref[...].astype(BF16)
    f2o_ref[...] = f2_ref[...].astype(BF16)
    t = _dot(_row_tile(hl_ref, hc_ref), w_scr[...]) + bias_ref[...]
    g_ref[...] = (0.5 * (jnp.tanh(0.5 * t) + 1.0)).astype(BF16)


def _gate_proj(hl, hc, w_in, bias, w_ff1, w_ff2, *, layer, tn):
    with_ctx = hc is not None
    row_tiles = LATENT_TILES + with_ctx
    n_src = tn // W_SRC
    w_spec = lambda c: pl.BlockSpec((1, D_MODEL, W_SRC), lambda n, i: (layer, 0, GATE_SRC0 + n * n_src + c))
    cast_in, cast_out, cast_shape = _side_cast_specs((w_ff1, w_ff2), layer=layer, row_tiles=row_tiles)
    h_args = (hl, hc) if with_ctx else (hl,)
    return pl.pallas_call(
        functools.partial(_gate_kernel, with_ctx=with_ctx),
        grid=(GATE_COLS // tn, row_tiles),
        in_specs=_h_specs(with_ctx)
        + [w_spec(c) for c in range(n_src)]
        + [pl.BlockSpec((1, tn), lambda n, i: (0, n))]
        + cast_in,
        out_specs=[pl.BlockSpec((TM_PROJ, tn), lambda n, i: (i, n))] + cast_out,
        out_shape=[jax.ShapeDtypeStruct((row_tiles * TM_PROJ, GATE_COLS), BF16)] + cast_shape,
        scratch_shapes=[pltpu.VMEM((D_MODEL, tn), BF16)],
        compiler_params=_cparams(("arbitrary", "arbitrary")),
        name="gate_proj",
    )(*h_args, *([w_in] * n_src), bias, w_ff1, w_ff2)


def _head_src_block(t, half):
    return jnp.where(t == H_AQ, half, jnp.where(t == H_AKV, 2, 2 * t - 1 + half))


def _head_kernel(hl_ref, hc_ref, wa_ref, wb_ref, gain_ref, cos_ref, sin_ref, bd_ref, s1_ref, s2_ref,
                 z_ref, s1o_ref, s2o_ref, w_scr, *, ctx_queries):
    t = pl.program_id(0)
    _cast_weight_blocks((wa_ref, wb_ref), w_scr)
    s1o_ref[...] = s1_ref[...].astype(BF16)
    s2o_ref[...] = s2_ref[...].astype(BF16)
    is_ctx = pl.program_id(1) >= LATENT_TILES
    used = True if ctx_queries else ~is_ctx

    def norm_tile(n_norm_chunks, with_rope):
        wide = bd_ref.shape[0]
        half_rows = TM_PROJ // 2
        for r0 in range(0, TM_PROJ, half_rows):
            rows = slice(r0, r0 + half_rows)
            acc = _dot(_row_tile(hl_ref, hc_ref, rows), w_scr[...])
            for c2 in range(Z_TILE // wide):
                a = acc[:, c2 * wide:(c2 + 1) * wide]
                normed = a
                if c2 * (wide // LANES) < n_norm_chunks:
                    ms = _dot((a * a).astype(BF16), bd_ref[...]) * (1.0 / HEAD_DIM)
                    normed = a * lax.rsqrt(ms + EPS) * gain_ref[:, c2 * wide:(c2 + 1) * wide]
                for c in range(wide // LANES):
                    chunk = c2 * (wide // LANES) + c
                    y = (normed if chunk < n_norm_chunks else a)[:, c * LANES:(c + 1) * LANES]
                    if with_rope and chunk < n_norm_chunks:
                        y = y * cos_ref[rows, :] + _swap16(y) * sin_ref[rows, :]
                    z_ref[rows, chunk * LANES:(chunk + 1) * LANES] = y.astype(BF16)

    is_query = (t == H_AQ) | (t == H_BQ) | (t == H_DQ) | (t == H_CU)

    @pl.when(((t == H_AQ) | (t == H_DQ)) & used | (t == H_DK))
    def _():
        norm_tile(4, True)

    @pl.when(t == H_AKV)
    def _():
        norm_tile(1, True)

    @pl.when((t == H_BQ) & used | (t == H_BK))
    def _():
        norm_tile(4, False)

    @pl.when((t == H_BV) | (t == H_CU) & used | (t == H_DV))
    def _():
        norm_tile(0, False)

    if not ctx_queries:
        @pl.when(is_query & is_ctx)
        def _():
            z_ref[...] = jnp.zeros_like(z_ref)


def _head_proj(hl, hc, w_in, gain, cos_t, sin_t, bd, w_branch, w_out, *, layer, ctx_queries):
    seq_tiles = SEQ // TM_PROJ
    row_tiles = LATENT_TILES + 1
    rope_block = lambda i: jnp.where(i < LATENT_TILES, i % seq_tiles, seq_tiles)
    w_spec = lambda half: pl.BlockSpec((1, D_MODEL, W_SRC), lambda t, i: (layer, 0, _head_src_block(t, half)))
    cast_in, cast_out, cast_shape = _side_cast_specs((w_branch, w_out), layer=layer, row_tiles=row_tiles)
    return pl.pallas_call(
        functools.partial(_head_kernel, ctx_queries=ctx_queries),
        grid=(HEAD_TILES, row_tiles),
        in_specs=_h_specs(True) + [
            w_spec(0), w_spec(1),
            pl.BlockSpec((1, Z_TILE), lambda t, i: (0, t)),
            pl.BlockSpec((TM_PROJ, LANES), lambda t, i: (rope_block(i), 0)),
            pl.BlockSpec((TM_PROJ, LANES), lambda t, i: (rope_block(i), 0)),
            pl.BlockSpec(bd.shape, lambda t, i: (0, 0)),
        ] + cast_in,
        out_specs=[pl.BlockSpec((None, TM_PROJ, Z_TILE), lambda t, i: (t, i, 0))] + cast_out,
        out_shape=[jax.ShapeDtypeStruct((HEAD_TILES, ALL_ROWS, Z_TILE), BF16)] + cast_shape,
        scratch_shapes=[pltpu.VMEM((D_MODEL, Z_TILE), BF16)],
        compiler_params=_cparams(("arbitrary", "arbitrary")),
        name="head_proj",
    )(hl, hc, w_in, w_in, gain, cos_t, sin_t, bd, w_branch, w_out)


def _softmax_attend(q, segs, extra_logit=None):
    scores = []
    for k, _, bias in segs:
        s = _dot_nt(q, k)
        if bias is not None:
            s = s + bias
        scores.append(s)
    m = scores[0].max(axis=-1, keepdims=True)
    for s in scores[1:]:
        m = jnp.maximum(m, s.max(axis=-1, keepdims=True))
    if extra_logit is not None:
        m = jnp.maximum(m, extra_logit)
    l = None
    o = None
    for s, (_, v, _) in zip(scores, segs):
        p = jnp.exp(s - m)
        ls = p.sum(axis=-1, keepdims=True)
        os_ = _dot(p.astype(BF16), v)
        l = ls if l is None else l + ls
        o = os_ if o is None else o + os_
    if extra_logit is not None:
        l = l + jnp.exp(extra_logit - m)
    return o, l


def _lane_group(shape, width):
    return lax.broadcasted_iota(jnp.int32, shape, 1) // width


def _zspec(tile, rows, index_rows, width=Z_TILE):
    return pl.BlockSpec((None, rows, width), lambda *ids: (tile, index_rows(*ids), 0))


def _repeat_kv_heads(kv):
    pieces = [kv[:, i * HEAD_DIM:(i + 1) * HEAD_DIM] for i in range(2 * A_KV_HEADS)]
    k = jnp.concatenate([pieces[g] for g in range(A_KV_HEADS) for _ in range(A_GROUP)], axis=1)
    v = jnp.concatenate([pieces[A_KV_HEADS + g] for g in range(A_KV_HEADS) for _ in range(A_GROUP)], axis=1)
    return k, v


WIN_BLOCKS_PER_STEP = 8


def _win_gqa_kernel(sink_ref, q_ref, *rest, latent):
    if latent:
        kv_ref, ckv_ref, o_ref, k_scr, v_scr, kc_scr, vc_scr = rest
    else:
        ckv_ref, o_ref, kc_scr, vc_scr = rest
    gw = A_GROUP * HEAD_DIM

    def build():
        kc_scr[...], vc_scr[...] = _repeat_kv_heads(ckv_ref[...])
        if latent:
            k_scr[...], v_scr[...] = _repeat_kv_heads(kv_ref[...])

    if latent:
        pl.when(pl.program_id(1) == 0)(build)
    else:
        build()

    tq = A_BLOCK if latent else q_ref.shape[0]
    n_sub = q_ref.shape[0] // tq
    rows = A_GROUP * tq
    row_head = lax.broadcasted_iota(jnp.int32, (rows, 1), 0) // tq
    lane_head = _lane_group((tq, gw), HEAD_DIM)
    for j in range(n_sub):
        q = q_ref[j * tq:(j + 1) * tq, :] * ATTN_SCALE
        if latent:
            n = pl.program_id(1) * n_sub + j
            span = 3 * A_BLOCK
            start = pl.multiple_of(jnp.clip((n - 1) * A_BLOCK, 0, SEQ - span), A_BLOCK)
            qpos = n * A_BLOCK + lax.broadcasted_iota(jnp.int32, (tq, span), 0)
            kpos = start + lax.broadcasted_iota(jnp.int32, (tq, span), 1)
            band = jnp.where(jnp.abs(qpos - kpos) <= A_WINDOW, 0.0, NEG).astype(F32)
            band = jnp.concatenate([band] * A_GROUP, axis=0)
        for g in range(A_KV_HEADS):
            gs = slice(g * gw, (g + 1) * gw)
            qg = q[:, gs]
            zero = jnp.zeros_like(qg)
            q_stack = jnp.concatenate([jnp.where(lane_head == r, qg, zero) for r in range(A_GROUP)], axis=0)
            sink = jnp.zeros((rows, 1), F32)
            for r in range(A_GROUP):
                sink = jnp.where(row_head == r, sink_ref[g * A_GROUP + r], sink)
            segs = [(kc_scr[:, gs], vc_scr[:, gs], None)]
            if latent:
                segs.append((k_scr[pl.ds(start, span), gs], v_scr[pl.ds(start, span), gs], band))
            o, l = _softmax_attend(q_stack, segs, extra_logit=sink)
            o = o / l
            og = jnp.zeros((tq, gw), F32)
            for r in range(A_GROUP):
                og = jnp.where(lane_head == r, o[r * tq:(r + 1) * tq], og)
            o_ref[j * tq:(j + 1) * tq, gs] = og.astype(BF16)


def _win_gqa(zh, sink, *, latent):
    kvb = 2 * A_KV_HEADS * HEAD_DIM
    c0 = _ctx_blocks(CTX_LEN)
    smem = pl.BlockSpec(memory_space=pltpu.SMEM)
    rep = lambda n: pltpu.VMEM((n, A_HEADS * HEAD_DIM), BF16)
    if latent:
        tq = WIN_BLOCKS_PER_STEP * A_BLOCK
        nq = SEQ // tq
        grid = (BATCH, nq)
        in_specs = [
            smem,
            _zspec(H_AQ, tq, lambda b, n: b * nq + n),
            _zspec(H_AKV, SEQ, lambda b, n: b, kvb),
            _zspec(H_AKV, CTX_LEN, lambda b, n: c0 + b, kvb),
        ]
        out_specs = pl.BlockSpec((tq, Z_TILE), lambda b, n: (b * nq + n, 0))
        args = (sink, zh, zh, zh)
        rows = LATENT_ROWS
        sem = ("parallel", "arbitrary")
        scratch = [rep(SEQ), rep(SEQ), rep(CTX_LEN), rep(CTX_LEN)]
    else:
        grid = (BATCH,)
        in_specs = [
            smem,
            _zspec(H_AQ, CTX_LEN, lambda b: c0 + b),
            _zspec(H_AKV, CTX_LEN, lambda b: c0 + b, kvb),
        ]
        out_specs = pl.BlockSpec((CTX_LEN, Z_TILE), lambda b: (b, 0))
        args = (sink, zh, zh)
        rows = CTX_ROWS
        sem = ("parallel",)
        scratch = [rep(CTX_LEN), rep(CTX_LEN)]
    return pl.pallas_call(
        functools.partial(_win_gqa_kernel, latent=latent),
        grid=grid,
        in_specs=in_specs,
        out_specs=out_specs,
        out_shape=jax.ShapeDtypeStruct((rows, BRANCH_W), BF16),
        scratch_shapes=scratch,
        compiler_params=_cparams(sem),
        name="win_gqa" if latent else "win_gqa_ctx",
    )(*args)


N_ROW_OFFSETS = 2 * B_WIN_ROWS - 1
N_COL_OFFSETS = 2 * B_WIN_COLS - 1
N_PAIR = N_ROW_OFFSETS + 1
NBR_QROWS = 4
NBR_SLAB = NBR_QROWS + B_WIN_ROWS
NBR_GROUPS = GRID_H // NBR_QROWS
NBR_CASES = 3


def _nbr_case_geometry(case):
    if case == 0:
        return B_WIN_ROWS - 1, lambda a, c: 0 <= c < B_WIN_ROWS
    if case == 1:
        return B_WIN_ROWS - 1 - NBR_QROWS, lambda a, c: 0 <= c - a < B_WIN_ROWS
    return B_WIN_ROWS - 1 - 2 * NBR_QROWS, lambda a, c: NBR_QROWS <= c < NBR_SLAB


def _rpb_table_kernel(rpb_ref, o_ref, pair_scr):
    h = pl.program_id(0)
    shape = (GRID_W, LANES)
    qcol = lax.broadcasted_iota(jnp.int32, shape, 0)
    lane = lax.broadcasted_iota(jnp.int32, shape, 1)
    kcol = lane & (GRID_W - 1)
    hi = lane >= GRID_W
    cs = jnp.clip(qcol - B_WIN_COLS // 2, 0, GRID_W - B_WIN_COLS)
    in_window = (kcol >= cs) & (kcol < cs + B_WIN_COLS)
    ci = kcol - qcol + (B_WIN_COLS - 1)
    neg = jnp.full(shape, NEG, F32)

    def rpb(d, c):
        return rpb_ref[(h * N_ROW_OFFSETS + d) * N_COL_OFFSETS + c]

    tables = []
    for d in range(N_ROW_OFFSETS):
        acc = neg
        for c in range(N_COL_OFFSETS):
            acc = jnp.where(ci == c, rpb(d, c), acc)
        tables.append(jnp.where(in_window, acc, NEG))
    for p in range(N_PAIR):
        lo = tables[p - 1] if p >= 1 else neg
        up = tables[p] if p < N_ROW_OFFSETS else neg
        pair_scr[p] = jnp.where(hi, up, lo)

    for case in range(NBR_CASES):
        shift, in_rows = _nbr_case_geometry(case)
        for a in range(NBR_QROWS):
            for cp in range(NBR_SLAB // 2):
                c = 2 * cp
                lo_ok, hi_ok = in_rows(a, c), in_rows(a, c + 1)
                if lo_ok or hi_ok:
                    piece = pair_scr[c - a + shift + 1]
                    if not lo_ok:
                        piece = jnp.where(hi, piece, neg)
                    if not hi_ok:
                        piece = jnp.where(hi, neg, piece)
                else:
                    piece = neg
                o_ref[case, 0, a * GRID_W:(a + 1) * GRID_W, cp * LANES:(cp + 1) * LANES] = piece


def _rpb_table(rpb):
    nq, nk = NBR_QROWS * GRID_W, NBR_SLAB * GRID_W
    return pl.pallas_call(
        _rpb_table_kernel,
        grid=(B_HEADS,),
        in_specs=[pl.BlockSpec(memory_space=pltpu.SMEM)],
        out_specs=pl.BlockSpec((NBR_CASES, 1, nq, nk), lambda h: (0, h, 0, 0)),
        out_shape=jax.ShapeDtypeStruct((NBR_CASES, B_HEADS, nq, nk), F32),
        scratch_shapes=[pltpu.VMEM((N_PAIR, GRID_W, LANES), F32)],
        compiler_params=_cparams(("parallel",)),
        name="rpb_table",
    )(rpb.reshape(-1))


def _pair_attend(q, lane_hi, segs):
    m = q.shape[0]
    zero = jnp.zeros_like(q)
    q_stack = jnp.concatenate([jnp.where(lane_hi, zero, q), jnp.where(lane_hi, q, zero)], axis=0)
    o, l = _softmax_attend(q_stack, segs)
    o = o / l
    return jnp.where(lane_hi, o[m:], o[:m])


NBR_GROUPS_PER_STEP = 4


def _nbr_kernel(q_ref, k_ref, v_ref, kc_ref, vc_ref, bias_ref, o_ref):
    nq = NBR_QROWS * GRID_W
    nk = NBR_SLAB * GRID_W
    lane_hi = lax.broadcasted_iota(jnp.int32, (nq, LANES), 1) >= HEAD_DIM
    for j in range(NBR_GROUPS_PER_STEP):
        g = pl.program_id(1) * NBR_GROUPS_PER_STEP + j
        case = jnp.where(g == 0, 0, jnp.where(g == NBR_GROUPS - 1, 2, 1))
        r0 = jnp.clip(NBR_QROWS * g - B_WIN_ROWS // 2, 0, GRID_H - NBR_SLAB)
        base = pl.multiple_of(r0 * GRID_W, GRID_W)
        rows = slice(j * nq, (j + 1) * nq)
        for pr in range(B_HEADS // 2):
            sl = slice(pr * LANES, (pr + 1) * LANES)
            kp = k_ref[pl.ds(base, nk), sl]
            vp = v_ref[pl.ds(base, nk), sl]
            kcp = kc_ref[:, sl]
            vcp = vc_ref[:, sl]
            bias = bias_ref[case, 2 * pr:2 * pr + 2].reshape(2 * nq, nk)
            segs = [(kp, vp, bias), (kcp, vcp, None)]
            q = q_ref[rows, sl] * ATTN_SCALE
            o_ref[rows, sl] = _pair_attend(q, lane_hi, segs).astype(BF16)


def _nbr_attn(zh, table):
    nq = NBR_GROUPS_PER_STEP * NBR_QROWS * GRID_W
    steps = NBR_GROUPS // NBR_GROUPS_PER_STEP
    c0 = _ctx_blocks(CTX_LEN)
    return pl.pallas_call(
        _nbr_kernel,
        grid=(BATCH, steps),
        in_specs=[
            _zspec(H_BQ, nq, lambda b, s: b * steps + s),
            _zspec(H_BK, SEQ, lambda b, s: b),
            _zspec(H_BV, SEQ, lambda b, s: b),
            _zspec(H_BK, CTX_LEN, lambda b, s: c0 + b),
            _zspec(H_BV, CTX_LEN, lambda b, s: c0 + b),
            pl.BlockSpec(table.shape, lambda b, s: (0, 0, 0, 0), pipeline_mode=pl.Buffered(1)),
        ],
        out_specs=pl.BlockSpec((nq, Z_TILE), lambda b, s: (b * steps + s, 0)),
        out_shape=jax.ShapeDtypeStruct((LATENT_ROWS, BRANCH_W), BF16),
        compiler_params=_cparams(("parallel", "arbitrary")),
        name="nbr_attn",
    )(zh, zh, zh, zh, zh, table)


def _ctx_mha_kernel(q_ref, k_ref, v_ref, o_ref):
    q = q_ref[...] * ATTN_SCALE
    lane_hi = lax.broadcasted_iota(jnp.int32, (q.shape[0], LANES), 1) >= HEAD_DIM
    for pr in range(B_HEADS // 2):
        sl = slice(pr * LANES, (pr + 1) * LANES)
        segs = [(k_ref[:, sl], v_ref[:, sl], None)]
        o_ref[:, sl] = _pair_attend(q[:, sl], lane_hi, segs).astype(BF16)


def _ctx_mha(zh):
    c0 = _ctx_blocks(CTX_LEN)
    rows = lambda b: c0 + b
    return pl.pallas_call(
        _ctx_mha_kernel,
        grid=(BATCH,),
        in_specs=[_zspec(H_BQ, CTX_LEN, rows), _zspec(H_BK, CTX_LEN, rows), _zspec(H_BV, CTX_LEN, rows)],
        out_specs=pl.BlockSpec((CTX_LEN, Z_TILE), lambda b: (b, 0)),
        out_shape=jax.ShapeDtypeStruct((CTX_ROWS, BRANCH_W), BF16),
        compiler_params=_cparams(("parallel",)),
        name="nbr_attn_ctx",
    )(zh, zh, zh)


def _pool_kernel(u_ref, w_ref, scale_ref, o_ref):
    n = u_ref.shape[0]
    t = lax.broadcasted_iota(jnp.int32, (n, C_GROUP_DIM), 0)

    def down(a, k):
        return jnp.where(t >= k, pltpu.roll(a, k, axis=0), 0.0)

    def up(a, k):
        return jnp.where(t < n - k, pltpu.roll(a, n - k, axis=0), 0.0)

    for g, w in enumerate(C_WINDOWS):
        sl = slice(g * C_GROUP_DIM, (g + 1) * C_GROUP_DIM)
        u = u_ref[:, sl].astype(F32)
        half = w // 2
        back = u
        fwd = u
        k = 1
        while k < half:
            back = back + down(back, k)
            fwd = fwd + up(fwd, k)
            k *= 2
        total = down(back, 1) + fwd
        lo = jnp.maximum(t - half, 0)
        hi = jnp.minimum(t + half - 1, n - 1)
        cnt = (hi - lo + 1).astype(F32)
        pooled = total / cnt - u
        y = _dot(pooled.astype(BF16), w_ref[0, g]) * scale_ref[:, sl]
        o_ref[:, sl] = y.astype(BF16)


def _pool(zh, c_w, c_scale, *, layer, n, latent):
    rows = LATENT_ROWS if latent else CTX_ROWS
    b0 = 0 if latent else _ctx_blocks(n)
    return pl.pallas_call(
        _pool_kernel,
        grid=(rows // n,),
        in_specs=[
            _zspec(H_CU, n, lambda b: b0 + b),
            pl.BlockSpec((1, len(C_WINDOWS), C_GROUP_DIM, C_GROUP_DIM), lambda b: (layer, 0, 0, 0)),
            pl.BlockSpec((1, BRANCH_W), lambda b: (0, 0)),
        ],
        out_specs=pl.BlockSpec((n, BRANCH_W), lambda b: (b, 0)),
        out_shape=jax.ShapeDtypeStruct((rows, BRANCH_W), BF16),
        compiler_params=_cparams(("parallel",)),
        name=f"pool_{n}",
    )(zh, c_w, c_scale)


DIFF_ROWS = 256


def _diff_kernel(lam_ref, q_ref, *rest, lam_init, latent):
    if latent:
        k_ref, v_ref, kc_ref, vc_ref, g_ref, o_ref = rest
    else:
        kc_ref, vc_ref, g_ref, o_ref = rest
    hw = 2 * HEAD_DIM
    dl = lam_ref[...]
    lam = (jnp.exp(jnp.sum(dl[0:1] * dl[1:2], axis=-1, keepdims=True))
           - jnp.exp(jnp.sum(dl[2:3] * dl[3:4], axis=-1, keepdims=True)) + lam_init)
    tq = min(q_ref.shape[0], DIFF_ROWS)
    lane_hi = lax.broadcasted_iota(jnp.int32, (tq, hw), 1) >= HEAD_DIM
    for r0 in range(0, q_ref.shape[0], tq):
        rows = slice(r0, r0 + tq)
        for h in range(D_HEADS):
            sl = slice(h * hw, (h + 1) * hw)
            qh = q_ref[rows, sl] * ATTN_SCALE
            zero = jnp.zeros_like(qh)
            q_stack = jnp.concatenate([jnp.where(lane_hi, zero, qh), jnp.where(lane_hi, qh, zero)], axis=0)
            segs = [(kc_ref[:, sl], vc_ref[:, sl], None)]
            if latent:
                segs.append((k_ref[:, sl], v_ref[:, sl], None))
            o, l = _softmax_attend(q_stack, segs)
            o = o / l
            od = o[:tq] - lam * o[tq:]
            ms = jnp.mean(od * od, axis=-1, keepdims=True)
            y = od * lax.rsqrt(ms + EPS) * g_ref[...] * (1.0 - lam_init)
            o_ref[rows, sl] = y.astype(BF16)


def _diff_attn(zh, d_lambda, subln, *, lam_init, latent):
    full = lambda shape: pl.BlockSpec(shape, lambda *_: (0,) * len(shape))
    c0 = _ctx_blocks(CTX_LEN)
    if latent:
        tq = 2 * DIFF_ROWS
        nq = SEQ // tq
        grid = (BATCH, nq)
        in_specs = [
            full((4, HEAD_DIM)),
            _zspec(H_DQ, tq, lambda b, n: b * nq + n),
            _zspec(H_DK, SEQ, lambda b, n: b),
            _zspec(H_DV, SEQ, lambda b, n: b),
            _zspec(H_DK, CTX_LEN, lambda b, n: c0 + b),
            _zspec(H_DV, CTX_LEN, lambda b, n: c0 + b),
            full((1, 2 * HEAD_DIM)),
        ]
        out_specs = pl.BlockSpec((tq, Z_TILE), lambda b, n: (b * nq + n, 0))
        args = (d_lambda, zh, zh, zh, zh, zh, subln)
        rows = LATENT_ROWS
        sem = ("parallel", "arbitrary")
    else:
        grid = (BATCH,)
        crow = lambda b: c0 + b
        in_specs = [
            full((4, HEAD_DIM)),
            _zspec(H_DQ, CTX_LEN, crow),
            _zspec(H_DK, CTX_LEN, crow),
            _zspec(H_DV, CTX_LEN, crow),
            full((1, 2 * HEAD_DIM)),
        ]
        out_specs = pl.BlockSpec((CTX_LEN, Z_TILE), lambda b: (b, 0))
        args = (d_lambda, zh, zh, zh, subln)
        rows = CTX_ROWS
        sem = ("parallel",)
    return pl.pallas_call(
        functools.partial(_diff_kernel, lam_init=lam_init, latent=latent),
        grid=grid,
        in_specs=in_specs,
        out_specs=out_specs,
        out_shape=jax.ShapeDtypeStruct((rows, BRANCH_W), BF16),
        compiler_params=_cparams(sem),
        name="diff_attn" if latent else "diff_attn_ctx",
    )(*args)


def _merge_kernel(*refs):
    y_refs = refs[:N_BRANCH]
    g_refs = refs[N_BRANCH:2 * N_BRANCH]
    wb_ref, wo_ref, x_ref, gt_ref, gn_ref, shift_ref, scale_ref, xo_ref, h_ref = refs[2 * N_BRANCH:]
    mixed = None
    for k in range(N_BRANCH):
        proj = _dot(y_refs[k][...], wb_ref[0, k]) * g_refs[k][...].astype(F32)
        mixed = proj if mixed is None else mixed + proj
    out = _dot(mixed.astype(BF16), wo_ref[0])
    xo_ref[...] = x_ref[...] + gt_ref[0] * out
    _norm_modulate_rows(xo_ref, gn_ref, shift_ref, scale_ref, h_ref)


def _merge(ys, gates, w_branch, w_out, x, gn2, mod3, *, tm, row0, mod_row):
    m = x.shape[0]
    row = lambda i: (i, 0)
    g0 = row0 // tm
    mod = lambda chunk: pl.BlockSpec((1, 1, D_MODEL), lambda i: (mod_row(i), 0, chunk))
    resident = pl.Buffered(1)
    return pl.pallas_call(
        _merge_kernel,
        grid=(m // tm,),
        in_specs=[pl.BlockSpec((tm, BRANCH_W), row)] * N_BRANCH
        + [pl.BlockSpec((tm, D_MODEL), lambda i, k=k: (g0 + i, k)) for k in range(N_BRANCH)]
        + [
            pl.BlockSpec((1, N_BRANCH, BRANCH_W, D_MODEL), lambda i: (0, 0, 0, 0), pipeline_mode=resident),
            pl.BlockSpec((1, D_MODEL, D_MODEL), lambda i: (0, 0, 0), pipeline_mode=resident),
            pl.BlockSpec((tm, D_MODEL), row),
            mod(2),
            pl.BlockSpec((1, D_MODEL), lambda i: (0, 0)),
            mod(3),
            mod(4),
        ],
        out_specs=[pl.BlockSpec((tm, D_MODEL), row), pl.BlockSpec((tm, D_MODEL), row)],
        out_shape=[jax.ShapeDtypeStruct((m, D_MODEL), F32), jax.ShapeDtypeStruct((m, D_MODEL), BF16)],
        compiler_params=_cparams(("parallel",)),
        name=f"merge_{m}",
    )(*ys, *([gates] * N_BRANCH), w_branch, w_out, x, mod3, gn2, mod3, mod3)


def _mlp_kernel(h_ref, x_ref, gt_ref, w1_ref, w2_ref, *rest, emit_next):
    if emit_next:
        gn_ref, shift_ref, scale_ref, xo_ref, hn_ref = rest
    else:
        (xo_ref,) = rest
    f = pl.program_id(1)

    @pl.when(f == 0)
    def _():
        xo_ref[...] = jnp.zeros_like(xo_ref)

    a = jnp.maximum(_dot(h_ref[...], w1_ref[0]), 0.0)
    xo_ref[...] += _dot((a * a).astype(BF16), w2_ref[0])

    @pl.when(f == pl.num_programs(1) - 1)
    def _():
        xo_ref[...] = x_ref[...] + gt_ref[0] * xo_ref[...]
        if emit_next:
            _norm_modulate_rows(xo_ref, gn_ref, shift_ref, scale_ref, hn_ref)


def _mlp(h, x, mod3, w1, w2, next_norm, *, tm, tf, mod_row):
    m = x.shape[0]
    emit_next = next_norm is not None
    row = lambda i, f: (i, 0)
    in_specs = [
        pl.BlockSpec((tm, D_MODEL), row),
        pl.BlockSpec((tm, D_MODEL), row),
        pl.BlockSpec((1, 1, D_MODEL), lambda i, f: (mod_row(i), 0, 5)),
        pl.BlockSpec((1, D_MODEL, tf), lambda i, f: (0, 0, f)),
        pl.BlockSpec((1, tf, D_MODEL), lambda i, f: (0, f, 0)),
    ]
    args = [h, x, mod3, w1, w2]
    out_specs = [pl.BlockSpec((tm, D_MODEL), row)]
    out_shape = [jax.ShapeDtypeStruct((m, D_MODEL), F32)]
    if emit_next:
        gn_next, mod3_next = next_norm
        in_specs += [
            pl.BlockSpec((1, D_MODEL), lambda i, f: (0, 0)),
            pl.BlockSpec((1, 1, D_MODEL), lambda i, f: (mod_row(i), 0, 0)),
            pl.BlockSpec((1, 1, D_MODEL), lambda i, f: (mod_row(i), 0, 1)),
        ]
        args += [gn_next, mod3_next, mod3_next]
        out_specs.append(pl.BlockSpec((tm, D_MODEL), row))
        out_shape.append(jax.ShapeDtypeStruct((m, D_MODEL), BF16))
    outs = pl.pallas_call(
        functools.partial(_mlp_kernel, emit_next=emit_next),
        grid=(m // tm, D_FF // tf),
        in_specs=in_specs,
        out_specs=out_specs,
        out_shape=out_shape,
        compiler_params=_cparams(("parallel", "arbitrary")),
        name=f"mlp_{m}",
    )(*args)
    return outs if emit_next else (outs[0], None)


def _rope_tables():
    t = jnp.arange(SEQ)
    row = (t // GRID_W).astype(F32)
    col = (t % GRID_W).astype(F32)
    n_freq = HEAD_DIM // 4
    inv = ROPE_BASE ** (-jnp.arange(n_freq, dtype=F32) / n_freq)
    ar = row[:, None] * inv
    ac = col[:, None] * inv
    cos_h = jnp.concatenate([jnp.cos(ar), jnp.cos(ar), jnp.cos(ac), jnp.cos(ac)], axis=-1)
    sin_h = jnp.concatenate([-jnp.sin(ar), jnp.sin(ar), -jnp.sin(ac), jnp.sin(ac)], axis=-1)
    reps = LANES // HEAD_DIM
    cos_t = jnp.concatenate([jnp.tile(cos_h, (1, reps)), jnp.ones((TM_PROJ, LANES), F32)], axis=0)
    sin_t = jnp.concatenate([jnp.tile(sin_h, (1, reps)), jnp.zeros((TM_PROJ, LANES), F32)], axis=0)
    return cos_t, sin_t


def _head_gain_row(qk_gain_l):
    ones = jnp.ones((Z_TILE,), F32)
    rep = lambda g: jnp.tile(g, Z_TILE // HEAD_DIM)
    akv = jnp.concatenate([jnp.tile(qk_gain_l[0, 1], A_KV_HEADS), jnp.ones((Z_TILE - A_KV_HEADS * HEAD_DIM,), F32)])
    return jnp.concatenate([
        rep(qk_gain_l[0, 0]), akv,
        rep(qk_gain_l[1, 0]), rep(qk_gain_l[1, 1]), ones, ones,
        rep(qk_gain_l[2, 0]), rep(qk_gain_l[2, 1]), ones,
    ])[None, :]


def kernel(x, c, ctx, c_ctx, w_ada, b_ada, g_norm1, g_norm2, w_in, b_gate, qk_gain, a_sink, b_rpb,
           c_w, c_scale, d_lambda, d_subln, w_branch, w_out, w_ff1, w_ff2):
    cos_t, sin_t = _rope_tables()
    lane = jnp.arange(HEAD_SUM_LANES)
    bd = (lane[:, None] // HEAD_DIM == lane[None, :] // HEAD_DIM).astype(BF16)

    cc = jnp.concatenate([c, c_ctx[None, :], jnp.zeros((8 - BATCH - 1, D_MODEL), F32)], axis=0)
    mod_all = _modulation(cc, w_ada, b_ada)

    cw = c_w.astype(BF16)
    w_branch2d = w_branch.reshape(DEPTH, N_BRANCH * BRANCH_W, D_MODEL)

    xl = x.reshape(LATENT_ROWS, D_MODEL)
    xc = ctx.reshape(CTX_ROWS, D_MODEL)
    ctx_row = lambda i: BATCH
    batch_row = lambda tm: (lambda i: i // (SEQ // tm))
    mods = [mod_all[l].reshape(8, 1, 6 * D_MODEL) for l in range(DEPTH)]
    gn1 = [g_norm1[l][None, :] for l in range(DEPTH)]

    tm_merge, tm_mlp, tf_mlp, tn_gate = TM_MERGE, TM_MLP, TF_MLP, TN_GATE
    hl = _norm_mod(xl, gn1[0], mods[0], tm=tm_mlp, mod_row=batch_row(tm_mlp))
    hc = _norm_mod(xc, gn1[0], mods[0], tm=tm_mlp, mod_row=ctx_row)

    for l in range(DEPTH):
        last = l == DEPTH - 1
        lam_init = 0.8 - 0.6 * math.exp(-0.3 * l)
        mod3 = mods[l]
        gn2 = g_norm2[l][None, :]
        cs = c_scale[l][None, :]
        subln = d_subln[l][None, :]
        next_norm = None if last else (gn1[l + 1], mods[l + 1])

        gates, w1, w2 = _gate_proj(hl, None if last else hc, w_in, b_gate[l][None, :], w_ff1, w_ff2,
                                   layer=l, tn=tn_gate)
        zh, wb, wo = _head_proj(hl, hc, w_in, _head_gain_row(qk_gain[l]), cos_t, sin_t, bd, w_branch2d, w_out,
                                layer=l, ctx_queries=not last)
        wb = wb.reshape(1, N_BRANCH, BRANCH_W, D_MODEL)

        table = _rpb_table(b_rpb[l])
        ya = _win_gqa(zh, a_sink[l], latent=True)
        yb = _nbr_attn(zh, table)
        yc = _pool(zh, cw, cs, layer=l, n=SEQ, latent=True)
        yd = _diff_attn(zh, d_lambda[l], subln, lam_init=lam_init, latent=True)
        xl, h2 = _merge((ya, yb, yc, yd), gates, wb, wo, xl, gn2, mod3,
                        tm=tm_merge, row0=0, mod_row=batch_row(tm_merge))
        xl, hl = _mlp(h2, xl, mod3, w1, w2, next_norm, tm=tm_mlp, tf=tf_mlp, mod_row=batch_row(tm_mlp))
        if not last:
            yac = _win_gqa(zh, a_sink[l], latent=False)
            ybc = _ctx_mha(zh)
            ycc = _pool(zh, cw, cs, layer=l, n=CTX_LEN, latent=False)
            ydc = _diff_attn(zh, d_lambda[l], subln, lam_init=lam_init, latent=False)
            xc, h2c = _merge((yac, ybc, ycc, ydc), gates, wb, wo, xc, gn2, mod3,
                             tm=tm_merge, row0=LATENT_ROWS, mod_row=ctx_row)
            xc, hc = _mlp(h2c, xc, mod3, w1, w2, next_norm, tm=tm_mlp, tf=tf_mlp, mod_row=ctx_row)

    return xl.reshape(BATCH, SEQ, D_MODEL)
```

```python
import functools
import math

import jax
import jax.numpy as jnp
from jax import lax
from jax.experimental import pallas as pl
from jax.experimental.pallas import tpu as pltpu

D_MODEL = 2048
BATCH = 4
SEQ = 2048
DEPTH = 2
GRID_W = 64
GRID_H = SEQ // GRID_W
CTX_LEN = 256
HEAD_DIM = 64
N_BRANCH = 4
BRANCH_W = D_MODEL // N_BRANCH
A_HEADS = 8
A_KV_HEADS = 2
A_GROUP = A_HEADS // A_KV_HEADS
A_WINDOW = 128
A_BLOCK = 128
B_HEADS = 8
B_WIN_ROWS = 8
B_WIN_COLS = 16
C_WINDOWS = (2, 4, 8, 16)
C_GROUP_DIM = 128
D_HEADS = 4
D_FF = 4 * D_MODEL
ROPE_BASE = 10000.0
EPS = 1e-6
NEG = -1e30
ATTN_SCALE = HEAD_DIM ** -0.5

F32 = jnp.float32
BF16 = jnp.bfloat16

V7X_VMEM_LIMIT_BYTES = 56 * 1024 * 1024
LANES = 128

LATENT_ROWS = BATCH * SEQ
CTX_ROWS = BATCH * CTX_LEN
ALL_ROWS = LATENT_ROWS + CTX_ROWS

Z_TILE = 512
H_AQ, H_AKV, H_BQ, H_BK, H_BV, H_CU, H_DQ, H_DK, H_DV = range(9)
HEAD_TILES = 9
W_SRC = 256
GATE_COLS = N_BRANCH * D_MODEL
GATE_SRC0 = (12544 - GATE_COLS) // W_SRC
HEAD_SUM_LANES = 256

TM_PROJ = 1024
TN_GATE = 1024
TM_MERGE = 256
TM_MLP = 512
TF_MLP = 1024


def _cparams(sem):
    return pltpu.CompilerParams(dimension_semantics=sem, vmem_limit_bytes=V7X_VMEM_LIMIT_BYTES)


def _dot_nt(a, b):
    return lax.dot_general(a, b, (((1,), (1,)), ((), ())), preferred_element_type=F32)


def _dot(a, b):
    return jnp.dot(a, b, preferred_element_type=F32)


def _ctx_blocks(block_rows):
    return LATENT_ROWS // block_rows


def _mod_kernel(c_ref, w_ref, b_ref, o_ref):
    c = c_ref[...]
    s = c * (0.5 * (jnp.tanh(0.5 * c) + 1.0))
    o_ref[0] = _dot(s, w_ref[0]) + b_ref[0]


def _modulation(cc, w_ada, b_ada):
    tn = 1024
    n = 6 * D_MODEL
    return pl.pallas_call(
        _mod_kernel,
        grid=(DEPTH, n // tn),
        in_specs=[
            pl.BlockSpec((8, D_MODEL), lambda l, j: (0, 0)),
            pl.BlockSpec((1, D_MODEL, tn), lambda l, j: (l, 0, j)),
            pl.BlockSpec((1, 1, tn), lambda l, j: (l, 0, j)),
        ],
        out_specs=pl.BlockSpec((1, 8, tn), lambda l, j: (l, 0, j)),
        out_shape=jax.ShapeDtypeStruct((DEPTH, 8, n), F32),
        compiler_params=_cparams(("parallel", "parallel")),
        name="modulation",
    )(cc, w_ada, b_ada.reshape(DEPTH, 1, n))


NORM_ROWS = 16


def _norm_modulate_rows(x_ref, gn_ref, shift_ref, scale_ref, h_ref):
    n = x_ref.shape[0]
    gs = gn_ref[...] * (1.0 + scale_ref[0])
    shift = shift_ref[0]

    def body(c, carry):
        rows = pl.ds(pl.multiple_of(c * NORM_ROWS, NORM_ROWS), NORM_ROWS)
        x = x_ref[rows, :]
        inv = lax.rsqrt(jnp.mean(x * x, axis=-1, keepdims=True) + EPS)
        h_ref[rows, :] = ((x * inv) * gs + shift).astype(BF16)
        return carry

    lax.fori_loop(0, n // NORM_ROWS, body, 0, unroll=8)


def _norm_mod_kernel(x_ref, gn_ref, shift_ref, scale_ref, h_ref):
    _norm_modulate_rows(x_ref, gn_ref, shift_ref, scale_ref, h_ref)


def _norm_mod(x, gn, mod3, *, tm, mod_row):
    m = x.shape[0]
    return pl.pallas_call(
        _norm_mod_kernel,
        grid=(m // tm,),
        in_specs=[
            pl.BlockSpec((tm, D_MODEL), lambda i: (i, 0)),
            pl.BlockSpec((1, D_MODEL), lambda i: (0, 0)),
            pl.BlockSpec((1, 1, D_MODEL), lambda i: (mod_row(i), 0, 0)),
            pl.BlockSpec((1, 1, D_MODEL), lambda i: (mod_row(i), 0, 1)),
        ],
        out_specs=pl.BlockSpec((tm, D_MODEL), lambda i: (i, 0)),
        out_shape=jax.ShapeDtypeStruct((m, D_MODEL), BF16),
        compiler_params=_cparams(("parallel",)),
        name=f"norm_mod_{m}",
    )(x, gn, mod3, mod3)


def _swap16(y):
    lane = lax.broadcasted_iota(jnp.int32, y.shape, 1)
    fwd = pltpu.roll(y, LANES - 16, axis=1)
    bwd = pltpu.roll(y, 16, axis=1)
    return jnp.where((lane & 16) == 0, fwd, bwd)


def _cast_weight_blocks(w_refs, w_scr):
    @pl.when(pl.program_id(1) == 0)
    def _():
        for c, w_ref in enumerate(w_refs):
            w_scr[:, c * W_SRC:(c + 1) * W_SRC] = w_ref[0].astype(BF16)


LATENT_TILES = LATENT_ROWS // TM_PROJ
CAST_CHUNKS = 64


def _row_tile(hl_ref, hc_ref, rows=slice(None)):
    if hc_ref is None:
        return hl_ref[rows, :]
    return jnp.where(pl.program_id(1) >= LATENT_TILES, hc_ref[rows, :], hl_ref[rows, :])


def _h_specs(with_ctx):
    specs = [pl.BlockSpec((TM_PROJ, D_MODEL), lambda n, i: (jnp.minimum(i, LATENT_TILES - 1), 0))]
    if with_ctx:
        specs.append(pl.BlockSpec((TM_PROJ, D_MODEL), lambda n, i: (0, 0), pipeline_mode=pl.Buffered(1)))
    return specs


def _side_cast_specs(arrays, *, layer, row_tiles):
    chunk = lambda n, i: jnp.minimum(n * row_tiles + i, CAST_CHUNKS - 1)
    in_specs, out_specs, out_shape = [], [], []
    for a in arrays:
        rows, cols = a.shape[1] // CAST_CHUNKS, a.shape[2]
        in_specs.append(pl.BlockSpec((1, rows, cols), lambda n, i: (layer, chunk(n, i), 0)))
        out_specs.append(pl.BlockSpec((1, rows, cols), lambda n, i: (0, chunk(n, i), 0)))
        out_shape.append(jax.ShapeDtypeStruct((1,) + a.shape[1:], BF16))
    return in_specs, out_specs, out_shape


def _gate_kernel(*refs, with_ctx):
    hl_ref = refs[0]
    hc_ref = refs[1] if with_ctx else None
    *w_refs, bias_ref, f1_ref, f2_ref, g_ref, f1o_ref, f2o_ref, w_scr = refs[1 + with_ctx:]
    _cast_weight_blocks(w_refs, w_scr)
    f1o_ref[...] = f1_ref[...].astype(BF16)
    f2o_ref[...] = f2_ref[...].astype(BF16)
    t = _dot(_row_tile(hl_ref, hc_ref), w_scr[...]) + bias_ref[...]
    g_ref[...] = (0.5 * (jnp.tanh(0.5 * t) + 1.0)).astype(BF16)


def _gate_proj(hl, hc, w_in, bias, w_ff1, w_ff2, *, layer, tn):
    with_ctx = hc is not None
    row_tiles = LATENT_TILES + with_ctx
    n_src = tn // W_SRC
    w_spec = lambda c: pl.BlockSpec((1, D_MODEL, W_SRC), lambda n, i: (layer, 0, GATE_SRC0 + n * n_src + c))
    cast_in, cast_out, cast_shape = _side_cast_specs((w_ff1, w_ff2), layer=layer, row_tiles=row_tiles)
    h_args = (hl, hc) if with_ctx else (hl,)
    return pl.pallas_call(
        functools.partial(_gate_kernel, with_ctx=with_ctx),
        grid=(GATE_COLS // tn, row_tiles),
        in_specs=_h_specs(with_ctx)
        + [w_spec(c) for c in range(n_src)]
        + [pl.BlockSpec((1, tn), lambda n, i: (0, n))]
        + cast_in,
        out_specs=[pl.BlockSpec((TM_PROJ, tn), lambda n, i: (i, n))] + cast_out,
        out_shape=[jax.ShapeDtypeStruct((row_tiles * TM_PROJ, GATE_COLS), BF16)] + cast_shape,
        scratch_shapes=[pltpu.VMEM((D_MODEL, tn), BF16)],
        compiler_params=_cparams(("arbitrary", "arbitrary")),
        name="gate_proj",
    )(*h_args, *([w_in] * n_src), bias, w_ff1, w_ff2)


def _head_src_block(t, half):
    return jnp.where(t == H_AQ, half, jnp.where(t == H_AKV, 2, 2 * t - 1 + half))


def _head_kernel(hl_ref, hc_ref, wa_ref, wb_ref, gain_ref, cos_ref, sin_ref, bd_ref, s1_ref, s2_ref,
                 z_ref, s1o_ref, s2o_ref, w_scr, *, ctx_queries):
    t = pl.program_id(0)
    _cast_weight_blocks((wa_ref, wb_ref), w_scr)
    s1o_ref[...] = s1_ref[...].astype(BF16)
    s2o_ref[...] = s2_ref[...].astype(BF16)
    is_ctx = pl.program_id(1) >= LATENT_TILES
    used = True if ctx_queries else ~is_ctx

    def norm_tile(n_norm_chunks, with_rope):
        wide = bd_ref.shape[0]
        half_rows = TM_PROJ // 2
        for r0 in range(0, TM_PROJ, half_rows):
            rows = slice(r0, r0 + half_rows)
            acc = _dot(_row_tile(hl_ref, hc_ref, rows), w_scr[...])
            for c2 in range(Z_TILE // wide):
                a = acc[:, c2 * wide:(c2 + 1) * wide]
                normed = a
                if c2 * (wide // LANES) < n_norm_chunks:
                    ms = _dot((a * a).astype(BF16), bd_ref[...]) * (1.0 / HEAD_DIM)
                    normed = a * lax.rsqrt(ms + EPS) * gain_ref[:, c2 * wide:(c2 + 1) * wide]
                for c in range(wide // LANES):
                    chunk = c2 * (wide // LANES) + c
                    y = (normed if chunk < n_norm_chunks else a)[:, c * LANES:(c + 1) * LANES]
                    if with_rope and chunk < n_norm_chunks:
                        y = y * cos_ref[rows, :] + _swap16(y) * sin_ref[rows, :]
                    z_ref[rows, chunk * LANES:(chunk + 1) * LANES] = y.astype(BF16)

    is_query = (t == H_AQ) | (t == H_BQ) | (t == H_DQ) | (t == H_CU)

    @pl.when(((t == H_AQ) | (t == H_DQ)) & used | (t == H_DK))
    def _():
        norm_tile(4, True)

    @pl.when(t == H_AKV)
    def _():
        norm_tile(1, True)

    @pl.when((t == H_BQ) & used | (t == H_BK))
    def _():
        norm_tile(4, False)

    @pl.when((t == H_BV) | (t == H_CU) & used | (t == H_DV))
    def _():
        norm_tile(0, False)

    if not ctx_queries:
        @pl.when(is_query & is_ctx)
        def _():
            z_ref[...] = jnp.zeros_like(z_ref)


def _head_proj(hl, hc, w_in, gain, cos_t, sin_t, bd, w_branch, w_out, *, layer, ctx_queries):
    seq_tiles = SEQ // TM_PROJ
    row_tiles = LATENT_TILES + 1
    rope_block = lambda i: jnp.where(i < LATENT_TILES, i % seq_tiles, seq_tiles)
    w_spec = lambda half: pl.BlockSpec((1, D_MODEL, W_SRC), lambda t, i: (layer, 0, _head_src_block(t, half)))
    cast_in, cast_out, cast_shape = _side_cast_specs((w_branch, w_out), layer=layer, row_tiles=row_tiles)
    return pl.pallas_call(
        functools.partial(_head_kernel, ctx_queries=ctx_queries),
        grid=(HEAD_TILES, row_tiles),
        in_specs=_h_specs(True) + [
            w_spec(0), w_spec(1),
            pl.BlockSpec((1, Z_TILE), lambda t, i: (0, t)),
            pl.BlockSpec((TM_PROJ, LANES), lambda t, i: (rope_block(i), 0)),
            pl.BlockSpec((TM_PROJ, LANES), lambda t, i: (rope_block(i), 0)),
            pl.BlockSpec(bd.shape, lambda t, i: (0, 0)),
        ] + cast_in,
        out_specs=[pl.BlockSpec((None, TM_PROJ, Z_TILE), lambda t, i: (t, i, 0))] + cast_out,
        out_shape=[jax.ShapeDtypeStruct((HEAD_TILES, ALL_ROWS, Z_TILE), BF16)] + cast_shape,
        scratch_shapes=[pltpu.VMEM((D_MODEL, Z_TILE), BF16)],
        compiler_params=_cparams(("arbitrary", "arbitrary")),
        name="head_proj",
    )(hl, hc, w_in, w_in, gain, cos_t, sin_t, bd, w_branch, w_out)


def _softmax_attend(q, segs, extra_logit=None):
    scores = []
    for k, _, bias in segs:
        s = _dot_nt(q, k)
        if bias is not None:
            s = s + bias
        scores.append(s)
    m = scores[0].max(axis=-1, keepdims=True)
    for s in scores[1:]:
        m = jnp.maximum(m, s.max(axis=-1, keepdims=True))
    if extra_logit is not None:
        m = jnp.maximum(m, extra_logit)
    l = None
    o = None
    for s, (_, v, _) in zip(scores, segs):
        p = jnp.exp(s - m)
        ls = p.sum(axis=-1, keepdims=True)
        os_ = _dot(p.astype(BF16), v)
        l = ls if l is None else l + ls
        o = os_ if o is None else o + os_
    if extra_logit is not None:
        l = l + jnp.exp(extra_logit - m)
    return o, l


def _lane_group(shape, width):
    return lax.broadcasted_iota(jnp.int32, shape, 1) // width


def _zspec(tile, rows, index_rows, width=Z_TILE):
    return pl.BlockSpec((None, rows, width), lambda *ids: (tile, index_rows(*ids), 0))


def _repeat_kv_heads(kv):
    pieces = [kv[:, i * HEAD_DIM:(i + 1) * HEAD_DIM] for i in range(2 * A_KV_HEADS)]
    k = jnp.concatenate([pieces[g] for g in range(A_KV_HEADS) for _ in range(A_GROUP)], axis=1)
    v = jnp.concatenate([pieces[A_KV_HEADS + g] for g in range(A_KV_HEADS) for _ in range(A_GROUP)], axis=1)
    return k, v


WIN_BLOCKS_PER_STEP = 8


def _win_gqa_kernel(sink_ref, q_ref, *rest, latent):
    if latent:
        kv_ref, ckv_ref, o_ref, k_scr, v_scr, kc_scr, vc_scr = rest
    else:
        ckv_ref, o_ref, kc_scr, vc_scr = rest
    gw = A_GROUP * HEAD_DIM

    def build():
        kc_scr[...], vc_scr[...] = _repeat_kv_heads(ckv_ref[...])
        if latent:
            k_scr[...], v_scr[...] = _repeat_kv_heads(kv_ref[...])

    if latent:
        pl.when(pl.program_id(1) == 0)(build)
    else:
        build()

    tq = A_BLOCK if latent else q_ref.shape[0]
    n_sub = q_ref.shape[0] // tq
    rows = A_GROUP * tq
    row_head = lax.broadcasted_iota(jnp.int32, (rows, 1), 0) // tq
    lane_head = _lane_group((tq, gw), HEAD_DIM)
    for j in range(n_sub):
        q = q_ref[j * tq:(j + 1) * tq, :] * ATTN_SCALE
        if latent:
            n = pl.program_id(1) * n_sub + j
            span = 3 * A_BLOCK
            start = pl.multiple_of(jnp.clip((n - 1) * A_BLOCK, 0, SEQ - span), A_BLOCK)
            qpos = n * A_BLOCK + lax.broadcasted_iota(jnp.int32, (tq, span), 0)
            kpos = start + lax.broadcasted_iota(jnp.int32, (tq, span), 1)
            band = jnp.where(jnp.abs(qpos - kpos) <= A_WINDOW, 0.0, NEG).astype(F32)
            band = jnp.concatenate([band] * A_GROUP, axis=0)
        for g in range(A_KV_HEADS):
            gs = slice(g * gw, (g + 1) * gw)
            qg = q[:, gs]
            zero = jnp.zeros_like(qg)
            q_stack = jnp.concatenate([jnp.where(lane_head == r, qg, zero) for r in range(A_GROUP)], axis=0)
            sink = jnp.zeros((rows, 1), F32)
            for r in range(A_GROUP):
                sink = jnp.where(row_head == r, sink_ref[g * A_GROUP + r], sink)
            segs = [(kc_scr[:, gs], vc_scr[:, gs], None)]
            if latent:
                segs.append((k_scr[pl.ds(start, span), gs], v_scr[pl.ds(start, span), gs], band))
            o, l = _softmax_attend(q_stack, segs, extra_logit=sink)
            o = o / l
            og = jnp.zeros((tq, gw), F32)
            for r in range(A_GROUP):
                og = jnp.where(lane_head == r, o[r * tq:(r + 1) * tq], og)
            o_ref[j * tq:(j + 1) * tq, gs] = og.astype(BF16)


def _win_gqa(zh, sink, *, latent):
    kvb = 2 * A_KV_HEADS * HEAD_DIM
    c0 = _ctx_blocks(CTX_LEN)
    smem = pl.BlockSpec(memory_space=pltpu.SMEM)
    rep = lambda n: pltpu.VMEM((n, A_HEADS * HEAD_DIM), BF16)
    if latent:
        tq = WIN_BLOCKS_PER_STEP * A_BLOCK
        nq = SEQ // tq
        grid = (BATCH, nq)
        in_specs = [
            smem,
            _zspec(H_AQ, tq, lambda b, n: b * nq + n),
            _zspec(H_AKV, SEQ, lambda b, n: b, kvb),
            _zspec(H_AKV, CTX_LEN, lambda b, n: c0 + b, kvb),
        ]
        out_specs = pl.BlockSpec((tq, Z_TILE), lambda b, n: (b * nq + n, 0))
        args = (sink, zh, zh, zh)
        rows = LATENT_ROWS
        sem = ("parallel", "arbitrary")
        scratch = [rep(SEQ), rep(SEQ), rep(CTX_LEN), rep(CTX_LEN)]
    else:
        grid = (BATCH,)
        in_specs = [
            smem,
            _zspec(H_AQ, CTX_LEN, lambda b: c0 + b),
            _zspec(H_AKV, CTX_LEN, lambda b: c0 + b, kvb),
        ]
        out_specs = pl.BlockSpec((CTX_LEN, Z_TILE), lambda b: (b, 0))
        args = (sink, zh, zh)
        rows = CTX_ROWS
        sem = ("parallel",)
        scratch = [rep(CTX_LEN), rep(CTX_LEN)]
    return pl.pallas_call(
        functools.partial(_win_gqa_kernel, latent=latent),
        grid=grid,
        in_specs=in_specs,
        out_specs=out_specs,
        out_shape=jax.ShapeDtypeStruct((rows, BRANCH_W), BF16),
        scratch_shapes=scratch,
        compiler_params=_cparams(sem),
        name="win_gqa" if latent else "win_gqa_ctx",
    )(*args)


N_ROW_OFFSETS = 2 * B_WIN_ROWS - 1
N_COL_OFFSETS = 2 * B_WIN_COLS - 1
N_PAIR = N_ROW_OFFSETS + 1
NBR_QROWS = 4
NBR_SLAB = NBR_QROWS + B_WIN_ROWS
NBR_GROUPS = GRID_H // NBR_QROWS
NBR_CASES = 3


def _nbr_case_geometry(case):
    if case == 0:
        return B_WIN_ROWS - 1, lambda a, c: 0 <= c < B_WIN_ROWS
    if case == 1:
        return B_WIN_ROWS - 1 - NBR_QROWS, lambda a, c: 0 <= c - a < B_WIN_ROWS
    return B_WIN_ROWS - 1 - 2 * NBR_QROWS, lambda a, c: NBR_QROWS <= c < NBR_SLAB


def _rpb_table_kernel(rpb_ref, o_ref, pair_scr):
    h = pl.program_id(0)
    shape = (GRID_W, LANES)
    qcol = lax.broadcasted_iota(jnp.int32, shape, 0)
    lane = lax.broadcasted_iota(jnp.int32, shape, 1)
    kcol = lane & (GRID_W - 1)
    hi = lane >= GRID_W
    cs = jnp.clip(qcol - B_WIN_COLS // 2, 0, GRID_W - B_WIN_COLS)
    in_window = (kcol >= cs) & (kcol < cs + B_WIN_COLS)
    ci = kcol - qcol + (B_WIN_COLS - 1)
    neg = jnp.full(shape, NEG, F32)

    def rpb(d, c):
        return rpb_ref[(h * N_ROW_OFFSETS + d) * N_COL_OFFSETS + c]

    tables = []
    for d in range(N_ROW_OFFSETS):
        acc = neg
        for c in range(N_COL_OFFSETS):
            acc = jnp.where(ci == c, rpb(d, c), acc)
        tables.append(jnp.where(in_window, acc, NEG))
    for p in range(N_PAIR):
        lo = tables[p - 1] if p >= 1 else neg
        up = tables[p] if p < N_ROW_OFFSETS else neg
        pair_scr[p] = jnp.where(hi, up, lo)

    for case in range(NBR_CASES):
        shift, in_rows = _nbr_case_geometry(case)
        for a in range(NBR_QROWS):
            for cp in range(NBR_SLAB // 2):
                c = 2 * cp
                lo_ok, hi_ok = in_rows(a, c), in_rows(a, c + 1)
                if lo_ok or hi_ok:
                    piece = pair_scr[c - a + shift + 1]
                    if not lo_ok:
                        piece = jnp.where(hi, piece, neg)
                    if not hi_ok:
                        piece = jnp.where(hi, neg, piece)
                else:
                    piece = neg
                o_ref[case, 0, a * GRID_W:(a + 1) * GRID_W, cp * LANES:(cp + 1) * LANES] = piece


def _rpb_table(rpb):
    nq, nk = NBR_QROWS * GRID_W, NBR_SLAB * GRID_W
    return pl.pallas_call(
        _rpb_table_kernel,
        grid=(B_HEADS,),
        in_specs=[pl.BlockSpec(memory_space=pltpu.SMEM)],
        out_specs=pl.BlockSpec((NBR_CASES, 1, nq, nk), lambda h: (0, h, 0, 0)),
        out_shape=jax.ShapeDtypeStruct((NBR_CASES, B_HEADS, nq, nk), F32),
        scratch_shapes=[pltpu.VMEM((N_PAIR, GRID_W, LANES), F32)],
        compiler_params=_cparams(("parallel",)),
        name="rpb_table",
    )(rpb.reshape(-1))


def _pair_attend(q, lane_hi, segs):
    m = q.shape[0]
    zero = jnp.zeros_like(q)
    q_stack = jnp.concatenate([jnp.where(lane_hi, zero, q), jnp.where(lane_hi, q, zero)], axis=0)
    o, l = _softmax_attend(q_stack, segs)
    o = o / l
    return jnp.where(lane_hi, o[m:], o[:m])


NBR_GROUPS_PER_STEP = 4


def _nbr_kernel(q_ref, k_ref, v_ref, kc_ref, vc_ref, bias_ref, o_ref):
    nq = NBR_QROWS * GRID_W
    nk = NBR_SLAB * GRID_W
    lane_hi = lax.broadcasted_iota(jnp.int32, (nq, LANES), 1) >= HEAD_DIM
    for j in range(NBR_GROUPS_PER_STEP):
        g = pl.program_id(1) * NBR_GROUPS_PER_STEP + j
        case = jnp.where(g == 0, 0, jnp.where(g == NBR_GROUPS - 1, 2, 1))
        r0 = jnp.clip(NBR_QROWS * g - B_WIN_ROWS // 2, 0, GRID_H - NBR_SLAB)
        base = pl.multiple_of(r0 * GRID_W, GRID_W)
        rows = slice(j * nq, (j + 1) * nq)
        for pr in range(B_HEADS // 2):
            sl = slice(pr * LANES, (pr + 1) * LANES)
            kp = k_ref[pl.ds(base, nk), sl]
            vp = v_ref[pl.ds(base, nk), sl]
            kcp = kc_ref[:, sl]
            vcp = vc_ref[:, sl]
            bias = bias_ref[case, 2 * pr:2 * pr + 2].reshape(2 * nq, nk)
            segs = [(kp, vp, bias), (kcp, vcp, None)]
            q = q_ref[rows, sl] * ATTN_SCALE
            o_ref[rows, sl] = _pair_attend(q, lane_hi, segs).astype(BF16)


def _nbr_attn(zh, table):
    nq = NBR_GROUPS_PER_STEP * NBR_QROWS * GRID_W
    steps = NBR_GROUPS // NBR_GROUPS_PER_STEP
    c0 = _ctx_blocks(CTX_LEN)
    return pl.pallas_call(
        _nbr_kernel,
        grid=(BATCH, steps),
        in_specs=[
            _zspec(H_BQ, nq, lambda b, s: b * steps + s),
            _zspec(H_BK, SEQ, lambda b, s: b),
            _zspec(H_BV, SEQ, lambda b, s: b),
            _zspec(H_BK, CTX_LEN, lambda b, s: c0 + b),
            _zspec(H_BV, CTX_LEN, lambda b, s: c0 + b),
            pl.BlockSpec(table.shape, lambda b, s: (0, 0, 0, 0), pipeline_mode=pl.Buffered(1)),
        ],
        out_specs=pl.BlockSpec((nq, Z_TILE), lambda b, s: (b * steps + s, 0)),
        out_shape=jax.ShapeDtypeStruct((LATENT_ROWS, BRANCH_W), BF16),
        compiler_params=_cparams(("parallel", "arbitrary")),
        name="nbr_attn",
    )(zh, zh, zh, zh, zh, table)


def _ctx_mha_kernel(q_ref, k_ref, v_ref, o_ref):
    q = q_ref[...] * ATTN_SCALE
    lane_hi = lax.broadcasted_iota(jnp.int32, (q.shape[0], LANES), 1) >= HEAD_DIM
    for pr in range(B_HEADS // 2):
        sl = slice(pr * LANES, (pr + 1) * LANES)
        segs = [(k_ref[:, sl], v_ref[:, sl], None)]
        o_ref[:, sl] = _pair_attend(q[:, sl], lane_hi, segs).astype(BF16)


def _ctx_mha(zh):
    c0 = _ctx_blocks(CTX_LEN)
    rows = lambda b: c0 + b
    return pl.pallas_call(
        _ctx_mha_kernel,
        grid=(BATCH,),
        in_specs=[_zspec(H_BQ, CTX_LEN, rows), _zspec(H_BK, CTX_LEN, rows), _zspec(H_BV, CTX_LEN, rows)],
        out_specs=pl.BlockSpec((CTX_LEN, Z_TILE), lambda b: (b, 0)),
        out_shape=jax.ShapeDtypeStruct((CTX_ROWS, BRANCH_W), BF16),
        compiler_params=_cparams(("parallel",)),
        name="nbr_attn_ctx",
    )(zh, zh, zh)


def _pool_kernel(u_ref, w_ref, scale_ref, o_ref):
    n = u_ref.shape[0]
    t = lax.broadcasted_iota(jnp.int32, (n, C_GROUP_DIM), 0)

    def down(a, k):
        return jnp.where(t >= k, pltpu.roll(a, k, axis=0), 0.0)

    def up(a, k):
        return jnp.where(t < n - k, pltpu.roll(a, n - k, axis=0), 0.0)

    for g, w in enumerate(C_WINDOWS):
        sl = slice(g * C_GROUP_DIM, (g + 1) * C_GROUP_DIM)
        u = u_ref[:, sl].astype(F32)
        half = w // 2
        back = u
        fwd = u
        k = 1
        while k < half:
            back = back + down(back, k)
            fwd = fwd + up(fwd, k)
            k *= 2
        total = down(back, 1) + fwd
        lo = jnp.maximum(t - half, 0)
        hi = jnp.minimum(t + half - 1, n - 1)
        cnt = (hi - lo + 1).astype(F32)
        pooled = total / cnt - u
        y = _dot(pooled.astype(BF16), w_ref[0, g]) * scale_ref[:, sl]
        o_ref[:, sl] = y.astype(BF16)


def _pool(zh, c_w, c_scale, *, layer, n, latent):
    rows = LATENT_ROWS if latent else CTX_ROWS
    b0 = 0 if latent else _ctx_blocks(n)
    return pl.pallas_call(
        _pool_kernel,
        grid=(rows // n,),
        in_specs=[
            _zspec(H_CU, n, lambda b: b0 + b),
            pl.BlockSpec((1, len(C_WINDOWS), C_GROUP_DIM, C_GROUP_DIM), lambda b: (layer, 0, 0, 0)),
            pl.BlockSpec((1, BRANCH_W), lambda b: (0, 0)),
        ],
        out_specs=pl.BlockSpec((n, BRANCH_W), lambda b: (b, 0)),
        out_shape=jax.ShapeDtypeStruct((rows, BRANCH_W), BF16),
        compiler_params=_cparams(("parallel",)),
        name=f"pool_{n}",
    )(zh, c_w, c_scale)


DIFF_ROWS = 256


def _diff_kernel(lam_ref, q_ref, *rest, lam_init, latent):
    if latent:
        k_ref, v_ref, kc_ref, vc_ref, g_ref, o_ref = rest
    else:
        kc_ref, vc_ref, g_ref, o_ref = rest
    hw = 2 * HEAD_DIM
    dl = lam_ref[...]
    lam = (jnp.exp(jnp.sum(dl[0:1] * dl[1:2], axis=-1, keepdims=True))
           - jnp.exp(jnp.sum(dl[2:3] * dl[3:4], axis=-1, keepdims=True)) + lam_init)
    tq = min(q_ref.shape[0], DIFF_ROWS)
    lane_hi = lax.broadcasted_iota(jnp.int32, (tq, hw), 1) >= HEAD_DIM
    for r0 in range(0, q_ref.shape[0], tq):
        rows = slice(r0, r0 + tq)
        for h in range(D_HEADS):
            sl = slice(h * hw, (h + 1) * hw)
            qh = q_ref[rows, sl] * ATTN_SCALE
            zero = jnp.zeros_like(qh)
            q_stack = jnp.concatenate([jnp.where(lane_hi, zero, qh), jnp.where(lane_hi, qh, zero)], axis=0)
            segs = [(kc_ref[:, sl], vc_ref[:, sl], None)]
            if latent:
                segs.append((k_ref[:, sl], v_ref[:, sl], None))
            o, l = _softmax_attend(q_stack, segs)
            o = o / l
            od = o[:tq] - lam * o[tq:]
            ms = jnp.mean(od * od, axis=-1, keepdims=True)
            y = od * lax.rsqrt(ms + EPS) * g_ref[...] * (1.0 - lam_init)
            o_ref[rows, sl] = y.astype(BF16)


def _diff_attn(zh, d_lambda, subln, *, lam_init, latent):
    full = lambda shape: pl.BlockSpec(shape, lambda *_: (0,) * len(shape))
    c0 = _ctx_blocks(CTX_LEN)
    if latent:
        tq = 2 * DIFF_ROWS
        nq = SEQ // tq
        grid = (BATCH, nq)
        in_specs = [
            full((4, HEAD_DIM)),
            _zspec(H_DQ, tq, lambda b, n: b * nq + n),
            _zspec(H_DK, SEQ, lambda b, n: b),
            _zspec(H_DV, SEQ, lambda b, n: b),
            _zspec(H_DK, CTX_LEN, lambda b, n: c0 + b),
            _zspec(H_DV, CTX_LEN, lambda b, n: c0 + b),
            full((1, 2 * HEAD_DIM)),
        ]
        out_specs = pl.BlockSpec((tq, Z_TILE), lambda b, n: (b * nq + n, 0))
        args = (d_lambda, zh, zh, zh, zh, zh, subln)
        rows = LATENT_ROWS
        sem = ("parallel", "arbitrary")
    else:
        grid = (BATCH,)
        crow = lambda b: c0 + b
        in_specs = [
            full((4, HEAD_DIM)),
            _zspec(H_DQ, CTX_LEN, crow),
            _zspec(H_DK, CTX_LEN, crow),
            _zspec(H_DV, CTX_LEN, crow),
            full((1, 2 * HEAD_DIM)),
        ]
        out_specs = pl.BlockSpec((CTX_LEN, Z_TILE), lambda b: (b, 0))
        args = (d_lambda, zh, zh, zh, subln)
        rows = CTX_ROWS
        sem = ("parallel",)
    return pl.pallas_call(
        functools.partial(_diff_kernel, lam_init=lam_init, latent=latent),
        grid=grid,
        in_specs=in_specs,
        out_specs=out_specs,
        out_shape=jax.ShapeDtypeStruct((rows, BRANCH_W), BF16),
        compiler_params=_cparams(sem),
        name="diff_attn" if latent else "diff_attn_ctx",
    )(*args)


def _merge_kernel(*refs):
    y_refs = refs[:N_BRANCH]
    g_refs = refs[N_BRANCH:2 * N_BRANCH]
    wb_ref, wo_ref, x_ref, gt_ref, gn_ref, shift_ref, scale_ref, xo_ref, h_ref = refs[2 * N_BRANCH:]
    mixed = None
    for k in range(N_BRANCH):
        proj = _dot(y_refs[k][...], wb_ref[0, k]) * g_refs[k][...].astype(F32)
        mixed = proj if mixed is None else mixed + proj
    out = _dot(mixed.astype(BF16), wo_ref[0])
    xo_ref[...] = x_ref[...] + gt_ref[0] * out
    _norm_modulate_rows(xo_ref, gn_ref, shift_ref, scale_ref, h_ref)


def _merge(ys, gates, w_branch, w_out, x, gn2, mod3, *, tm, row0, mod_row):
    m = x.shape[0]
    row = lambda i: (i, 0)
    g0 = row0 // tm
    mod = lambda chunk: pl.BlockSpec((1, 1, D_MODEL), lambda i: (mod_row(i), 0, chunk))
    resident = pl.Buffered(1)
    return pl.pallas_call(
        _merge_kernel,
        grid=(m // tm,),
        in_specs=[pl.BlockSpec((tm, BRANCH_W), row)] * N_BRANCH
        + [pl.BlockSpec((tm, D_MODEL), lambda i, k=k: (g0 + i, k)) for k in range(N_BRANCH)]
        + [
            pl.BlockSpec((1, N_BRANCH, BRANCH_W, D_MODEL), lambda i: (0, 0, 0, 0), pipeline_mode=resident),
            pl.BlockSpec((1, D_MODEL, D_MODEL), lambda i: (0, 0, 0), pipeline_mode=resident),
            pl.BlockSpec((tm, D_MODEL), row),
            mod(2),
            pl.BlockSpec((1, D_MODEL), lambda i: (0, 0)),
            mod(3),
            mod(4),
        ],
        out_specs=[pl.BlockSpec((tm, D_MODEL), row), pl.BlockSpec((tm, D_MODEL), row)],
        out_shape=[jax.ShapeDtypeStruct((m, D_MODEL), F32), jax.ShapeDtypeStruct((m, D_MODEL), BF16)],
        compiler_params=_cparams(("parallel",)),
        name=f"merge_{m}",
    )(*ys, *([gates] * N_BRANCH), w_branch, w_out, x, mod3, gn2, mod3, mod3)


def _mlp_kernel(h_ref, x_ref, gt_ref, w1_ref, w2_ref, *rest, emit_next):
    if emit_next:
        gn_ref, shift_ref, scale_ref, xo_ref, hn_ref, a_scr = rest
    else:
        xo_ref, a_scr = rest
    f = pl.program_id(1)
    last = pl.num_programs(1) - 1

    def hidden(slot):
        a = jnp.maximum(_dot(h_ref[...], w1_ref[0]), 0.0)
        a_scr[slot] = (a * a).astype(BF16)

    def accumulate(slot):
        xo_ref[...] += _dot(a_scr[slot], w2_ref[0])

    @pl.when(f == 0)
    def _():
        xo_ref[...] = jnp.zeros_like(xo_ref)
        hidden(0)

    for parity in range(2):
        @pl.when((f > 0) & (f < last) & (f % 2 == parity))
        def _(parity=parity):
            hidden(parity)
            accumulate(1 - parity)

    @pl.when(f == last)
    def _():
        accumulate((D_FF // w2_ref.shape[1] - 1) % 2)
        xo_ref[...] = x_ref[...] + gt_ref[0] * xo_ref[...]
        if emit_next:
            _norm_modulate_rows(xo_ref, gn_ref, shift_ref, scale_ref, hn_ref)


def _mlp(h, x, mod3, w1, w2, next_norm, *, tm, tf, mod_row):
    m = x.shape[0]
    nf = D_FF // tf
    emit_next = next_norm is not None
    row = lambda i, f: (i, 0)
    in_specs = [
        pl.BlockSpec((tm, D_MODEL), row),
        pl.BlockSpec((tm, D_MODEL), row),
        pl.BlockSpec((1, 1, D_MODEL), lambda i, f: (mod_row(i), 0, 5)),
        pl.BlockSpec((1, D_MODEL, tf), lambda i, f: (0, 0, jnp.minimum(f, nf - 1))),
        pl.BlockSpec((1, tf, D_MODEL), lambda i, f: (0, jnp.maximum(f - 1, 0), 0)),
    ]
    args = [h, x, mod3, w1, w2]
    out_specs = [pl.BlockSpec((tm, D_MODEL), row)]
    out_shape = [jax.ShapeDtypeStruct((m, D_MODEL), F32)]
    if emit_next:
        gn_next, mod3_next = next_norm
        in_specs += [
            pl.BlockSpec((1, D_MODEL), lambda i, f: (0, 0)),
            pl.BlockSpec((1, 1, D_MODEL), lambda i, f: (mod_row(i), 0, 0)),
            pl.BlockSpec((1, 1, D_MODEL), lambda i, f: (mod_row(i), 0, 1)),
        ]
        args += [gn_next, mod3_next, mod3_next]
        out_specs.append(pl.BlockSpec((tm, D_MODEL), row))
        out_shape.append(jax.ShapeDtypeStruct((m, D_MODEL), BF16))
    outs = pl.pallas_call(
        functools.partial(_mlp_kernel, emit_next=emit_next),
        grid=(m // tm, nf + 1),
        in_specs=in_specs,
        out_specs=out_specs,
        out_shape=out_shape,
        scratch_shapes=[pltpu.VMEM((2, tm, tf), BF16)],
        compiler_params=_cparams(("parallel", "arbitrary")),
        name=f"mlp_{m}",
    )(*args)
    return outs if emit_next else (outs[0], None)


def _rope_tables():
    t = jnp.arange(SEQ)
    row = (t // GRID_W).astype(F32)
    col = (t % GRID_W).astype(F32)
    n_freq = HEAD_DIM // 4
    inv = ROPE_BASE ** (-jnp.arange(n_freq, dtype=F32) / n_freq)
    ar = row[:, None] * inv
    ac = col[:, None] * inv
    cos_h = jnp.concatenate([jnp.cos(ar), jnp.cos(ar), jnp.cos(ac), jnp.cos(ac)], axis=-1)
    sin_h = jnp.concatenate([-jnp.sin(ar), jnp.sin(ar), -jnp.sin(ac), jnp.sin(ac)], axis=-1)
    reps = LANES // HEAD_DIM
    cos_t = jnp.concatenate([jnp.tile(cos_h, (1, reps)), jnp.ones((TM_PROJ, LANES), F32)], axis=0)
    sin_t = jnp.concatenate([jnp.tile(sin_h, (1, reps)), jnp.zeros((TM_PROJ, LANES), F32)], axis=0)
    return cos_t, sin_t


def _head_gain_row(qk_gain_l):
    ones = jnp.ones((Z_TILE,), F32)
    rep = lambda g: jnp.tile(g, Z_TILE // HEAD_DIM)
    akv = jnp.concatenate([jnp.tile(qk_gain_l[0, 1], A_KV_HEADS), jnp.ones((Z_TILE - A_KV_HEADS * HEAD_DIM,), F32)])
    return jnp.concatenate([
        rep(qk_gain_l[0, 0]), akv,
        rep(qk_gain_l[1, 0]), rep(qk_gain_l[1, 1]), ones, ones,
        rep(qk_gain_l[2, 0]), rep(qk_gain_l[2, 1]), ones,
    ])[None, :]


def kernel(x, c, ctx, c_ctx, w_ada, b_ada, g_norm1, g_norm2, w_in, b_gate, qk_gain, a_sink, b_rpb,
           c_w, c_scale, d_lambda, d_subln, w_branch, w_out, w_ff1, w_ff2):
    cos_t, sin_t = _rope_tables()
    lane = jnp.arange(HEAD_SUM_LANES)
    bd = (lane[:, None] // HEAD_DIM == lane[None, :] // HEAD_DIM).astype(BF16)

    cc = jnp.concatenate([c, c_ctx[None, :], jnp.zeros((8 - BATCH - 1, D_MODEL), F32)], axis=0)
    mod_all = _modulation(cc, w_ada, b_ada)

    cw = c_w.astype(BF16)
    w_branch2d = w_branch.reshape(DEPTH, N_BRANCH * BRANCH_W, D_MODEL)

    xl = x.reshape(LATENT_ROWS, D_MODEL)
    xc = ctx.reshape(CTX_ROWS, D_MODEL)
    ctx_row = lambda i: BATCH
    batch_row = lambda tm: (lambda i: i // (SEQ // tm))
    mods = [mod_all[l].reshape(8, 1, 6 * D_MODEL) for l in range(DEPTH)]
    gn1 = [g_norm1[l][None, :] for l in range(DEPTH)]

    tm_merge, tm_mlp, tf_mlp, tn_gate = TM_MERGE, TM_MLP, TF_MLP, TN_GATE
    hl = _norm_mod(xl, gn1[0], mods[0], tm=tm_mlp, mod_row=batch_row(tm_mlp))
    hc = _norm_mod(xc, gn1[0], mods[0], tm=tm_mlp, mod_row=ctx_row)

    for l in range(DEPTH):
        last = l == DEPTH - 1
        lam_init = 0.8 - 0.6 * math.exp(-0.3 * l)
        mod3 = mods[l]
        gn2 = g_norm2[l][None, :]
        cs = c_scale[l][None, :]
        subln = d_subln[l][None, :]
        next_norm = None if last else (gn1[l + 1], mods[l + 1])

        gates, w1, w2 = _gate_proj(hl, None if last else hc, w_in, b_gate[l][None, :], w_ff1, w_ff2,
                                   layer=l, tn=tn_gate)
        zh, wb, wo = _head_proj(hl, hc, w_in, _head_gain_row(qk_gain[l]), cos_t, sin_t, bd, w_branch2d, w_out,
                                layer=l, ctx_queries=not last)
        wb = wb.reshape(1, N_BRANCH, BRANCH_W, D_MODEL)

        table = _rpb_table(b_rpb[l])
        ya = _win_gqa(zh, a_sink[l], latent=True)
        yb = _nbr_attn(zh, table)
        yc = _pool(zh, cw, cs, layer=l, n=SEQ, latent=True)
        yd = _diff_attn(zh, d_lambda[l], subln, lam_init=lam_init, latent=True)
        xl, h2 = _merge((ya, yb, yc, yd), gates, wb, wo, xl, gn2, mod3,
                        tm=tm_merge, row0=0, mod_row=batch_row(tm_merge))
        xl, hl = _mlp(h2, xl, mod3, w1, w2, next_norm, tm=tm_mlp, tf=tf_mlp, mod_row=batch_row(tm_mlp))
        if not last:
            yac = _win_gqa(zh, a_sink[l], latent=False)
            ybc = _ctx_mha(zh)
            ycc = _pool(zh, cw, cs, layer=l, n=CTX_LEN, latent=False)
            ydc = _diff_attn(zh, d_lambda[l], subln, lam_init=lam_init, latent=False)
            xc, h2c = _merge((yac, ybc, ycc, ydc), gates, wb, wo, xc, gn2, mod3,
                             tm=tm_merge, row0=LATENT_ROWS, mod_row=ctx_row)
            xc, hc = _mlp(h2c, xc, mod3, w1, w2, next_norm, tm=tm_mlp, tf=tf_mlp, mod_row=ctx_row)

    return xl.reshape(BATCH, SEQ, D_MODEL)
```

```python
import functools
import math

import jax
import jax.numpy as jnp
from jax import lax
from jax.experimental import pallas as pl
from jax.experimental.pallas import tpu as pltpu

D_MODEL = 2048
BATCH = 4
SEQ = 2048
DEPTH = 2
GRID_W = 64
GRID_H = SEQ // GRID_W
CTX_LEN = 256
HEAD_DIM = 64
N_BRANCH = 4
BRANCH_W = D_MODEL // N_BRANCH
A_HEADS = 8
A_KV_HEADS = 2
A_GROUP = A_HEADS // A_KV_HEADS
A_WINDOW = 128
A_BLOCK = 128
B_HEADS = 8
B_WIN_ROWS = 8
B_WIN_COLS = 16
C_WINDOWS = (2, 4, 8, 16)
C_GROUP_DIM = 128
D_HEADS = 4
D_FF = 4 * D_MODEL
ROPE_BASE = 10000.0
EPS = 1e-6
NEG = -1e30
ATTN_SCALE = HEAD_DIM ** -0.5

F32 = jnp.float32
BF16 = jnp.bfloat16

V7X_VMEM_LIMIT_BYTES = 56 * 1024 * 1024
LANES = 128

LATENT_ROWS = BATCH * SEQ
CTX_ROWS = BATCH * CTX_LEN
ALL_ROWS = LATENT_ROWS + CTX_ROWS

Z_TILE = 512
H_AQ, H_AKV, H_BQ, H_BK, H_BV, H_CU, H_DQ, H_DK, H_DV = range(9)
HEAD_TILES = 9
W_SRC = 256
GATE_COLS = N_BRANCH * D_MODEL
GATE_SRC0 = (12544 - GATE_COLS) // W_SRC
HEAD_SUM_LANES = 256

TM_PROJ = 1024
TN_GATE = 1024
TM_MERGE = 256
TM_MLP = 512
TF_MLP = 1024


def _cparams(sem):
    return pltpu.CompilerParams(dimension_semantics=sem, vmem_limit_bytes=V7X_VMEM_LIMIT_BYTES)


def _dot_nt(a, b):
    return lax.dot_general(a, b, (((1,), (1,)), ((), ())), preferred_element_type=F32)


def _dot(a, b):
    return jnp.dot(a, b, preferred_element_type=F32)


def _ctx_blocks(block_rows):
    return LATENT_ROWS // block_rows


def _mod_kernel(c_ref, w_ref, b_ref, o_ref):
    c = c_ref[...]
    s = c * (0.5 * (jnp.tanh(0.5 * c) + 1.0))
    o_ref[0] = _dot(s, w_ref[0]) + b_ref[0]


def _modulation(cc, w_ada, b_ada):
    tn = 1024
    n = 6 * D_MODEL
    return pl.pallas_call(
        _mod_kernel,
        grid=(DEPTH, n // tn),
        in_specs=[
            pl.BlockSpec((8, D_MODEL), lambda l, j: (0, 0)),
            pl.BlockSpec((1, D_MODEL, tn), lambda l, j: (l, 0, j)),
            pl.BlockSpec((1, 1, tn), lambda l, j: (l, 0, j)),
        ],
        out_specs=pl.BlockSpec((1, 8, tn), lambda l, j: (l, 0, j)),
        out_shape=jax.ShapeDtypeStruct((DEPTH, 8, n), F32),
        compiler_params=_cparams(("parallel", "parallel")),
        name="modulation",
    )(cc, w_ada, b_ada.reshape(DEPTH, 1, n))


NORM_ROWS = 16


def _norm_modulate_rows(x_ref, gn_ref, shift_ref, scale_ref, h_ref):
    n = x_ref.shape[0]
    gs = gn_ref[...] * (1.0 + scale_ref[0])
    shift = shift_ref[0]

    def body(c, carry):
        rows = pl.ds(pl.multiple_of(c * NORM_ROWS, NORM_ROWS), NORM_ROWS)
        x = x_ref[rows, :]
        inv = lax.rsqrt(jnp.mean(x * x, axis=-1, keepdims=True) + EPS)
        h_ref[rows, :] = ((x * inv) * gs + shift).astype(BF16)
        return carry

    lax.fori_loop(0, n // NORM_ROWS, body, 0, unroll=8)


def _norm_mod_kernel(x_ref, gn_ref, shift_ref, scale_ref, h_ref):
    _norm_modulate_rows(x_ref, gn_ref, shift_ref, scale_ref, h_ref)


def _norm_mod(x, gn, mod3, *, tm, mod_row):
    m = x.shape[0]
    return pl.pallas_call(
        _norm_mod_kernel,
        grid=(m // tm,),
        in_specs=[
            pl.BlockSpec((tm, D_MODEL), lambda i: (i, 0)),
            pl.BlockSpec((1, D_MODEL), lambda i: (0, 0)),
            pl.BlockSpec((1, 1, D_MODEL), lambda i: (mod_row(i), 0, 0)),
            pl.BlockSpec((1, 1, D_MODEL), lambda i: (mod_row(i), 0, 1)),
        ],
        out_specs=pl.BlockSpec((tm, D_MODEL), lambda i: (i, 0)),
        out_shape=jax.ShapeDtypeStruct((m, D_MODEL), BF16),
        compiler_params=_cparams(("parallel",)),
        name=f"norm_mod_{m}",
    )(x, gn, mod3, mod3)


def _swap16(y):
    lane = lax.broadcasted_iota(jnp.int32, y.shape, 1)
    fwd = pltpu.roll(y, LANES - 16, axis=1)
    bwd = pltpu.roll(y, 16, axis=1)
    return jnp.where((lane & 16) == 0, fwd, bwd)


def _cast_weight_blocks(w_refs, w_scr):
    @pl.when(pl.program_id(1) == 0)
    def _():
        for c, w_ref in enumerate(w_refs):
            w_scr[:, c * W_SRC:(c + 1) * W_SRC] = w_ref[0].astype(BF16)


LATENT_TILES = LATENT_ROWS // TM_PROJ
CAST_CHUNKS = 64


def _row_tile(hl_ref, hc_ref, rows=slice(None)):
    if hc_ref is None:
        return hl_ref[rows, :]
    return jnp.where(pl.program_id(1) >= LATENT_TILES, hc_ref[rows, :], hl_ref[rows, :])


def _h_specs(with_ctx):
    specs = [pl.BlockSpec((TM_PROJ, D_MODEL), lambda n, i: (jnp.minimum(i, LATENT_TILES - 1), 0))]
    if with_ctx:
        specs.append(pl.BlockSpec((TM_PROJ, D_MODEL), lambda n, i: (0, 0), pipeline_mode=pl.Buffered(1)))
    return specs


def _side_cast_specs(arrays, *, layer, row_tiles):
    chunk = lambda n, i: jnp.minimum(n * row_tiles + i, CAST_CHUNKS - 1)
    in_specs, out_specs, out_shape = [], [], []
    for a in arrays:
        rows, cols = a.shape[1] // CAST_CHUNKS, a.shape[2]
        in_specs.append(pl.BlockSpec((1, rows, cols), lambda n, i: (layer, chunk(n, i), 0)))
        out_specs.append(pl.BlockSpec((1, rows, cols), lambda n, i: (0, chunk(n, i), 0)))
        out_shape.append(jax.ShapeDtypeStruct((1,) + a.shape[1:], BF16))
    return in_specs, out_specs, out_shape


def _gate_kernel(*refs, with_ctx):
    hl_ref = refs[0]
    hc_ref = refs[1] if with_ctx else None
    *w_refs, bias_ref, f1_ref, f2_ref, g_ref, f1o_ref, f2o_ref, w_scr = refs[1 + with_ctx:]
    _cast_weight_blocks(w_refs, w_scr)
    f1o_ref[...] = f1_ref[...].astype(BF16)
    f2o_ref[...] = f2_ref[...].astype(BF16)
    g_ref[...] = (_dot(_row_tile(hl_ref, hc_ref), w_scr[...]) + bias_ref[...]).astype(BF16)


def _gate_proj(hl, hc, w_in, bias, w_ff1, w_ff2, *, layer, tn):
    with_ctx = hc is not None
    row_tiles = LATENT_TILES + with_ctx
    n_src = tn // W_SRC
    w_spec = lambda c: pl.BlockSpec((1, D_MODEL, W_SRC), lambda n, i: (layer, 0, GATE_SRC0 + n * n_src + c))
    cast_in, cast_out, cast_shape = _side_cast_specs((w_ff1, w_ff2), layer=layer, row_tiles=row_tiles)
    h_args = (hl, hc) if with_ctx else (hl,)
    return pl.pallas_call(
        functools.partial(_gate_kernel, with_ctx=with_ctx),
        grid=(GATE_COLS // tn, row_tiles),
        in_specs=_h_specs(with_ctx)
        + [w_spec(c) for c in range(n_src)]
        + [pl.BlockSpec((1, tn), lambda n, i: (0, n))]
        + cast_in,
        out_specs=[pl.BlockSpec((TM_PROJ, tn), lambda n, i: (i, n))] + cast_out,
        out_shape=[jax.ShapeDtypeStruct((row_tiles * TM_PROJ, GATE_COLS), BF16)] + cast_shape,
        scratch_shapes=[pltpu.VMEM((D_MODEL, tn), BF16)],
        compiler_params=_cparams(("arbitrary", "arbitrary")),
        name="gate_proj",
    )(*h_args, *([w_in] * n_src), bias, w_ff1, w_ff2)


def _head_src_block(t, half):
    return jnp.where(t == H_AQ, half, jnp.where(t == H_AKV, 2, 2 * t - 1 + half))


def _head_kernel(hl_ref, hc_ref, wa_ref, wb_ref, gain_ref, cos_ref, sin_ref, bd_ref, s1_ref, s2_ref,
                 z_ref, s1o_ref, s2o_ref, w_scr, *, ctx_queries):
    t = pl.program_id(0)
    _cast_weight_blocks((wa_ref, wb_ref), w_scr)
    s1o_ref[...] = s1_ref[...].astype(BF16)
    s2o_ref[...] = s2_ref[...].astype(BF16)
    is_ctx = pl.program_id(1) >= LATENT_TILES
    used = True if ctx_queries else ~is_ctx

    def norm_tile(n_norm_chunks, with_rope):
        wide = bd_ref.shape[0]
        half_rows = TM_PROJ // 2
        for r0 in range(0, TM_PROJ, half_rows):
            rows = slice(r0, r0 + half_rows)
            acc = _dot(_row_tile(hl_ref, hc_ref, rows), w_scr[...])
            for c2 in range(Z_TILE // wide):
                a = acc[:, c2 * wide:(c2 + 1) * wide]
                normed = a
                if c2 * (wide // LANES) < n_norm_chunks:
                    ms = _dot((a * a).astype(BF16), bd_ref[...]) * (1.0 / HEAD_DIM)
                    normed = a * lax.rsqrt(ms + EPS) * gain_ref[:, c2 * wide:(c2 + 1) * wide]
                for c in range(wide // LANES):
                    chunk = c2 * (wide // LANES) + c
                    y = (normed if chunk < n_norm_chunks else a)[:, c * LANES:(c + 1) * LANES]
                    if with_rope and chunk < n_norm_chunks:
                        y = y * cos_ref[rows, :] + _swap16(y) * sin_ref[rows, :]
                    z_ref[rows, chunk * LANES:(chunk + 1) * LANES] = y.astype(BF16)

    is_query = (t == H_AQ) | (t == H_BQ) | (t == H_DQ) | (t == H_CU)

    @pl.when(((t == H_AQ) | (t == H_DQ)) & used | (t == H_DK))
    def _():
        norm_tile(4, True)

    @pl.when(t == H_AKV)
    def _():
        norm_tile(1, True)

    @pl.when((t == H_BQ) & used | (t == H_BK))
    def _():
        norm_tile(4, False)

    @pl.when((t == H_BV) | (t == H_CU) & used | (t == H_DV))
    def _():
        norm_tile(0, False)

    if not ctx_queries:
        @pl.when(is_query & is_ctx)
        def _():
            z_ref[...] = jnp.zeros_like(z_ref)


def _head_proj(hl, hc, w_in, gain, cos_t, sin_t, bd, w_branch, w_out, *, layer, ctx_queries):
    seq_tiles = SEQ // TM_PROJ
    row_tiles = LATENT_TILES + 1
    rope_block = lambda i: jnp.where(i < LATENT_TILES, i % seq_tiles, seq_tiles)
    w_spec = lambda half: pl.BlockSpec((1, D_MODEL, W_SRC), lambda t, i: (layer, 0, _head_src_block(t, half)))
    cast_in, cast_out, cast_shape = _side_cast_specs((w_branch, w_out), layer=layer, row_tiles=row_tiles)
    return pl.pallas_call(
        functools.partial(_head_kernel, ctx_queries=ctx_queries),
        grid=(HEAD_TILES, row_tiles),
        in_specs=_h_specs(True) + [
            w_spec(0), w_spec(1),
            pl.BlockSpec((1, Z_TILE), lambda t, i: (0, t)),
            pl.BlockSpec((TM_PROJ, LANES), lambda t, i: (rope_block(i), 0)),
            pl.BlockSpec((TM_PROJ, LANES), lambda t, i: (rope_block(i), 0)),
            pl.BlockSpec(bd.shape, lambda t, i: (0, 0)),
        ] + cast_in,
        out_specs=[pl.BlockSpec((None, TM_PROJ, Z_TILE), lambda t, i: (t, i, 0))] + cast_out,
        out_shape=[jax.ShapeDtypeStruct((HEAD_TILES, ALL_ROWS, Z_TILE), BF16)] + cast_shape,
        scratch_shapes=[pltpu.VMEM((D_MODEL, Z_TILE), BF16)],
        compiler_params=_cparams(("arbitrary", "arbitrary")),
        name="head_proj",
    )(hl, hc, w_in, w_in, gain, cos_t, sin_t, bd, w_branch, w_out)


def _softmax_attend(q, segs, extra_logit=None):
    scores = []
    for k, _, bias in segs:
        s = _dot_nt(q, k)
        if bias is not None:
            s = s + bias
        scores.append(s)
    m = scores[0].max(axis=-1, keepdims=True)
    for s in scores[1:]:
        m = jnp.maximum(m, s.max(axis=-1, keepdims=True))
    if extra_logit is not None:
        m = jnp.maximum(m, extra_logit)
    l = None
    o = None
    for s, (_, v, _) in zip(scores, segs):
        p = jnp.exp(s - m)
        ls = p.sum(axis=-1, keepdims=True)
        os_ = _dot(p.astype(BF16), v)
        l = ls if l is None else l + ls
        o = os_ if o is None else o + os_
    if extra_logit is not None:
        l = l + jnp.exp(extra_logit - m)
    return o, l


def _lane_group(shape, width):
    return lax.broadcasted_iota(jnp.int32, shape, 1) // width


def _zspec(tile, rows, index_rows, width=Z_TILE):
    return pl.BlockSpec((None, rows, width), lambda *ids: (tile, index_rows(*ids), 0))


def _repeat_kv_heads(kv):
    pieces = [kv[:, i * HEAD_DIM:(i + 1) * HEAD_DIM] for i in range(2 * A_KV_HEADS)]
    k = jnp.concatenate([pieces[g] for g in range(A_KV_HEADS) for _ in range(A_GROUP)], axis=1)
    v = jnp.concatenate([pieces[A_KV_HEADS + g] for g in range(A_KV_HEADS) for _ in range(A_GROUP)], axis=1)
    return k, v


WIN_BLOCKS_PER_STEP = 8


def _win_gqa_kernel(sink_ref, q_ref, *rest, latent):
    if latent:
        kv_ref, ckv_ref, o_ref, k_scr, v_scr, kc_scr, vc_scr = rest
    else:
        ckv_ref, o_ref, kc_scr, vc_scr = rest
    gw = A_GROUP * HEAD_DIM

    def build():
        kc_scr[...], vc_scr[...] = _repeat_kv_heads(ckv_ref[...])
        if latent:
            k_scr[...], v_scr[...] = _repeat_kv_heads(kv_ref[...])

    if latent:
        pl.when(pl.program_id(1) == 0)(build)
    else:
        build()

    tq = A_BLOCK if latent else q_ref.shape[0]
    n_sub = q_ref.shape[0] // tq
    rows = A_GROUP * tq
    row_head = lax.broadcasted_iota(jnp.int32, (rows, 1), 0) // tq
    lane_head = _lane_group((tq, gw), HEAD_DIM)
    for j in range(n_sub):
        q = q_ref[j * tq:(j + 1) * tq, :] * ATTN_SCALE
        if latent:
            n = pl.program_id(1) * n_sub + j
            span = 3 * A_BLOCK
            start = pl.multiple_of(jnp.clip((n - 1) * A_BLOCK, 0, SEQ - span), A_BLOCK)
            qpos = n * A_BLOCK + lax.broadcasted_iota(jnp.int32, (tq, span), 0)
            kpos = start + lax.broadcasted_iota(jnp.int32, (tq, span), 1)
            band = jnp.where(jnp.abs(qpos - kpos) <= A_WINDOW, 0.0, NEG).astype(F32)
            band = jnp.concatenate([band] * A_GROUP, axis=0)
        for g in range(A_KV_HEADS):
            gs = slice(g * gw, (g + 1) * gw)
            qg = q[:, gs]
            zero = jnp.zeros_like(qg)
            q_stack = jnp.concatenate([jnp.where(lane_head == r, qg, zero) for r in range(A_GROUP)], axis=0)
            sink = jnp.zeros((rows, 1), F32)
            for r in range(A_GROUP):
                sink = jnp.where(row_head == r, sink_ref[g * A_GROUP + r], sink)
            segs = [(kc_scr[:, gs], vc_scr[:, gs], None)]
            if latent:
                segs.append((k_scr[pl.ds(start, span), gs], v_scr[pl.ds(start, span), gs], band))
            o, l = _softmax_attend(q_stack, segs, extra_logit=sink)
            o = o / l
            og = jnp.zeros((tq, gw), F32)
            for r in range(A_GROUP):
                og = jnp.where(lane_head == r, o[r * tq:(r + 1) * tq], og)
            o_ref[j * tq:(j + 1) * tq, gs] = og.astype(BF16)


def _win_gqa(zh, sink, *, latent):
    kvb = 2 * A_KV_HEADS * HEAD_DIM
    c0 = _ctx_blocks(CTX_LEN)
    smem = pl.BlockSpec(memory_space=pltpu.SMEM)
    rep = lambda n: pltpu.VMEM((n, A_HEADS * HEAD_DIM), BF16)
    if latent:
        tq = WIN_BLOCKS_PER_STEP * A_BLOCK
        nq = SEQ // tq
        grid = (BATCH, nq)
        in_specs = [
            smem,
            _zspec(H_AQ, tq, lambda b, n: b * nq + n),
            _zspec(H_AKV, SEQ, lambda b, n: b, kvb),
            _zspec(H_AKV, CTX_LEN, lambda b, n: c0 + b, kvb),
        ]
        out_specs = pl.BlockSpec((tq, Z_TILE), lambda b, n: (b * nq + n, 0))
        args = (sink, zh, zh, zh)
        rows = LATENT_ROWS
        sem = ("parallel", "arbitrary")
        scratch = [rep(SEQ), rep(SEQ), rep(CTX_LEN), rep(CTX_LEN)]
    else:
        grid = (BATCH,)
        in_specs = [
            smem,
            _zspec(H_AQ, CTX_LEN, lambda b: c0 + b),
            _zspec(H_AKV, CTX_LEN, lambda b: c0 + b, kvb),
        ]
        out_specs = pl.BlockSpec((CTX_LEN, Z_TILE), lambda b: (b, 0))
        args = (sink, zh, zh)
        rows = CTX_ROWS
        sem = ("parallel",)
        scratch = [rep(CTX_LEN), rep(CTX_LEN)]
    return pl.pallas_call(
        functools.partial(_win_gqa_kernel, latent=latent),
        grid=grid,
        in_specs=in_specs,
        out_specs=out_specs,
        out_shape=jax.ShapeDtypeStruct((rows, BRANCH_W), BF16),
        scratch_shapes=scratch,
        compiler_params=_cparams(sem),
        name="win_gqa" if latent else "win_gqa_ctx",
    )(*args)


N_ROW_OFFSETS = 2 * B_WIN_ROWS - 1
N_COL_OFFSETS = 2 * B_WIN_COLS - 1
N_PAIR = N_ROW_OFFSETS + 1
NBR_QROWS = 4
NBR_SLAB = NBR_QROWS + B_WIN_ROWS
NBR_GROUPS = GRID_H // NBR_QROWS
NBR_CASES = 3


def _nbr_case_geometry(case):
    if case == 0:
        return B_WIN_ROWS - 1, lambda a, c: 0 <= c < B_WIN_ROWS
    if case == 1:
        return B_WIN_ROWS - 1 - NBR_QROWS, lambda a, c: 0 <= c - a < B_WIN_ROWS
    return B_WIN_ROWS - 1 - 2 * NBR_QROWS, lambda a, c: NBR_QROWS <= c < NBR_SLAB


def _rpb_table_kernel(rpb_ref, o_ref, pair_scr):
    h = pl.program_id(0)
    shape = (GRID_W, LANES)
    qcol = lax.broadcasted_iota(jnp.int32, shape, 0)
    lane = lax.broadcasted_iota(jnp.int32, shape, 1)
    kcol = lane & (GRID_W - 1)
    hi = lane >= GRID_W
    cs = jnp.clip(qcol - B_WIN_COLS // 2, 0, GRID_W - B_WIN_COLS)
    in_window = (kcol >= cs) & (kcol < cs + B_WIN_COLS)
    ci = kcol - qcol + (B_WIN_COLS - 1)
    neg = jnp.full(shape, NEG, F32)

    def rpb(d, c):
        return rpb_ref[(h * N_ROW_OFFSETS + d) * N_COL_OFFSETS + c]

    tables = []
    for d in range(N_ROW_OFFSETS):
        acc = neg
        for c in range(N_COL_OFFSETS):
            acc = jnp.where(ci == c, rpb(d, c), acc)
        tables.append(jnp.where(in_window, acc, NEG))
    for p in range(N_PAIR):
        lo = tables[p - 1] if p >= 1 else neg
        up = tables[p] if p < N_ROW_OFFSETS else neg
        pair_scr[p] = jnp.where(hi, up, lo)

    for case in range(NBR_CASES):
        shift, in_rows = _nbr_case_geometry(case)
        for a in range(NBR_QROWS):
            for cp in range(NBR_SLAB // 2):
                c = 2 * cp
                lo_ok, hi_ok = in_rows(a, c), in_rows(a, c + 1)
                if lo_ok or hi_ok:
                    piece = pair_scr[c - a + shift + 1]
                    if not lo_ok:
                        piece = jnp.where(hi, piece, neg)
                    if not hi_ok:
                        piece = jnp.where(hi, neg, piece)
                else:
                    piece = neg
                o_ref[case, 0, a * GRID_W:(a + 1) * GRID_W, cp * LANES:(cp + 1) * LANES] = piece


def _rpb_table(rpb):
    nq, nk = NBR_QROWS * GRID_W, NBR_SLAB * GRID_W
    return pl.pallas_call(
        _rpb_table_kernel,
        grid=(B_HEADS,),
        in_specs=[pl.BlockSpec(memory_space=pltpu.SMEM)],
        out_specs=pl.BlockSpec((NBR_CASES, 1, nq, nk), lambda h: (0, h, 0, 0)),
        out_shape=jax.ShapeDtypeStruct((NBR_CASES, B_HEADS, nq, nk), F32),
        scratch_shapes=[pltpu.VMEM((N_PAIR, GRID_W, LANES), F32)],
        compiler_params=_cparams(("parallel",)),
        name="rpb_table",
    )(rpb.reshape(-1))


def _pair_attend(q, lane_hi, segs):
    m = q.shape[0]
    zero = jnp.zeros_like(q)
    q_stack = jnp.concatenate([jnp.where(lane_hi, zero, q), jnp.where(lane_hi, q, zero)], axis=0)
    o, l = _softmax_attend(q_stack, segs)
    o = o / l
    return jnp.where(lane_hi, o[m:], o[:m])


NBR_GROUPS_PER_STEP = 4


def _nbr_kernel(q_ref, k_ref, v_ref, kc_ref, vc_ref, bias_ref, o_ref):
    nq = NBR_QROWS * GRID_W
    nk = NBR_SLAB * GRID_W
    lane_hi = lax.broadcasted_iota(jnp.int32, (nq, LANES), 1) >= HEAD_DIM
    for j in range(NBR_GROUPS_PER_STEP):
        g = pl.program_id(1) * NBR_GROUPS_PER_STEP + j
        case = jnp.where(g == 0, 0, jnp.where(g == NBR_GROUPS - 1, 2, 1))
        r0 = jnp.clip(NBR_QROWS * g - B_WIN_ROWS // 2, 0, GRID_H - NBR_SLAB)
        base = pl.multiple_of(r0 * GRID_W, GRID_W)
        rows = slice(j * nq, (j + 1) * nq)
        for pr in range(B_HEADS // 2):
            sl = slice(pr * LANES, (pr + 1) * LANES)
            kp = k_ref[pl.ds(base, nk), sl]
            vp = v_ref[pl.ds(base, nk), sl]
            kcp = kc_ref[:, sl]
            vcp = vc_ref[:, sl]
            bias = bias_ref[case, 2 * pr:2 * pr + 2].reshape(2 * nq, nk)
            segs = [(kp, vp, bias), (kcp, vcp, None)]
            q = q_ref[rows, sl] * ATTN_SCALE
            o_ref[rows, sl] = _pair_attend(q, lane_hi, segs).astype(BF16)


def _nbr_attn(zh, table):
    nq = NBR_GROUPS_PER_STEP * NBR_QROWS * GRID_W
    steps = NBR_GROUPS // NBR_GROUPS_PER_STEP
    c0 = _ctx_blocks(CTX_LEN)
    return pl.pallas_call(
        _nbr_kernel,
        grid=(BATCH, steps),
        in_specs=[
            _zspec(H_BQ, nq, lambda b, s: b * steps + s),
            _zspec(H_BK, SEQ, lambda b, s: b),
            _zspec(H_BV, SEQ, lambda b, s: b),
            _zspec(H_BK, CTX_LEN, lambda b, s: c0 + b),
            _zspec(H_BV, CTX_LEN, lambda b, s: c0 + b),
            pl.BlockSpec(table.shape, lambda b, s: (0, 0, 0, 0), pipeline_mode=pl.Buffered(1)),
        ],
        out_specs=pl.BlockSpec((nq, Z_TILE), lambda b, s: (b * steps + s, 0)),
        out_shape=jax.ShapeDtypeStruct((LATENT_ROWS, BRANCH_W), BF16),
        compiler_params=_cparams(("parallel", "arbitrary")),
        name="nbr_attn",
    )(zh, zh, zh, zh, zh, table)


def _ctx_mha_kernel(q_ref, k_ref, v_ref, o_ref):
    q = q_ref[...] * ATTN_SCALE
    lane_hi = lax.broadcasted_iota(jnp.int32, (q.shape[0], LANES), 1) >= HEAD_DIM
    for pr in range(B_HEADS // 2):
        sl = slice(pr * LANES, (pr + 1) * LANES)
        segs = [(k_ref[:, sl], v_ref[:, sl], None)]
        o_ref[:, sl] = _pair_attend(q[:, sl], lane_hi, segs).astype(BF16)


def _ctx_mha(zh):
    c0 = _ctx_blocks(CTX_LEN)
    rows = lambda b: c0 + b
    return pl.pallas_call(
        _ctx_mha_kernel,
        grid=(BATCH,),
        in_specs=[_zspec(H_BQ, CTX_LEN, rows), _zspec(H_BK, CTX_LEN, rows), _zspec(H_BV, CTX_LEN, rows)],
        out_specs=pl.BlockSpec((CTX_LEN, Z_TILE), lambda b: (b, 0)),
        out_shape=jax.ShapeDtypeStruct((CTX_ROWS, BRANCH_W), BF16),
        compiler_params=_cparams(("parallel",)),
        name="nbr_attn_ctx",
    )(zh, zh, zh)


def _pool_kernel(u_ref, w_ref, scale_ref, o_ref):
    n = u_ref.shape[0]
    t = lax.broadcasted_iota(jnp.int32, (n, C_GROUP_DIM), 0)

    def down(a, k):
        return jnp.where(t >= k, pltpu.roll(a, k, axis=0), 0.0)

    def up(a, k):
        return jnp.where(t < n - k, pltpu.roll(a, n - k, axis=0), 0.0)

    for g, w in enumerate(C_WINDOWS):
        sl = slice(g * C_GROUP_DIM, (g + 1) * C_GROUP_DIM)
        u = u_ref[:, sl].astype(F32)
        half = w // 2
        back = u
        fwd = u
        k = 1
        while k < half:
            back = back + down(back, k)
            fwd = fwd + up(fwd, k)
            k *= 2
        total = down(back, 1) + fwd
        lo = jnp.maximum(t - half, 0)
        hi = jnp.minimum(t + half - 1, n - 1)
        cnt = (hi - lo + 1).astype(F32)
        pooled = total / cnt - u
        y = _dot(pooled.astype(BF16), w_ref[0, g]) * scale_ref[:, sl]
        o_ref[:, sl] = y.astype(BF16)


def _pool(zh, c_w, c_scale, *, layer, n, latent):
    rows = LATENT_ROWS if latent else CTX_ROWS
    b0 = 0 if latent else _ctx_blocks(n)
    return pl.pallas_call(
        _pool_kernel,
        grid=(rows // n,),
        in_specs=[
            _zspec(H_CU, n, lambda b: b0 + b),
            pl.BlockSpec((1, len(C_WINDOWS), C_GROUP_DIM, C_GROUP_DIM), lambda b: (layer, 0, 0, 0)),
            pl.BlockSpec((1, BRANCH_W), lambda b: (0, 0)),
        ],
        out_specs=pl.BlockSpec((n, BRANCH_W), lambda b: (b, 0)),
        out_shape=jax.ShapeDtypeStruct((rows, BRANCH_W), BF16),
        compiler_params=_cparams(("parallel",)),
        name=f"pool_{n}",
    )(zh, c_w, c_scale)


DIFF_ROWS = 256
DIFF_BLOCKS_PER_STEP = 2


def _diff_kernel(lam_ref, q_ref, *rest, lam_init, latent):
    if latent:
        k_ref, v_ref, kc_ref, vc_ref, g_ref, o_ref = rest
    else:
        kc_ref, vc_ref, g_ref, o_ref = rest
    hw = 2 * HEAD_DIM
    dl = lam_ref[...]
    lam = (jnp.exp(jnp.sum(dl[0:1] * dl[1:2], axis=-1, keepdims=True))
           - jnp.exp(jnp.sum(dl[2:3] * dl[3:4], axis=-1, keepdims=True)) + lam_init)
    tq = min(q_ref.shape[0], DIFF_ROWS)
    lane_hi = lax.broadcasted_iota(jnp.int32, (tq, hw), 1) >= HEAD_DIM
    for r0 in range(0, q_ref.shape[0], tq):
        rows = slice(r0, r0 + tq)
        for h in range(D_HEADS):
            sl = slice(h * hw, (h + 1) * hw)
            qh = q_ref[rows, sl] * ATTN_SCALE
            zero = jnp.zeros_like(qh)
            q_stack = jnp.concatenate([jnp.where(lane_hi, zero, qh), jnp.where(lane_hi, qh, zero)], axis=0)
            segs = [(kc_ref[:, sl], vc_ref[:, sl], None)]
            if latent:
                segs.append((k_ref[:, sl], v_ref[:, sl], None))
            o, l = _softmax_attend(q_stack, segs)
            o = o / l
            od = o[:tq] - lam * o[tq:]
            ms = jnp.mean(od * od, axis=-1, keepdims=True)
            y = od * lax.rsqrt(ms + EPS) * g_ref[...] * (1.0 - lam_init)
            o_ref[rows, sl] = y.astype(BF16)


def _diff_attn(zh, d_lambda, subln, *, lam_init, latent):
    full = lambda shape: pl.BlockSpec(shape, lambda *_: (0,) * len(shape))
    c0 = _ctx_blocks(CTX_LEN)
    if latent:
        tq = DIFF_BLOCKS_PER_STEP * DIFF_ROWS
        nq = SEQ // tq
        grid = (BATCH, nq)
        in_specs = [
            full((4, HEAD_DIM)),
            _zspec(H_DQ, tq, lambda b, n: b * nq + n),
            _zspec(H_DK, SEQ, lambda b, n: b),
            _zspec(H_DV, SEQ, lambda b, n: b),
            _zspec(H_DK, CTX_LEN, lambda b, n: c0 + b),
            _zspec(H_DV, CTX_LEN, lambda b, n: c0 + b),
            full((1, 2 * HEAD_DIM)),
        ]
        out_specs = pl.BlockSpec((tq, Z_TILE), lambda b, n: (b * nq + n, 0))
        args = (d_lambda, zh, zh, zh, zh, zh, subln)
        rows = LATENT_ROWS
        sem = ("parallel", "arbitrary")
    else:
        grid = (BATCH,)
        crow = lambda b: c0 + b
        in_specs = [
            full((4, HEAD_DIM)),
            _zspec(H_DQ, CTX_LEN, crow),
            _zspec(H_DK, CTX_LEN, crow),
            _zspec(H_DV, CTX_LEN, crow),
            full((1, 2 * HEAD_DIM)),
        ]
        out_specs = pl.BlockSpec((CTX_LEN, Z_TILE), lambda b: (b, 0))
        args = (d_lambda, zh, zh, zh, subln)
        rows = CTX_ROWS
        sem = ("parallel",)
    return pl.pallas_call(
        functools.partial(_diff_kernel, lam_init=lam_init, latent=latent),
        grid=grid,
        in_specs=in_specs,
        out_specs=out_specs,
        out_shape=jax.ShapeDtypeStruct((rows, BRANCH_W), BF16),
        compiler_params=_cparams(sem),
        name="diff_attn" if latent else "diff_attn_ctx",
    )(*args)


def _merge_kernel(*refs):
    y_refs = refs[:N_BRANCH]
    g_refs = refs[N_BRANCH:2 * N_BRANCH]
    wb_ref, wo_ref, x_ref, gt_ref, gn_ref, shift_ref, scale_ref, xo_ref, h_ref = refs[2 * N_BRANCH:]
    mixed = None
    for k in range(N_BRANCH):
        gate = 0.5 * (jnp.tanh(0.5 * g_refs[k][...].astype(F32)) + 1.0)
        proj = _dot(y_refs[k][...], wb_ref[0, k]) * gate
        mixed = proj if mixed is None else mixed + proj
    out = _dot(mixed.astype(BF16), wo_ref[0])
    xo_ref[...] = x_ref[...] + gt_ref[0] * out
    _norm_modulate_rows(xo_ref, gn_ref, shift_ref, scale_ref, h_ref)


def _merge(ys, gates, w_branch, w_out, x, gn2, mod3, *, tm, row0, mod_row):
    m = x.shape[0]
    row = lambda i: (i, 0)
    g0 = row0 // tm
    mod = lambda chunk: pl.BlockSpec((1, 1, D_MODEL), lambda i: (mod_row(i), 0, chunk))
    resident = pl.Buffered(1)
    return pl.pallas_call(
        _merge_kernel,
        grid=(m // tm,),
        in_specs=[pl.BlockSpec((tm, BRANCH_W), row)] * N_BRANCH
        + [pl.BlockSpec((tm, D_MODEL), lambda i, k=k: (g0 + i, k)) for k in range(N_BRANCH)]
        + [
            pl.BlockSpec((1, N_BRANCH, BRANCH_W, D_MODEL), lambda i: (0, 0, 0, 0), pipeline_mode=resident),
            pl.BlockSpec((1, D_MODEL, D_MODEL), lambda i: (0, 0, 0), pipeline_mode=resident),
            pl.BlockSpec((tm, D_MODEL), row),
            mod(2),
            pl.BlockSpec((1, D_MODEL), lambda i: (0, 0)),
            mod(3),
            mod(4),
        ],
        out_specs=[pl.BlockSpec((tm, D_MODEL), row), pl.BlockSpec((tm, D_MODEL), row)],
        out_shape=[jax.ShapeDtypeStruct((m, D_MODEL), F32), jax.ShapeDtypeStruct((m, D_MODEL), BF16)],
        compiler_params=_cparams(("parallel",)),
        name=f"merge_{m}",
    )(*ys, *([gates] * N_BRANCH), w_branch, w_out, x, mod3, gn2, mod3, mod3)


def _mlp_kernel(h_ref, x_ref, gt_ref, w1_ref, w2_ref, *rest, emit_next):
    if emit_next:
        gn_ref, shift_ref, scale_ref, xo_ref, hn_ref = rest
    else:
        (xo_ref,) = rest
    f = pl.program_id(1)

    @pl.when(f == 0)
    def _():
        xo_ref[...] = jnp.zeros_like(xo_ref)

    a = jnp.maximum(_dot(h_ref[...], w1_ref[0]), 0.0)
    xo_ref[...] += _dot((a * a).astype(BF16), w2_ref[0])

    @pl.when(f == pl.num_programs(1) - 1)
    def _():
        xo_ref[...] = x_ref[...] + gt_ref[0] * xo_ref[...]
        if emit_next:
            _norm_modulate_rows(xo_ref, gn_ref, shift_ref, scale_ref, hn_ref)


def _mlp(h, x, mod3, w1, w2, next_norm, *, tm, tf, mod_row):
    m = x.shape[0]
    emit_next = next_norm is not None
    row = lambda i, f: (i, 0)
    in_specs = [
        pl.BlockSpec((tm, D_MODEL), row),
        pl.BlockSpec((tm, D_MODEL), row),
        pl.BlockSpec((1, 1, D_MODEL), lambda i, f: (mod_row(i), 0, 5)),
        pl.BlockSpec((1, D_MODEL, tf), lambda i, f: (0, 0, f)),
        pl.BlockSpec((1, tf, D_MODEL), lambda i, f: (0, f, 0)),
    ]
    args = [h, x, mod3, w1, w2]
    out_specs = [pl.BlockSpec((tm, D_MODEL), row)]
    out_shape = [jax.ShapeDtypeStruct((m, D_MODEL), F32)]
    if emit_next:
        gn_next, mod3_next = next_norm
        in_specs += [
            pl.BlockSpec((1, D_MODEL), lambda i, f: (0, 0)),
            pl.BlockSpec((1, 1, D_MODEL), lambda i, f: (mod_row(i), 0, 0)),
            pl.BlockSpec((1, 1, D_MODEL), lambda i, f: (mod_row(i), 0, 1)),
        ]
        args += [gn_next, mod3_next, mod3_next]
        out_specs.append(pl.BlockSpec((tm, D_MODEL), row))
        out_shape.append(jax.ShapeDtypeStruct((m, D_MODEL), BF16))
    outs = pl.pallas_call(
        functools.partial(_mlp_kernel, emit_next=emit_next),
        grid=(m // tm, D_FF // tf),
        in_specs=in_specs,
        out_specs=out_specs,
        out_shape=out_shape,
        compiler_params=_cparams(("parallel", "arbitrary")),
        name=f"mlp_{m}",
    )(*args)
    return outs if emit_next else (outs[0], None)


def _rope_tables():
    t = jnp.arange(SEQ)
    row = (t // GRID_W).astype(F32)
    col = (t % GRID_W).astype(F32)
    n_freq = HEAD_DIM // 4
    inv = ROPE_BASE ** (-jnp.arange(n_freq, dtype=F32) / n_freq)
    ar = row[:, None] * inv
    ac = col[:, None] * inv
    cos_h = jnp.concatenate([jnp.cos(ar), jnp.cos(ar), jnp.cos(ac), jnp.cos(ac)], axis=-1)
    sin_h = jnp.concatenate([-jnp.sin(ar), jnp.sin(ar), -jnp.sin(ac), jnp.sin(ac)], axis=-1)
    reps = LANES // HEAD_DIM
    cos_t = jnp.concatenate([jnp.tile(cos_h, (1, reps)), jnp.ones((TM_PROJ, LANES), F32)], axis=0)
    sin_t = jnp.concatenate([jnp.tile(sin_h, (1, reps)), jnp.zeros((TM_PROJ, LANES), F32)], axis=0)
    return cos_t, sin_t


def _head_gain_row(qk_gain_l):
    ones = jnp.ones((Z_TILE,), F32)
    rep = lambda g: jnp.tile(g, Z_TILE // HEAD_DIM)
    akv = jnp.concatenate([jnp.tile(qk_gain_l[0, 1], A_KV_HEADS), jnp.ones((Z_TILE - A_KV_HEADS * HEAD_DIM,), F32)])
    return jnp.concatenate([
        rep(qk_gain_l[0, 0]), akv,
        rep(qk_gain_l[1, 0]), rep(qk_gain_l[1, 1]), ones, ones,
        rep(qk_gain_l[2, 0]), rep(qk_gain_l[2, 1]), ones,
    ])[None, :]


def kernel(x, c, ctx, c_ctx, w_ada, b_ada, g_norm1, g_norm2, w_in, b_gate, qk_gain, a_sink, b_rpb,
           c_w, c_scale, d_lambda, d_subln, w_branch, w_out, w_ff1, w_ff2):
    cos_t, sin_t = _rope_tables()
    lane = jnp.arange(HEAD_SUM_LANES)
    bd = (lane[:, None] // HEAD_DIM == lane[None, :] // HEAD_DIM).astype(BF16)

    cc = jnp.concatenate([c, c_ctx[None, :], jnp.zeros((8 - BATCH - 1, D_MODEL), F32)], axis=0)
    mod_all = _modulation(cc, w_ada, b_ada)

    cw = c_w.astype(BF16)
    w_branch2d = w_branch.reshape(DEPTH, N_BRANCH * BRANCH_W, D_MODEL)

    xl = x.reshape(LATENT_ROWS, D_MODEL)
    xc = ctx.reshape(CTX_ROWS, D_MODEL)
    ctx_row = lambda i: BATCH
    batch_row = lambda tm: (lambda i: i // (SEQ // tm))
    mods = [mod_all[l].reshape(8, 1, 6 * D_MODEL) for l in range(DEPTH)]
    gn1 = [g_norm1[l][None, :] for l in range(DEPTH)]

    tm_merge, tm_mlp, tf_mlp, tn_gate = TM_MERGE, TM_MLP, TF_MLP, TN_GATE
    hl = _norm_mod(xl, gn1[0], mods[0], tm=tm_mlp, mod_row=batch_row(tm_mlp))
    hc = _norm_mod(xc, gn1[0], mods[0], tm=tm_mlp, mod_row=ctx_row)

    for l in range(DEPTH):
        last = l == DEPTH - 1
        lam_init = 0.8 - 0.6 * math.exp(-0.3 * l)
        mod3 = mods[l]
        gn2 = g_norm2[l][None, :]
        cs = c_scale[l][None, :]
        subln = d_subln[l][None, :]
        next_norm = None if last else (gn1[l + 1], mods[l + 1])

        gates, w1, w2 = _gate_proj(hl, None if last else hc, w_in, b_gate[l][None, :], w_ff1, w_ff2,
                                   layer=l, tn=tn_gate)
        zh, wb, wo = _head_proj(hl, hc, w_in, _head_gain_row(qk_gain[l]), cos_t, sin_t, bd, w_branch2d, w_out,
                                layer=l, ctx_queries=not last)
        wb = wb.reshape(1, N_BRANCH, BRANCH_W, D_MODEL)

        table = _rpb_table(b_rpb[l])
        ya = _win_gqa(zh, a_sink[l], latent=True)
        yb = _nbr_attn(zh, table)
        yc = _pool(zh, cw, cs, layer=l, n=SEQ, latent=True)
        yd = _diff_attn(zh, d_lambda[l], subln, lam_init=lam_init, latent=True)
        xl, h2 = _merge((ya, yb, yc, yd), gates, wb, wo, xl, gn2, mod3,
                        tm=tm_merge, row0=0, mod_row=batch_row(tm_merge))
        xl, hl = _mlp(h2, xl, mod3, w1, w2, next_norm, tm=tm_mlp, tf=tf_mlp, mod_row=batch_row(tm_mlp))
        if not last:
            yac = _win_gqa(zh, a_sink[l], latent=False)
            ybc = _ctx_mha(zh)
            ycc = _pool(zh, cw, cs, layer=l, n=CTX_LEN, latent=False)
            ydc = _diff_attn(zh, d_lambda[l], subln, lam_init=lam_init, latent=False)
            xc, h2c = _merge((yac, ybc, ycc, ydc), gates, wb, wo, xc, gn2, mod3,
                             tm=tm_merge, row0=LATENT_ROWS, mod_row=ctx_row)
            xc, hc = _mlp(h2c, xc, mod3, w1, w2, next_norm, tm=tm_mlp, tf=tf_mlp, mod_row=ctx_row)

    return xl.reshape(BATCH, SEQ, D_MODEL)
```

```python
import functools
import math

import jax
import jax.numpy as jnp
from jax import lax
from jax.experimental import pallas as pl
from jax.experimental.pallas import tpu as pltpu

D_MODEL = 2048
BATCH = 4
SEQ = 2048
DEPTH = 2
GRID_W = 64
GRID_H = SEQ // GRID_W
CTX_LEN = 256
HEAD_DIM = 64
N_BRANCH = 4
BRANCH_W = D_MODEL // N_BRANCH
A_HEADS = 8
A_KV_HEADS = 2
A_GROUP = A_HEADS // A_KV_HEADS
A_WINDOW = 128
A_BLOCK = 128
B_HEADS = 8
B_WIN_ROWS = 8
B_WIN_COLS = 16
C_WINDOWS = (2, 4, 8, 16)
C_GROUP_DIM = 128
D_HEADS = 4
D_FF = 4 * D_MODEL
ROPE_BASE = 10000.0
EPS = 1e-6
NEG = -1e30
ATTN_SCALE = HEAD_DIM ** -0.5

F32 = jnp.float32
BF16 = jnp.bfloat16

V7X_VMEM_LIMIT_BYTES = 56 * 1024 * 1024
LANES = 128

LATENT_ROWS = BATCH * SEQ
CTX_ROWS = BATCH * CTX_LEN
ALL_ROWS = LATENT_ROWS + CTX_ROWS

Z_TILE = 512
H_AQ, H_AKV, H_BQ, H_BK, H_BV, H_CU, H_DQ, H_DK, H_DV = range(9)
HEAD_TILES = 9
W_SRC = 256
GATE_COLS = N_BRANCH * D_MODEL
HEAD_COLS = (A_HEADS + 2 * A_KV_HEADS + 3 * B_HEADS + 3 * 2 * D_HEADS) * HEAD_DIM + BRANCH_W
GATE_SRC0 = HEAD_COLS // W_SRC
HEAD_SUM_LANES = 256

TM_PROJ = 1024
TN_GATE = 1024
TM_MERGE = 256
TM_MLP = 512
TF_MLP = 1024


def _cparams(sem):
    return pltpu.CompilerParams(dimension_semantics=sem, vmem_limit_bytes=V7X_VMEM_LIMIT_BYTES)


def _dot_nt(a, b):
    return lax.dot_general(a, b, (((1,), (1,)), ((), ())), preferred_element_type=F32)


def _dot(a, b):
    return jnp.dot(a, b, preferred_element_type=F32)


def _ctx_blocks(block_rows):
    return LATENT_ROWS // block_rows


def _mod_kernel(c_ref, w_ref, b_ref, o_ref):
    c = c_ref[...]
    s = c * (0.5 * (jnp.tanh(0.5 * c) + 1.0))
    o_ref[0] = _dot(s, w_ref[0]) + b_ref[0]


def _modulation(cc, w_ada, b_ada):
    tn = 1024
    n = 6 * D_MODEL
    return pl.pallas_call(
        _mod_kernel,
        grid=(DEPTH, n // tn),
        in_specs=[
            pl.BlockSpec((8, D_MODEL), lambda l, j: (0, 0)),
            pl.BlockSpec((1, D_MODEL, tn), lambda l, j: (l, 0, j)),
            pl.BlockSpec((1, 1, tn), lambda l, j: (l, 0, j)),
        ],
        out_specs=pl.BlockSpec((1, 8, tn), lambda l, j: (l, 0, j)),
        out_shape=jax.ShapeDtypeStruct((DEPTH, 8, n), F32),
        compiler_params=_cparams(("parallel", "parallel")),
        name="modulation",
    )(cc, w_ada, b_ada.reshape(DEPTH, 1, n))


NORM_ROWS = 16


def _norm_modulate_rows(x_ref, gn_ref, shift_ref, scale_ref, h_ref):
    n = x_ref.shape[0]
    gs = gn_ref[...] * (1.0 + scale_ref[0])
    shift = shift_ref[0]

    def body(c, carry):
        rows = pl.ds(pl.multiple_of(c * NORM_ROWS, NORM_ROWS), NORM_ROWS)
        x = x_ref[rows, :]
        inv = lax.rsqrt(jnp.mean(x * x, axis=-1, keepdims=True) + EPS)
        h_ref[rows, :] = ((x * inv) * gs + shift).astype(BF16)
        return carry

    lax.fori_loop(0, n // NORM_ROWS, body, 0, unroll=8)


def _norm_mod_kernel(x_ref, gn_ref, shift_ref, scale_ref, h_ref):
    _norm_modulate_rows(x_ref, gn_ref, shift_ref, scale_ref, h_ref)


def _norm_mod(x, gn, mod3, *, tm, mod_row):
    m = x.shape[0]
    return pl.pallas_call(
        _norm_mod_kernel,
        grid=(m // tm,),
        in_specs=[
            pl.BlockSpec((tm, D_MODEL), lambda i: (i, 0)),
            pl.BlockSpec((1, D_MODEL), lambda i: (0, 0)),
            pl.BlockSpec((1, 1, D_MODEL), lambda i: (mod_row(i), 0, 0)),
            pl.BlockSpec((1, 1, D_MODEL), lambda i: (mod_row(i), 0, 1)),
        ],
        out_specs=pl.BlockSpec((tm, D_MODEL), lambda i: (i, 0)),
        out_shape=jax.ShapeDtypeStruct((m, D_MODEL), BF16),
        compiler_params=_cparams(("parallel",)),
        name=f"norm_mod_{m}",
    )(x, gn, mod3, mod3)


def _swap16(y):
    lane = lax.broadcasted_iota(jnp.int32, y.shape, 1)
    fwd = pltpu.roll(y, LANES - 16, axis=1)
    bwd = pltpu.roll(y, 16, axis=1)
    return jnp.where((lane & 16) == 0, fwd, bwd)


def _cast_weight_blocks(w_refs, w_scr):
    @pl.when(pl.program_id(1) == 0)
    def _():
        for c, w_ref in enumerate(w_refs):
            w_scr[:, c * W_SRC:(c + 1) * W_SRC] = w_ref[0].astype(BF16)


LATENT_TILES = LATENT_ROWS // TM_PROJ
CAST_CHUNKS = 64


def _row_tile(hl_ref, hc_ref, rows=slice(None)):
    if hc_ref is None:
        return hl_ref[rows, :]
    return jnp.where(pl.program_id(1) >= LATENT_TILES, hc_ref[rows, :], hl_ref[rows, :])


def _h_specs(with_ctx):
    specs = [pl.BlockSpec((TM_PROJ, D_MODEL), lambda n, i: (jnp.minimum(i, LATENT_TILES - 1), 0))]
    if with_ctx:
        specs.append(pl.BlockSpec((TM_PROJ, D_MODEL), lambda n, i: (0, 0), pipeline_mode=pl.Buffered(1)))
    return specs


def _side_cast_specs(arrays, *, layer, row_tiles):
    chunk = lambda n, i: jnp.minimum(n * row_tiles + i, CAST_CHUNKS - 1)
    in_specs, out_specs, out_shape = [], [], []
    for a in arrays:
        rows, cols = a.shape[1] // CAST_CHUNKS, a.shape[2]
        in_specs.append(pl.BlockSpec((1, rows, cols), lambda n, i: (layer, chunk(n, i), 0)))
        out_specs.append(pl.BlockSpec((1, rows, cols), lambda n, i: (0, chunk(n, i), 0)))
        out_shape.append(jax.ShapeDtypeStruct((1,) + a.shape[1:], BF16))
    return in_specs, out_specs, out_shape


def _gate_kernel(*refs, with_ctx):
    hl_ref = refs[0]
    hc_ref = refs[1] if with_ctx else None
    *w_refs, bias_ref, f1_ref, f2_ref, g_ref, f1o_ref, f2o_ref, w_scr = refs[1 + with_ctx:]
    _cast_weight_blocks(w_refs, w_scr)
    f1o_ref[...] = f1_ref[...].astype(BF16)
    f2o_ref[...] = f2_ref[...].astype(BF16)
    g_ref[...] = (_dot(_row_tile(hl_ref, hc_ref), w_scr[...]) + bias_ref[...]).astype(BF16)


def _gate_proj(hl, hc, w_in, bias, w_ff1, w_ff2, *, layer, tn):
    with_ctx = hc is not None
    row_tiles = LATENT_TILES + with_ctx
    n_src = tn // W_SRC
    w_spec = lambda c: pl.BlockSpec((1, D_MODEL, W_SRC), lambda n, i: (layer, 0, GATE_SRC0 + n * n_src + c))
    cast_in, cast_out, cast_shape = _side_cast_specs((w_ff1, w_ff2), layer=layer, row_tiles=row_tiles)
    h_args = (hl, hc) if with_ctx else (hl,)
    return pl.pallas_call(
        functools.partial(_gate_kernel, with_ctx=with_ctx),
        grid=(GATE_COLS // tn, row_tiles),
        in_specs=_h_specs(with_ctx)
        + [w_spec(c) for c in range(n_src)]
        + [pl.BlockSpec((1, tn), lambda n, i: (0, n))]
        + cast_in,
        out_specs=[pl.BlockSpec((TM_PROJ, tn), lambda n, i: (i, n))] + cast_out,
        out_shape=[jax.ShapeDtypeStruct((row_tiles * TM_PROJ, GATE_COLS), BF16)] + cast_shape,
        scratch_shapes=[pltpu.VMEM((D_MODEL, tn), BF16)],
        compiler_params=_cparams(("arbitrary", "arbitrary")),
        name="gate_proj",
    )(*h_args, *([w_in] * n_src), bias, w_ff1, w_ff2)


def _head_src_block(t, half):
    return jnp.where(t == H_AQ, half, jnp.where(t == H_AKV, 2, 2 * t - 1 + half))


def _head_kernel(hl_ref, hc_ref, wa_ref, wb_ref, gain_ref, cos_ref, sin_ref, bd_ref, s1_ref, s2_ref,
                 z_ref, s1o_ref, s2o_ref, w_scr, *, ctx_queries):
    t = pl.program_id(0)
    _cast_weight_blocks((wa_ref, wb_ref), w_scr)
    s1o_ref[...] = s1_ref[...].astype(BF16)
    s2o_ref[...] = s2_ref[...].astype(BF16)
    is_ctx = pl.program_id(1) >= LATENT_TILES
    used = True if ctx_queries else ~is_ctx

    def norm_tile(n_norm_chunks, with_rope):
        wide = bd_ref.shape[0]
        half_rows = TM_PROJ // 2
        for r0 in range(0, TM_PROJ, half_rows):
            rows = slice(r0, r0 + half_rows)
            acc = _dot(_row_tile(hl_ref, hc_ref, rows), w_scr[...])
            for c2 in range(Z_TILE // wide):
                a = acc[:, c2 * wide:(c2 + 1) * wide]
                normed = a
                if c2 * (wide // LANES) < n_norm_chunks:
                    ms = _dot((a * a).astype(BF16), bd_ref[...]) * (1.0 / HEAD_DIM)
                    normed = a * lax.rsqrt(ms + EPS) * gain_ref[:, c2 * wide:(c2 + 1) * wide]
                for c in range(wide // LANES):
                    chunk = c2 * (wide // LANES) + c
                    y = (normed if chunk < n_norm_chunks else a)[:, c * LANES:(c + 1) * LANES]
                    if with_rope and chunk < n_norm_chunks:
                        y = y * cos_ref[rows, :] + _swap16(y) * sin_ref[rows, :]
                    z_ref[rows, chunk * LANES:(chunk + 1) * LANES] = y.astype(BF16)

    is_query = (t == H_AQ) | (t == H_BQ) | (t == H_DQ) | (t == H_CU)

    @pl.when(((t == H_AQ) | (t == H_DQ)) & used | (t == H_DK))
    def _():
        norm_tile(4, True)

    @pl.when(t == H_AKV)
    def _():
        norm_tile(1, True)

    @pl.when((t == H_BQ) & used | (t == H_BK))
    def _():
        norm_tile(4, False)

    @pl.when((t == H_BV) | (t == H_CU) & used | (t == H_DV))
    def _():
        norm_tile(0, False)

    if not ctx_queries:
        @pl.when(is_query & is_ctx)
        def _():
            z_ref[...] = jnp.zeros_like(z_ref)


def _head_proj(hl, hc, w_in, gain, cos_t, sin_t, bd, w_branch, w_out, *, layer, ctx_queries):
    seq_tiles = SEQ // TM_PROJ
    row_tiles = LATENT_TILES + 1
    rope_block = lambda i: jnp.where(i < LATENT_TILES, i % seq_tiles, seq_tiles)
    w_spec = lambda half: pl.BlockSpec((1, D_MODEL, W_SRC), lambda t, i: (layer, 0, _head_src_block(t, half)))
    cast_in, cast_out, cast_shape = _side_cast_specs((w_branch, w_out), layer=layer, row_tiles=row_tiles)
    return pl.pallas_call(
        functools.partial(_head_kernel, ctx_queries=ctx_queries),
        grid=(HEAD_TILES, row_tiles),
        in_specs=_h_specs(True) + [
            w_spec(0), w_spec(1),
            pl.BlockSpec((1, Z_TILE), lambda t, i: (0, t)),
            pl.BlockSpec((TM_PROJ, LANES), lambda t, i: (rope_block(i), 0)),
            pl.BlockSpec((TM_PROJ, LANES), lambda t, i: (rope_block(i), 0)),
            pl.BlockSpec(bd.shape, lambda t, i: (0, 0)),
        ] + cast_in,
        out_specs=[pl.BlockSpec((None, TM_PROJ, Z_TILE), lambda t, i: (t, i, 0))] + cast_out,
        out_shape=[jax.ShapeDtypeStruct((HEAD_TILES, ALL_ROWS, Z_TILE), BF16)] + cast_shape,
        scratch_shapes=[pltpu.VMEM((D_MODEL, Z_TILE), BF16)],
        compiler_params=_cparams(("arbitrary", "arbitrary")),
        name="head_proj",
    )(hl, hc, w_in, w_in, gain, cos_t, sin_t, bd, w_branch, w_out)


def _softmax_attend(q, segs, extra_logit=None):
    scores = []
    for k, _, bias in segs:
        s = _dot_nt(q, k)
        if bias is not None:
            s = s + bias
        scores.append(s)
    m = scores[0].max(axis=-1, keepdims=True)
    for s in scores[1:]:
        m = jnp.maximum(m, s.max(axis=-1, keepdims=True))
    if extra_logit is not None:
        m = jnp.maximum(m, extra_logit)
    l = None
    o = None
    for s, (_, v, _) in zip(scores, segs):
        p = jnp.exp(s - m)
        ls = p.sum(axis=-1, keepdims=True)
        os_ = _dot(p.astype(BF16), v)
        l = ls if l is None else l + ls
        o = os_ if o is None else o + os_
    if extra_logit is not None:
        l = l + jnp.exp(extra_logit - m)
    return o, l


def _lane_group(shape, width):
    return lax.broadcasted_iota(jnp.int32, shape, 1) // width


def _zspec(tile, rows, index_rows, width=Z_TILE):
    return pl.BlockSpec((None, rows, width), lambda *ids: (tile, index_rows(*ids), 0))


def _repeat_kv_heads(kv):
    pieces = [kv[:, i * HEAD_DIM:(i + 1) * HEAD_DIM] for i in range(2 * A_KV_HEADS)]
    k = jnp.concatenate([pieces[g] for g in range(A_KV_HEADS) for _ in range(A_GROUP)], axis=1)
    v = jnp.concatenate([pieces[A_KV_HEADS + g] for g in range(A_KV_HEADS) for _ in range(A_GROUP)], axis=1)
    return k, v


WIN_BLOCKS_PER_STEP = 8


def _win_gqa_kernel(sink_ref, q_ref, *rest, latent):
    if latent:
        kv_ref, ckv_ref, o_ref, k_scr, v_scr, kc_scr, vc_scr = rest
    else:
        ckv_ref, o_ref, kc_scr, vc_scr = rest
    gw = A_GROUP * HEAD_DIM

    def build():
        kc_scr[...], vc_scr[...] = _repeat_kv_heads(ckv_ref[...])
        if latent:
            k_scr[...], v_scr[...] = _repeat_kv_heads(kv_ref[...])

    if latent:
        pl.when(pl.program_id(1) == 0)(build)
    else:
        build()

    tq = A_BLOCK if latent else q_ref.shape[0]
    n_sub = q_ref.shape[0] // tq
    rows = A_GROUP * tq
    row_head = lax.broadcasted_iota(jnp.int32, (rows, 1), 0) // tq
    lane_head = _lane_group((tq, gw), HEAD_DIM)
    for j in range(n_sub):
        q = q_ref[j * tq:(j + 1) * tq, :] * ATTN_SCALE
        if latent:
            n = pl.program_id(1) * n_sub + j
            span = 3 * A_BLOCK
            start = pl.multiple_of(jnp.clip((n - 1) * A_BLOCK, 0, SEQ - span), A_BLOCK)
            qpos = n * A_BLOCK + lax.broadcasted_iota(jnp.int32, (tq, span), 0)
            kpos = start + lax.broadcasted_iota(jnp.int32, (tq, span), 1)
            band = jnp.where(jnp.abs(qpos - kpos) <= A_WINDOW, 0.0, NEG).astype(F32)
            band = jnp.concatenate([band] * A_GROUP, axis=0)
        for g in range(A_KV_HEADS):
            gs = slice(g * gw, (g + 1) * gw)
            qg = q[:, gs]
            zero = jnp.zeros_like(qg)
            q_stack = jnp.concatenate([jnp.where(lane_head == r, qg, zero) for r in range(A_GROUP)], axis=0)
            sink = jnp.zeros((rows, 1), F32)
            for r in range(A_GROUP):
                sink = jnp.where(row_head == r, sink_ref[g * A_GROUP + r], sink)
            segs = [(kc_scr[:, gs], vc_scr[:, gs], None)]
            if latent:
                segs.append((k_scr[pl.ds(start, span), gs], v_scr[pl.ds(start, span), gs], band))
            o, l = _softmax_attend(q_stack, segs, extra_logit=sink)
            o = o / l
            og = jnp.zeros((tq, gw), F32)
            for r in range(A_GROUP):
                og = jnp.where(lane_head == r, o[r * tq:(r + 1) * tq], og)
            o_ref[j * tq:(j + 1) * tq, gs] = og.astype(BF16)


def _win_gqa(zh, sink, *, latent):
    kvb = 2 * A_KV_HEADS * HEAD_DIM
    c0 = _ctx_blocks(CTX_LEN)
    smem = pl.BlockSpec(memory_space=pltpu.SMEM)
    rep = lambda n: pltpu.VMEM((n, A_HEADS * HEAD_DIM), BF16)
    if latent:
        tq = WIN_BLOCKS_PER_STEP * A_BLOCK
        nq = SEQ // tq
        grid = (BATCH, nq)
        in_specs = [
            smem,
            _zspec(H_AQ, tq, lambda b, n: b * nq + n),
            _zspec(H_AKV, SEQ, lambda b, n: b, kvb),
            _zspec(H_AKV, CTX_LEN, lambda b, n: c0 + b, kvb),
        ]
        out_specs = pl.BlockSpec((tq, Z_TILE), lambda b, n: (b * nq + n, 0))
        args = (sink, zh, zh, zh)
        rows = LATENT_ROWS
        sem = ("parallel", "arbitrary")
        scratch = [rep(SEQ), rep(SEQ), rep(CTX_LEN), rep(CTX_LEN)]
    else:
        grid = (BATCH,)
        in_specs = [
            smem,
            _zspec(H_AQ, CTX_LEN, lambda b: c0 + b),
            _zspec(H_AKV, CTX_LEN, lambda b: c0 + b, kvb),
        ]
        out_specs = pl.BlockSpec((CTX_LEN, Z_TILE), lambda b: (b, 0))
        args = (sink, zh, zh)
        rows = CTX_ROWS
        sem = ("parallel",)
        scratch = [rep(CTX_LEN), rep(CTX_LEN)]
    return pl.pallas_call(
        functools.partial(_win_gqa_kernel, latent=latent),
        grid=grid,
        in_specs=in_specs,
        out_specs=out_specs,
        out_shape=jax.ShapeDtypeStruct((rows, BRANCH_W), BF16),
        scratch_shapes=scratch,
        compiler_params=_cparams(sem),
        name="win_gqa" if latent else "win_gqa_ctx",
    )(*args)


N_ROW_OFFSETS = 2 * B_WIN_ROWS - 1
N_COL_OFFSETS = 2 * B_WIN_COLS - 1
N_PAIR = N_ROW_OFFSETS + 1
NBR_QROWS = 4
NBR_SLAB = NBR_QROWS + B_WIN_ROWS
NBR_GROUPS = GRID_H // NBR_QROWS
NBR_CASES = 3


def _nbr_case_geometry(case):
    if case == 0:
        return B_WIN_ROWS - 1, lambda a, c: 0 <= c < B_WIN_ROWS
    if case == 1:
        return B_WIN_ROWS - 1 - NBR_QROWS, lambda a, c: 0 <= c - a < B_WIN_ROWS
    return B_WIN_ROWS - 1 - 2 * NBR_QROWS, lambda a, c: NBR_QROWS <= c < NBR_SLAB


def _rpb_table_kernel(rpb_ref, o_ref, pair_scr):
    h = pl.program_id(0)
    shape = (GRID_W, LANES)
    qcol = lax.broadcasted_iota(jnp.int32, shape, 0)
    lane = lax.broadcasted_iota(jnp.int32, shape, 1)
    kcol = lane & (GRID_W - 1)
    hi = lane >= GRID_W
    cs = jnp.clip(qcol - B_WIN_COLS // 2, 0, GRID_W - B_WIN_COLS)
    in_window = (kcol >= cs) & (kcol < cs + B_WIN_COLS)
    ci = kcol - qcol + (B_WIN_COLS - 1)
    neg = jnp.full(shape, NEG, F32)

    def rpb(d, c):
        return rpb_ref[(h * N_ROW_OFFSETS + d) * N_COL_OFFSETS + c]

    tables = []
    for d in range(N_ROW_OFFSETS):
        acc = neg
        for c in range(N_COL_OFFSETS):
            acc = jnp.where(ci == c, rpb(d, c), acc)
        tables.append(jnp.where(in_window, acc, NEG))
    for p in range(N_PAIR):
        lo = tables[p - 1] if p >= 1 else neg
        up = tables[p] if p < N_ROW_OFFSETS else neg
        pair_scr[p] = jnp.where(hi, up, lo)

    for case in range(NBR_CASES):
        shift, in_rows = _nbr_case_geometry(case)
        for a in range(NBR_QROWS):
            for cp in range(NBR_SLAB // 2):
                c = 2 * cp
                lo_ok, hi_ok = in_rows(a, c), in_rows(a, c + 1)
                if lo_ok or hi_ok:
                    piece = pair_scr[c - a + shift + 1]
                    if not lo_ok:
                        piece = jnp.where(hi, piece, neg)
                    if not hi_ok:
                        piece = jnp.where(hi, neg, piece)
                else:
                    piece = neg
                o_ref[case, 0, a * GRID_W:(a + 1) * GRID_W, cp * LANES:(cp + 1) * LANES] = piece


def _rpb_table(rpb):
    nq, nk = NBR_QROWS * GRID_W, NBR_SLAB * GRID_W
    return pl.pallas_call(
        _rpb_table_kernel,
        grid=(B_HEADS,),
        in_specs=[pl.BlockSpec(memory_space=pltpu.SMEM)],
        out_specs=pl.BlockSpec((NBR_CASES, 1, nq, nk), lambda h: (0, h, 0, 0)),
        out_shape=jax.ShapeDtypeStruct((NBR_CASES, B_HEADS, nq, nk), F32),
        scratch_shapes=[pltpu.VMEM((N_PAIR, GRID_W, LANES), F32)],
        compiler_params=_cparams(("parallel",)),
        name="rpb_table",
    )(rpb.reshape(-1))


def _pair_attend(q, lane_hi, segs):
    m = q.shape[0]
    zero = jnp.zeros_like(q)
    q_stack = jnp.concatenate([jnp.where(lane_hi, zero, q), jnp.where(lane_hi, q, zero)], axis=0)
    o, l = _softmax_attend(q_stack, segs)
    o = o / l
    return jnp.where(lane_hi, o[m:], o[:m])


NBR_GROUPS_PER_STEP = 4


def _nbr_kernel(q_ref, k_ref, v_ref, kc_ref, vc_ref, bias_ref, o_ref):
    nq = NBR_QROWS * GRID_W
    nk = NBR_SLAB * GRID_W
    lane_hi = lax.broadcasted_iota(jnp.int32, (nq, LANES), 1) >= HEAD_DIM
    for j in range(NBR_GROUPS_PER_STEP):
        g = pl.program_id(1) * NBR_GROUPS_PER_STEP + j
        case = jnp.where(g == 0, 0, jnp.where(g == NBR_GROUPS - 1, 2, 1))
        r0 = jnp.clip(NBR_QROWS * g - B_WIN_ROWS // 2, 0, GRID_H - NBR_SLAB)
        base = pl.multiple_of(r0 * GRID_W, GRID_W)
        rows = slice(j * nq, (j + 1) * nq)
        for pr in range(B_HEADS // 2):
            sl = slice(pr * LANES, (pr + 1) * LANES)
            kp = k_ref[pl.ds(base, nk), sl]
            vp = v_ref[pl.ds(base, nk), sl]
            kcp = kc_ref[:, sl]
            vcp = vc_ref[:, sl]
            bias = bias_ref[case, 2 * pr:2 * pr + 2].reshape(2 * nq, nk)
            segs = [(kp, vp, bias), (kcp, vcp, None)]
            q = q_ref[rows, sl] * ATTN_SCALE
            o_ref[rows, sl] = _pair_attend(q, lane_hi, segs).astype(BF16)


def _nbr_attn(zh, table):
    nq = NBR_GROUPS_PER_STEP * NBR_QROWS * GRID_W
    steps = NBR_GROUPS // NBR_GROUPS_PER_STEP
    c0 = _ctx_blocks(CTX_LEN)
    return pl.pallas_call(
        _nbr_kernel,
        grid=(BATCH, steps),
        in_specs=[
            _zspec(H_BQ, nq, lambda b, s: b * steps + s),
            _zspec(H_BK, SEQ, lambda b, s: b),
            _zspec(H_BV, SEQ, lambda b, s: b),
            _zspec(H_BK, CTX_LEN, lambda b, s: c0 + b),
            _zspec(H_BV, CTX_LEN, lambda b, s: c0 + b),
            pl.BlockSpec(table.shape, lambda b, s: (0, 0, 0, 0), pipeline_mode=pl.Buffered(1)),
        ],
        out_specs=pl.BlockSpec((nq, Z_TILE), lambda b, s: (b * steps + s, 0)),
        out_shape=jax.ShapeDtypeStruct((LATENT_ROWS, BRANCH_W), BF16),
        compiler_params=_cparams(("parallel", "arbitrary")),
        name="nbr_attn",
    )(zh, zh, zh, zh, zh, table)


def _ctx_mha_kernel(q_ref, k_ref, v_ref, o_ref):
    q = q_ref[...] * ATTN_SCALE
    lane_hi = lax.broadcasted_iota(jnp.int32, (q.shape[0], LANES), 1) >= HEAD_DIM
    for pr in range(B_HEADS // 2):
        sl = slice(pr * LANES, (pr + 1) * LANES)
        segs = [(k_ref[:, sl], v_ref[:, sl], None)]
        o_ref[:, sl] = _pair_attend(q[:, sl], lane_hi, segs).astype(BF16)


def _ctx_mha(zh):
    c0 = _ctx_blocks(CTX_LEN)
    rows = lambda b: c0 + b
    return pl.pallas_call(
        _ctx_mha_kernel,
        grid=(BATCH,),
        in_specs=[_zspec(H_BQ, CTX_LEN, rows), _zspec(H_BK, CTX_LEN, rows), _zspec(H_BV, CTX_LEN, rows)],
        out_specs=pl.BlockSpec((CTX_LEN, Z_TILE), lambda b: (b, 0)),
        out_shape=jax.ShapeDtypeStruct((CTX_ROWS, BRANCH_W), BF16),
        compiler_params=_cparams(("parallel",)),
        name="nbr_attn_ctx",
    )(zh, zh, zh)


def _pool_kernel(u_ref, w_ref, scale_ref, o_ref):
    n = u_ref.shape[0]
    t = lax.broadcasted_iota(jnp.int32, (n, C_GROUP_DIM), 0)

    def down(a, k):
        return jnp.where(t >= k, pltpu.roll(a, k, axis=0), 0.0)

    def up(a, k):
        return jnp.where(t < n - k, pltpu.roll(a, n - k, axis=0), 0.0)

    for g, w in enumerate(C_WINDOWS):
        sl = slice(g * C_GROUP_DIM, (g + 1) * C_GROUP_DIM)
        u = u_ref[:, sl].astype(F32)
        half = w // 2
        back = u
        fwd = u
        k = 1
        while k < half:
            back = back + down(back, k)
            fwd = fwd + up(fwd, k)
            k *= 2
        total = down(back, 1) + fwd
        lo = jnp.maximum(t - half, 0)
        hi = jnp.minimum(t + half - 1, n - 1)
        cnt = (hi - lo + 1).astype(F32)
        pooled = total / cnt - u
        y = _dot(pooled.astype(BF16), w_ref[0, g]) * scale_ref[:, sl]
        o_ref[:, sl] = y.astype(BF16)


def _pool(zh, c_w, c_scale, *, layer, n, latent):
    rows = LATENT_ROWS if latent else CTX_ROWS
    b0 = 0 if latent else _ctx_blocks(n)
    return pl.pallas_call(
        _pool_kernel,
        grid=(rows // n,),
        in_specs=[
            _zspec(H_CU, n, lambda b: b0 + b),
            pl.BlockSpec((1, len(C_WINDOWS), C_GROUP_DIM, C_GROUP_DIM), lambda b: (layer, 0, 0, 0)),
            pl.BlockSpec((1, BRANCH_W), lambda b: (0, 0)),
        ],
        out_specs=pl.BlockSpec((n, BRANCH_W), lambda b: (b, 0)),
        out_shape=jax.ShapeDtypeStruct((rows, BRANCH_W), BF16),
        compiler_params=_cparams(("parallel",)),
        name=f"pool_{n}",
    )(zh, c_w, c_scale)


DIFF_ROWS = 256
DIFF_BLOCKS_PER_STEP = 2


def _diff_kernel(lam_ref, q_ref, *rest, lam_init, latent):
    if latent:
        k_ref, v_ref, kc_ref, vc_ref, g_ref, o_ref = rest
    else:
        kc_ref, vc_ref, g_ref, o_ref = rest
    hw = 2 * HEAD_DIM
    dl = lam_ref[...]
    lam = (jnp.exp(jnp.sum(dl[0:1] * dl[1:2], axis=-1, keepdims=True))
           - jnp.exp(jnp.sum(dl[2:3] * dl[3:4], axis=-1, keepdims=True)) + lam_init)
    tq = min(q_ref.shape[0], DIFF_ROWS)
    lane_hi = lax.broadcasted_iota(jnp.int32, (tq, hw), 1) >= HEAD_DIM
    for r0 in range(0, q_ref.shape[0], tq):
        rows = slice(r0, r0 + tq)
        for h in range(D_HEADS):
            sl = slice(h * hw, (h + 1) * hw)
            qh = q_ref[rows, sl] * ATTN_SCALE
            zero = jnp.zeros_like(qh)
            q_stack = jnp.concatenate([jnp.where(lane_hi, zero, qh), jnp.where(lane_hi, qh, zero)], axis=0)
            segs = [(kc_ref[:, sl], vc_ref[:, sl], None)]
            if latent:
                segs.append((k_ref[:, sl], v_ref[:, sl], None))
            o, l = _softmax_attend(q_stack, segs)
            o = o / l
            od = o[:tq] - lam * o[tq:]
            ms = jnp.mean(od * od, axis=-1, keepdims=True)
            y = od * lax.rsqrt(ms + EPS) * g_ref[...] * (1.0 - lam_init)
            o_ref[rows, sl] = y.astype(BF16)


def _diff_attn(zh, d_lambda, subln, *, lam_init, latent):
    full = lambda shape: pl.BlockSpec(shape, lambda *_: (0,) * len(shape))
    c0 = _ctx_blocks(CTX_LEN)
    if latent:
        tq = DIFF_BLOCKS_PER_STEP * DIFF_ROWS
        nq = SEQ // tq
        grid = (BATCH, nq)
        in_specs = [
            full((4, HEAD_DIM)),
            _zspec(H_DQ, tq, lambda b, n: b * nq + n),
            _zspec(H_DK, SEQ, lambda b, n: b),
            _zspec(H_DV, SEQ, lambda b, n: b),
            _zspec(H_DK, CTX_LEN, lambda b, n: c0 + b),
            _zspec(H_DV, CTX_LEN, lambda b, n: c0 + b),
            full((1, 2 * HEAD_DIM)),
        ]
        out_specs = pl.BlockSpec((tq, Z_TILE), lambda b, n: (b * nq + n, 0))
        args = (d_lambda, zh, zh, zh, zh, zh, subln)
        rows = LATENT_ROWS
        sem = ("parallel", "arbitrary")
    else:
        grid = (BATCH,)
        crow = lambda b: c0 + b
        in_specs = [
            full((4, HEAD_DIM)),
            _zspec(H_DQ, CTX_LEN, crow),
            _zspec(H_DK, CTX_LEN, crow),
            _zspec(H_DV, CTX_LEN, crow),
            full((1, 2 * HEAD_DIM)),
        ]
        out_specs = pl.BlockSpec((CTX_LEN, Z_TILE), lambda b: (b, 0))
        args = (d_lambda, zh, zh, zh, subln)
        rows = CTX_ROWS
        sem = ("parallel",)
    return pl.pallas_call(
        functools.partial(_diff_kernel, lam_init=lam_init, latent=latent),
        grid=grid,
        in_specs=in_specs,
        out_specs=out_specs,
        out_shape=jax.ShapeDtypeStruct((rows, BRANCH_W), BF16),
        compiler_params=_cparams(sem),
        name="diff_attn" if latent else "diff_attn_ctx",
    )(*args)


def _merge_kernel(*refs):
    y_refs = refs[:N_BRANCH]
    g_refs = refs[N_BRANCH:2 * N_BRANCH]
    wb_ref, wo_ref, x_ref, gt_ref, gn_ref, shift_ref, scale_ref, xo_ref, h_ref = refs[2 * N_BRANCH:]
    mixed = None
    for k in range(N_BRANCH):
        gate = 0.5 * (jnp.tanh(0.5 * g_refs[k][...].astype(F32)) + 1.0)
        proj = _dot(y_refs[k][...], wb_ref[0, k]) * gate
        mixed = proj if mixed is None else mixed + proj
    out = _dot(mixed.astype(BF16), wo_ref[0])
    xo_ref[...] = x_ref[...] + gt_ref[0] * out
    _norm_modulate_rows(xo_ref, gn_ref, shift_ref, scale_ref, h_ref)


def _merge(ys, gates, w_branch, w_out, x, gn2, mod3, *, tm, row0, mod_row):
    m = x.shape[0]
    row = lambda i: (i, 0)
    g0 = row0 // tm
    mod = lambda chunk: pl.BlockSpec((1, 1, D_MODEL), lambda i: (mod_row(i), 0, chunk))
    resident = pl.Buffered(1)
    return pl.pallas_call(
        _merge_kernel,
        grid=(m // tm,),
        in_specs=[pl.BlockSpec((tm, BRANCH_W), row)] * N_BRANCH
        + [pl.BlockSpec((tm, D_MODEL), lambda i, k=k: (g0 + i, k)) for k in range(N_BRANCH)]
        + [
            pl.BlockSpec((1, N_BRANCH, BRANCH_W, D_MODEL), lambda i: (0, 0, 0, 0), pipeline_mode=resident),
            pl.BlockSpec((1, D_MODEL, D_MODEL), lambda i: (0, 0, 0), pipeline_mode=resident),
            pl.BlockSpec((tm, D_MODEL), row),
            mod(2),
            pl.BlockSpec((1, D_MODEL), lambda i: (0, 0)),
            mod(3),
            mod(4),
        ],
        out_specs=[pl.BlockSpec((tm, D_MODEL), row), pl.BlockSpec((tm, D_MODEL), row)],
        out_shape=[jax.ShapeDtypeStruct((m, D_MODEL), F32), jax.ShapeDtypeStruct((m, D_MODEL), BF16)],
        compiler_params=_cparams(("parallel",)),
        name=f"merge_{m}",
    )(*ys, *([gates] * N_BRANCH), w_branch, w_out, x, mod3, gn2, mod3, mod3)


def _mlp_kernel(h_ref, x_ref, gt_ref, w1_ref, w2_ref, *rest, emit_next):
    if emit_next:
        gn_ref, shift_ref, scale_ref, xo_ref, hn_ref = rest
    else:
        (xo_ref,) = rest
    f = pl.program_id(1)

    @pl.when(f == 0)
    def _():
        xo_ref[...] = jnp.zeros_like(xo_ref)

    a = jnp.maximum(_dot(h_ref[...], w1_ref[0]), 0.0)
    xo_ref[...] += _dot((a * a).astype(BF16), w2_ref[0])

    @pl.when(f == pl.num_programs(1) - 1)
    def _():
        xo_ref[...] = x_ref[...] + gt_ref[0] * xo_ref[...]
        if emit_next:
            _norm_modulate_rows(xo_ref, gn_ref, shift_ref, scale_ref, hn_ref)


def _mlp(h, x, mod3, w1, w2, next_norm, *, tm, tf, mod_row):
    m = x.shape[0]
    emit_next = next_norm is not None
    row = lambda i, f: (i, 0)
    in_specs = [
        pl.BlockSpec((tm, D_MODEL), row),
        pl.BlockSpec((tm, D_MODEL), row),
        pl.BlockSpec((1, 1, D_MODEL), lambda i, f: (mod_row(i), 0, 5)),
        pl.BlockSpec((1, D_MODEL, tf), lambda i, f: (0, 0, f)),
        pl.BlockSpec((1, tf, D_MODEL), lambda i, f: (0, f, 0)),
    ]
    args = [h, x, mod3, w1, w2]
    out_specs = [pl.BlockSpec((tm, D_MODEL), row)]
    out_shape = [jax.ShapeDtypeStruct((m, D_MODEL), F32)]
    if emit_next:
        gn_next, mod3_next = next_norm
        in_specs += [
            pl.BlockSpec((1, D_MODEL), lambda i, f: (0, 0)),
            pl.BlockSpec((1, 1, D_MODEL), lambda i, f: (mod_row(i), 0, 0)),
            pl.BlockSpec((1, 1, D_MODEL), lambda i, f: (mod_row(i), 0, 1)),
        ]
        args += [gn_next, mod3_next, mod3_next]
        out_specs.append(pl.BlockSpec((tm, D_MODEL), row))
        out_shape.append(jax.ShapeDtypeStruct((m, D_MODEL), BF16))
    outs = pl.pallas_call(
        functools.partial(_mlp_kernel, emit_next=emit_next),
        grid=(m // tm, D_FF // tf),
        in_specs=in_specs,
        out_specs=out_specs,
        out_shape=out_shape,
        compiler_params=_cparams(("parallel", "arbitrary")),
        name=f"mlp_{m}",
    )(*args)
    return outs if emit_next else (outs[0], None)


def _rope_tables():
    t = jnp.arange(SEQ)
    row = (t // GRID_W).astype(F32)
    col = (t % GRID_W).astype(F32)
    n_freq = HEAD_DIM // 4
    inv = ROPE_BASE ** (-jnp.arange(n_freq, dtype=F32) / n_freq)
    ar = row[:, None] * inv
    ac = col[:, None] * inv
    cos_h = jnp.concatenate([jnp.cos(ar), jnp.cos(ar), jnp.cos(ac), jnp.cos(ac)], axis=-1)
    sin_h = jnp.concatenate([-jnp.sin(ar), jnp.sin(ar), -jnp.sin(ac), jnp.sin(ac)], axis=-1)
    reps = LANES // HEAD_DIM
    cos_t = jnp.concatenate([jnp.tile(cos_h, (1, reps)), jnp.ones((TM_PROJ, LANES), F32)], axis=0)
    sin_t = jnp.concatenate([jnp.tile(sin_h, (1, reps)), jnp.zeros((TM_PROJ, LANES), F32)], axis=0)
    return cos_t, sin_t


def _head_gain_row(qk_gain_l):
    ones = jnp.ones((Z_TILE,), F32)
    rep = lambda g: jnp.tile(g, Z_TILE // HEAD_DIM)
    akv = jnp.concatenate([jnp.tile(qk_gain_l[0, 1], A_KV_HEADS), jnp.ones((Z_TILE - A_KV_HEADS * HEAD_DIM,), F32)])
    return jnp.concatenate([
        rep(qk_gain_l[0, 0]), akv,
        rep(qk_gain_l[1, 0]), rep(qk_gain_l[1, 1]), ones, ones,
        rep(qk_gain_l[2, 0]), rep(qk_gain_l[2, 1]), ones,
    ])[None, :]


def kernel(x, c, ctx, c_ctx, w_ada, b_ada, g_norm1, g_norm2, w_in, b_gate, qk_gain, a_sink, b_rpb,
           c_w, c_scale, d_lambda, d_subln, w_branch, w_out, w_ff1, w_ff2):
    cos_t, sin_t = _rope_tables()
    lane = jnp.arange(HEAD_SUM_LANES)
    bd = (lane[:, None] // HEAD_DIM == lane[None, :] // HEAD_DIM).astype(BF16)

    cc = jnp.concatenate([c, c_ctx[None, :], jnp.zeros((8 - BATCH - 1, D_MODEL), F32)], axis=0)
    mod_all = _modulation(cc, w_ada, b_ada)

    cw = c_w.astype(BF16)
    w_branch2d = w_branch.reshape(DEPTH, N_BRANCH * BRANCH_W, D_MODEL)

    xl = x.reshape(LATENT_ROWS, D_MODEL)
    xc = ctx.reshape(CTX_ROWS, D_MODEL)
    ctx_row = lambda i: BATCH
    batch_row = lambda tm: (lambda i: i // (SEQ // tm))
    mods = [mod_all[l].reshape(8, 1, 6 * D_MODEL) for l in range(DEPTH)]
    gn1 = [g_norm1[l][None, :] for l in range(DEPTH)]

    tm_merge, tm_mlp, tf_mlp, tn_gate = TM_MERGE, TM_MLP, TF_MLP, TN_GATE
    hl = _norm_mod(xl, gn1[0], mods[0], tm=TM_PROJ, mod_row=batch_row(TM_PROJ))
    hc = _norm_mod(xc, gn1[0], mods[0], tm=TM_PROJ, mod_row=ctx_row)

    for l in range(DEPTH):
        last = l == DEPTH - 1
        lam_init = 0.8 - 0.6 * math.exp(-0.3 * l)
        mod3 = mods[l]
        gn2 = g_norm2[l][None, :]
        cs = c_scale[l][None, :]
        subln = d_subln[l][None, :]
        next_norm = None if last else (gn1[l + 1], mods[l + 1])

        gates, w1, w2 = _gate_proj(hl, None if last else hc, w_in, b_gate[l][None, :], w_ff1, w_ff2,
                                   layer=l, tn=tn_gate)
        zh, wb, wo = _head_proj(hl, hc, w_in, _head_gain_row(qk_gain[l]), cos_t, sin_t, bd, w_branch2d, w_out,
                                layer=l, ctx_queries=not last)
        wb = wb.reshape(1, N_BRANCH, BRANCH_W, D_MODEL)

        table = _rpb_table(b_rpb[l])
        ya = _win_gqa(zh, a_sink[l], latent=True)
        yb = _nbr_attn(zh, table)
        yc = _pool(zh, cw, cs, layer=l, n=SEQ, latent=True)
        yd = _diff_attn(zh, d_lambda[l], subln, lam_init=lam_init, latent=True)
        xl, h2 = _merge((ya, yb, yc, yd), gates, wb, wo, xl, gn2, mod3,
                        tm=tm_merge, row0=0, mod_row=batch_row(tm_merge))
        xl, hl = _mlp(h2, xl, mod3, w1, w2, next_norm, tm=tm_mlp, tf=tf_mlp, mod_row=batch_row(tm_mlp))
        if not last:
            yac = _win_gqa(zh, a_sink[l], latent=False)
            ybc = _ctx_mha(zh)
            ycc = _pool(zh, cw, cs, layer=l, n=CTX_LEN, latent=False)
            ydc = _diff_attn(zh, d_lambda[l], subln, lam_init=lam_init, latent=False)
            xc, h2c = _merge((yac, ybc, ycc, ydc), gates, wb, wo, xc, gn2, mod3,
                             tm=tm_merge, row0=LATENT_ROWS, mod_row=ctx_row)
            xc, hc = _mlp(h2c, xc, mod3, w1, w2, next_norm, tm=tm_mlp, tf=tf_mlp, mod_row=ctx_row)

    return xl.reshape(BATCH, SEQ, D_MODEL)
```

```python
import functools
import math

import jax
import jax.numpy as jnp
from jax import lax
from jax.experimental import pallas as pl
from jax.experimental.pallas import tpu as pltpu

D_MODEL = 2048
BATCH = 4
SEQ = 2048
DEPTH = 2
GRID_W = 64
GRID_H = SEQ // GRID_W
CTX_LEN = 256
HEAD_DIM = 64
N_BRANCH = 4
BRANCH_W = D_MODEL // N_BRANCH
A_HEADS = 8
A_KV_HEADS = 2
A_GROUP = A_HEADS // A_KV_HEADS
A_WINDOW = 128
A_BLOCK = 128
B_HEADS = 8
B_WIN_ROWS = 8
B_WIN_COLS = 16
C_WINDOWS = (2, 4, 8, 16)
C_GROUP_DIM = 128
D_HEADS = 4
D_FF = 4 * D_MODEL
ROPE_BASE = 10000.0
EPS = 1e-6
NEG = -1e30
ATTN_SCALE = HEAD_DIM ** -0.5

F32 = jnp.float32
BF16 = jnp.bfloat16

V7X_VMEM_LIMIT_BYTES = 56 * 1024 * 1024
LANES = 128

LATENT_ROWS = BATCH * SEQ
CTX_ROWS = BATCH * CTX_LEN
ALL_ROWS = LATENT_ROWS + CTX_ROWS

Z_TILE = 512
H_AQ, H_AKV, H_BQ, H_BK, H_BV, H_CU, H_DQ, H_DK, H_DV = range(9)
HEAD_TILES = 9
W_SRC = 256
GATE_COLS = N_BRANCH * D_MODEL
GATE_SRC0 = (12544 - GATE_COLS) // W_SRC
HEAD_SUM_LANES = 256

TM_PROJ = 1024
TN_GATE = 1024
TM_MERGE = 512
TM_MLP = 512
TF_MLP = 1024


def _cparams(sem):
    return pltpu.CompilerParams(dimension_semantics=sem, vmem_limit_bytes=V7X_VMEM_LIMIT_BYTES)


def _dot_nt(a, b):
    return lax.dot_general(a, b, (((1,), (1,)), ((), ())), preferred_element_type=F32)


def _dot(a, b):
    return jnp.dot(a, b, preferred_element_type=F32)


def _ctx_blocks(block_rows):
    return LATENT_ROWS // block_rows


def _mod_kernel(c_ref, w_ref, b_ref, o_ref):
    c = c_ref[...]
    s = c * (0.5 * (jnp.tanh(0.5 * c) + 1.0))
    o_ref[0] = _dot(s, w_ref[0]) + b_ref[0]


def _modulation(cc, w_ada, b_ada):
    tn = 1024
    n = 6 * D_MODEL
    return pl.pallas_call(
        _mod_kernel,
        grid=(DEPTH, n // tn),
        in_specs=[
            pl.BlockSpec((8, D_MODEL), lambda l, j: (0, 0)),
            pl.BlockSpec((1, D_MODEL, tn), lambda l, j: (l, 0, j)),
            pl.BlockSpec((1, 1, tn), lambda l, j: (l, 0, j)),
        ],
        out_specs=pl.BlockSpec((1, 8, tn), lambda l, j: (l, 0, j)),
        out_shape=jax.ShapeDtypeStruct((DEPTH, 8, n), F32),
        compiler_params=_cparams(("parallel", "parallel")),
        name="modulation",
    )(cc, w_ada, b_ada.reshape(DEPTH, 1, n))


NORM_ROWS = 16


def _norm_modulate_rows(x_ref, gn_ref, shift_ref, scale_ref, h_ref):
    n = x_ref.shape[0]
    gs = gn_ref[...] * (1.0 + scale_ref[0])
    shift = shift_ref[0]

    def body(c, carry):
        rows = pl.ds(pl.multiple_of(c * NORM_ROWS, NORM_ROWS), NORM_ROWS)
        x = x_ref[rows, :]
        inv = lax.rsqrt(jnp.mean(x * x, axis=-1, keepdims=True) + EPS)
        h_ref[rows, :] = ((x * inv) * gs + shift).astype(BF16)
        return carry

    lax.fori_loop(0, n // NORM_ROWS, body, 0, unroll=8)


def _norm_mod_kernel(x_ref, gn_ref, shift_ref, scale_ref, h_ref):
    _norm_modulate_rows(x_ref, gn_ref, shift_ref, scale_ref, h_ref)


def _norm_mod(x, gn, mod3, *, tm, mod_row):
    m = x.shape[0]
    return pl.pallas_call(
        _norm_mod_kernel,
        grid=(m // tm,),
        in_specs=[
            pl.BlockSpec((tm, D_MODEL), lambda i: (i, 0)),
            pl.BlockSpec((1, D_MODEL), lambda i: (0, 0)),
            pl.BlockSpec((1, 1, D_MODEL), lambda i: (mod_row(i), 0, 0)),
            pl.BlockSpec((1, 1, D_MODEL), lambda i: (mod_row(i), 0, 1)),
        ],
        out_specs=pl.BlockSpec((tm, D_MODEL), lambda i: (i, 0)),
        out_shape=jax.ShapeDtypeStruct((m, D_MODEL), BF16),
        compiler_params=_cparams(("parallel",)),
        name=f"norm_mod_{m}",
    )(x, gn, mod3, mod3)


def _swap16(y):
    lane = lax.broadcasted_iota(jnp.int32, y.shape, 1)
    fwd = pltpu.roll(y, LANES - 16, axis=1)
    bwd = pltpu.roll(y, 16, axis=1)
    return jnp.where((lane & 16) == 0, fwd, bwd)


def _cast_weight_blocks(w_refs, w_scr):
    @pl.when(pl.program_id(1) == 0)
    def _():
        for c, w_ref in enumerate(w_refs):
            w_scr[:, c * W_SRC:(c + 1) * W_SRC] = w_ref[0].astype(BF16)


LATENT_TILES = LATENT_ROWS // TM_PROJ
CAST_CHUNKS = 64


def _row_tile(hl_ref, hc_ref, rows=slice(None)):
    if hc_ref is None:
        return hl_ref[rows, :]
    return jnp.where(pl.program_id(1) >= LATENT_TILES, hc_ref[rows, :], hl_ref[rows, :])


def _h_specs(with_ctx):
    specs = [pl.BlockSpec((TM_PROJ, D_MODEL), lambda n, i: (jnp.minimum(i, LATENT_TILES - 1), 0))]
    if with_ctx:
        specs.append(pl.BlockSpec((TM_PROJ, D_MODEL), lambda n, i: (0, 0), pipeline_mode=pl.Buffered(1)))
    return specs


def _side_cast_specs(arrays, *, layer, row_tiles):
    chunk = lambda n, i: jnp.minimum(n * row_tiles + i, CAST_CHUNKS - 1)
    in_specs, out_specs, out_shape = [], [], []
    for a in arrays:
        rows, cols = a.shape[1] // CAST_CHUNKS, a.shape[2]
        in_specs.append(pl.BlockSpec((1, rows, cols), lambda n, i: (layer, chunk(n, i), 0)))
        out_specs.append(pl.BlockSpec((1, rows, cols), lambda n, i: (0, chunk(n, i), 0)))
        out_shape.append(jax.ShapeDtypeStruct((1,) + a.shape[1:], BF16))
    return in_specs, out_specs, out_shape


def _gate_kernel(*refs, with_ctx):
    hl_ref = refs[0]
    hc_ref = refs[1] if with_ctx else None
    *w_refs, bias_ref, f1_ref, f2_ref, g_ref, f1o_ref, f2o_ref, w_scr = refs[1 + with_ctx:]
    _cast_weight_blocks(w_refs, w_scr)
    f1o_ref[...] = f1_ref[...].astype(BF16)
    f2o_ref[...] = f2_ref[...].astype(BF16)
    g_ref[...] = (_dot(_row_tile(hl_ref, hc_ref), w_scr[...]) + bias_ref[...]).astype(BF16)


def _gate_proj(hl, hc, w_in, bias, w_ff1, w_ff2, *, layer, tn):
    with_ctx = hc is not None
    row_tiles = LATENT_TILES + with_ctx
    n_src = tn // W_SRC
    w_spec = lambda c: pl.BlockSpec((1, D_MODEL, W_SRC), lambda n, i: (layer, 0, GATE_SRC0 + n * n_src + c))
    cast_in, cast_out, cast_shape = _side_cast_specs((w_ff1, w_ff2), layer=layer, row_tiles=row_tiles)
    h_args = (hl, hc) if with_ctx else (hl,)
    return pl.pallas_call(
        functools.partial(_gate_kernel, with_ctx=with_ctx),
        grid=(GATE_COLS // tn, row_tiles),
        in_specs=_h_specs(with_ctx)
        + [w_spec(c) for c in range(n_src)]
        + [pl.BlockSpec((1, tn), lambda n, i: (0, n))]
        + cast_in,
        out_specs=[pl.BlockSpec((TM_PROJ, tn), lambda n, i: (i, n))] + cast_out,
        out_shape=[jax.ShapeDtypeStruct((row_tiles * TM_PROJ, GATE_COLS), BF16)] + cast_shape,
        scratch_shapes=[pltpu.VMEM((D_MODEL, tn), BF16)],
        compiler_params=_cparams(("arbitrary", "arbitrary")),
        name="gate_proj",
    )(*h_args, *([w_in] * n_src), bias, w_ff1, w_ff2)


def _head_src_block(t, half):
    return jnp.where(t == H_AQ, half, jnp.where(t == H_AKV, 2, 2 * t - 1 + half))


def _head_kernel(hl_ref, hc_ref, wa_ref, wb_ref, gain_ref, cos_ref, sin_ref, bd_ref, s1_ref, s2_ref,
                 z_ref, s1o_ref, s2o_ref, w_scr, *, ctx_queries):
    t = pl.program_id(0)
    _cast_weight_blocks((wa_ref, wb_ref), w_scr)
    s1o_ref[...] = s1_ref[...].astype(BF16)
    s2o_ref[...] = s2_ref[...].astype(BF16)
    is_ctx = pl.program_id(1) >= LATENT_TILES
    used = True if ctx_queries else ~is_ctx

    def norm_tile(n_norm_chunks, with_rope):
        wide = bd_ref.shape[0]
        half_rows = TM_PROJ // 2
        for r0 in range(0, TM_PROJ, half_rows):
            rows = slice(r0, r0 + half_rows)
            acc = _dot(_row_tile(hl_ref, hc_ref, rows), w_scr[...])
            for c2 in range(Z_TILE // wide):
                a = acc[:, c2 * wide:(c2 + 1) * wide]
                normed = a
                if c2 * (wide // LANES) < n_norm_chunks:
                    ms = _dot((a * a).astype(BF16), bd_ref[...]) * (1.0 / HEAD_DIM)
                    normed = a * lax.rsqrt(ms + EPS) * gain_ref[:, c2 * wide:(c2 + 1) * wide]
                for c in range(wide // LANES):
                    chunk = c2 * (wide // LANES) + c
                    y = (normed if chunk < n_norm_chunks else a)[:, c * LANES:(c + 1) * LANES]
                    if with_rope and chunk < n_norm_chunks:
                        y = y * cos_ref[rows, :] + _swap16(y) * sin_ref[rows, :]
                    z_ref[rows, chunk * LANES:(chunk + 1) * LANES] = y.astype(BF16)

    is_query = (t == H_AQ) | (t == H_BQ) | (t == H_DQ) | (t == H_CU)

    @pl.when(((t == H_AQ) | (t == H_DQ)) & used | (t == H_DK))
    def _():
        norm_tile(4, True)

    @pl.when(t == H_AKV)
    def _():
        norm_tile(1, True)

    @pl.when((t == H_BQ) & used | (t == H_BK))
    def _():
        norm_tile(4, False)

    @pl.when((t == H_BV) | (t == H_CU) & used | (t == H_DV))
    def _():
        norm_tile(0, False)

    if not ctx_queries:
        @pl.when(is_query & is_ctx)
        def _():
            z_ref[...] = jnp.zeros_like(z_ref)


def _head_proj(hl, hc, w_in, gain, cos_t, sin_t, bd, w_branch, w_out, *, layer, ctx_queries):
    seq_tiles = SEQ // TM_PROJ
    row_tiles = LATENT_TILES + 1
    rope_block = lambda i: jnp.where(i < LATENT_TILES, i % seq_tiles, seq_tiles)
    w_spec = lambda half: pl.BlockSpec((1, D_MODEL, W_SRC), lambda t, i: (layer, 0, _head_src_block(t, half)))
    cast_in, cast_out, cast_shape = _side_cast_specs((w_branch, w_out), layer=layer, row_tiles=row_tiles)
    return pl.pallas_call(
        functools.partial(_head_kernel, ctx_queries=ctx_queries),
        grid=(HEAD_TILES, row_tiles),
        in_specs=_h_specs(True) + [
            w_spec(0), w_spec(1),
            pl.BlockSpec((1, Z_TILE), lambda t, i: (0, t)),
            pl.BlockSpec((TM_PROJ, LANES), lambda t, i: (rope_block(i), 0)),
            pl.BlockSpec((TM_PROJ, LANES), lambda t, i: (rope_block(i), 0)),
            pl.BlockSpec(bd.shape, lambda t, i: (0, 0)),
        ] + cast_in,
        out_specs=[pl.BlockSpec((None, TM_PROJ, Z_TILE), lambda t, i: (t, i, 0))] + cast_out,
        out_shape=[jax.ShapeDtypeStruct((HEAD_TILES, ALL_ROWS, Z_TILE), BF16)] + cast_shape,
        scratch_shapes=[pltpu.VMEM((D_MODEL, Z_TILE), BF16)],
        compiler_params=_cparams(("arbitrary", "arbitrary")),
        name="head_proj",
    )(hl, hc, w_in, w_in, gain, cos_t, sin_t, bd, w_branch, w_out)


def _softmax_attend(q, segs, extra_logit=None):
    scores = []
    for k, _, bias in segs:
        s = _dot_nt(q, k)
        if bias is not None:
            s = s + bias
        scores.append(s)
    m = scores[0].max(axis=-1, keepdims=True)
    for s in scores[1:]:
        m = jnp.maximum(m, s.max(axis=-1, keepdims=True))
    if extra_logit is not None:
        m = jnp.maximum(m, extra_logit)
    l = None
    o = None
    for s, (_, v, _) in zip(scores, segs):
        p = jnp.exp(s - m)
        ls = p.sum(axis=-1, keepdims=True)
        os_ = _dot(p.astype(BF16), v)
        l = ls if l is None else l + ls
        o = os_ if o is None else o + os_
    if extra_logit is not None:
        l = l + jnp.exp(extra_logit - m)
    return o, l


def _lane_group(shape, width):
    return lax.broadcasted_iota(jnp.int32, shape, 1) // width


def _zspec(tile, rows, index_rows, width=Z_TILE):
    return pl.BlockSpec((None, rows, width), lambda *ids: (tile, index_rows(*ids), 0))


def _repeat_kv_heads(kv):
    pieces = [kv[:, i * HEAD_DIM:(i + 1) * HEAD_DIM] for i in range(2 * A_KV_HEADS)]
    k = jnp.concatenate([pieces[g] for g in range(A_KV_HEADS) for _ in range(A_GROUP)], axis=1)
    v = jnp.concatenate([pieces[A_KV_HEADS + g] for g in range(A_KV_HEADS) for _ in range(A_GROUP)], axis=1)
    return k, v


WIN_BLOCKS_PER_STEP = 8


def _win_gqa_kernel(sink_ref, q_ref, *rest, latent):
    if latent:
        kv_ref, ckv_ref, o_ref, k_scr, v_scr, kc_scr, vc_scr = rest
    else:
        ckv_ref, o_ref, kc_scr, vc_scr = rest
    gw = A_GROUP * HEAD_DIM

    def build():
        kc_scr[...], vc_scr[...] = _repeat_kv_heads(ckv_ref[...])
        if latent:
            k_scr[...], v_scr[...] = _repeat_kv_heads(kv_ref[...])

    if latent:
        pl.when(pl.program_id(1) == 0)(build)
    else:
        build()

    tq = A_BLOCK if latent else q_ref.shape[0]
    n_sub = q_ref.shape[0] // tq
    rows = A_GROUP * tq
    row_head = lax.broadcasted_iota(jnp.int32, (rows, 1), 0) // tq
    lane_head = _lane_group((tq, gw), HEAD_DIM)
    for j in range(n_sub):
        q = q_ref[j * tq:(j + 1) * tq, :] * ATTN_SCALE
        if latent:
            n = pl.program_id(1) * n_sub + j
            span = 3 * A_BLOCK
            start = pl.multiple_of(jnp.clip((n - 1) * A_BLOCK, 0, SEQ - span), A_BLOCK)
            qpos = n * A_BLOCK + lax.broadcasted_iota(jnp.int32, (tq, span), 0)
            kpos = start + lax.broadcasted_iota(jnp.int32, (tq, span), 1)
            band = jnp.where(jnp.abs(qpos - kpos) <= A_WINDOW, 0.0, NEG).astype(F32)
            band = jnp.concatenate([band] * A_GROUP, axis=0)
        for g in range(A_KV_HEADS):
            gs = slice(g * gw, (g + 1) * gw)
            qg = q[:, gs]
            zero = jnp.zeros_like(qg)
            q_stack = jnp.concatenate([jnp.where(lane_head == r, qg, zero) for r in range(A_GROUP)], axis=0)
            sink = jnp.zeros((rows, 1), F32)
            for r in range(A_GROUP):
                sink = jnp.where(row_head == r, sink_ref[g * A_GROUP + r], sink)
            segs = [(kc_scr[:, gs], vc_scr[:, gs], None)]
            if latent:
                segs.append((k_scr[pl.ds(start, span), gs], v_scr[pl.ds(start, span), gs], band))
            o, l = _softmax_attend(q_stack, segs, extra_logit=sink)
            o = o / l
            og = jnp.zeros((tq, gw), F32)
            for r in range(A_GROUP):
                og = jnp.where(lane_head == r, o[r * tq:(r + 1) * tq], og)
            o_ref[j * tq:(j + 1) * tq, gs] = og.astype(BF16)


def _win_gqa(zh, sink, *, latent):
    kvb = 2 * A_KV_HEADS * HEAD_DIM
    c0 = _ctx_blocks(CTX_LEN)
    smem = pl.BlockSpec(memory_space=pltpu.SMEM)
    rep = lambda n: pltpu.VMEM((n, A_HEADS * HEAD_DIM), BF16)
    if latent:
        tq = WIN_BLOCKS_PER_STEP * A_BLOCK
        nq = SEQ // tq
        grid = (BATCH, nq)
        in_specs = [
            smem,
            _zspec(H_AQ, tq, lambda b, n: b * nq + n),
            _zspec(H_AKV, SEQ, lambda b, n: b, kvb),
            _zspec(H_AKV, CTX_LEN, lambda b, n: c0 + b, kvb),
        ]
        out_specs = pl.BlockSpec((tq, Z_TILE), lambda b, n: (b * nq + n, 0))
        args = (sink, zh, zh, zh)
        rows = LATENT_ROWS
        sem = ("parallel", "arbitrary")
        scratch = [rep(SEQ), rep(SEQ), rep(CTX_LEN), rep(CTX_LEN)]
    else:
        grid = (BATCH,)
        in_specs = [
            smem,
            _zspec(H_AQ, CTX_LEN, lambda b: c0 + b),
            _zspec(H_AKV, CTX_LEN, lambda b: c0 + b, kvb),
        ]
        out_specs = pl.BlockSpec((CTX_LEN, Z_TILE), lambda b: (b, 0))
        args = (sink, zh, zh)
        rows = CTX_ROWS
        sem = ("parallel",)
        scratch = [rep(CTX_LEN), rep(CTX_LEN)]
    return pl.pallas_call(
        functools.partial(_win_gqa_kernel, latent=latent),
        grid=grid,
        in_specs=in_specs,
        out_specs=out_specs,
        out_shape=jax.ShapeDtypeStruct((rows, BRANCH_W), BF16),
        scratch_shapes=scratch,
        compiler_params=_cparams(sem),
        name="win_gqa" if latent else "win_gqa_ctx",
    )(*args)


N_ROW_OFFSETS = 2 * B_WIN_ROWS - 1
N_COL_OFFSETS = 2 * B_WIN_COLS - 1
N_PAIR = N_ROW_OFFSETS + 1
NBR_QROWS = 4
NBR_SLAB = NBR_QROWS + B_WIN_ROWS
NBR_GROUPS = GRID_H // NBR_QROWS
NBR_CASES = 3


def _nbr_case_geometry(case):
    if case == 0:
        return B_WIN_ROWS - 1, lambda a, c: 0 <= c < B_WIN_ROWS
    if case == 1:
        return B_WIN_ROWS - 1 - NBR_QROWS, lambda a, c: 0 <= c - a < B_WIN_ROWS
    return B_WIN_ROWS - 1 - 2 * NBR_QROWS, lambda a, c: NBR_QROWS <= c < NBR_SLAB


def _rpb_table_kernel(rpb_ref, o_ref, pair_scr):
    h = pl.program_id(0)
    shape = (GRID_W, LANES)
    qcol = lax.broadcasted_iota(jnp.int32, shape, 0)
    lane = lax.broadcasted_iota(jnp.int32, shape, 1)
    kcol = lane & (GRID_W - 1)
    hi = lane >= GRID_W
    cs = jnp.clip(qcol - B_WIN_COLS // 2, 0, GRID_W - B_WIN_COLS)
    in_window = (kcol >= cs) & (kcol < cs + B_WIN_COLS)
    ci = kcol - qcol + (B_WIN_COLS - 1)
    neg = jnp.full(shape, NEG, F32)

    def rpb(d, c):
        return rpb_ref[(h * N_ROW_OFFSETS + d) * N_COL_OFFSETS + c]

    tables = []
    for d in range(N_ROW_OFFSETS):
        acc = neg
        for c in range(N_COL_OFFSETS):
            acc = jnp.where(ci == c, rpb(d, c), acc)
        tables.append(jnp.where(in_window, acc, NEG))
    for p in range(N_PAIR):
        lo = tables[p - 1] if p >= 1 else neg
        up = tables[p] if p < N_ROW_OFFSETS else neg
        pair_scr[p] = jnp.where(hi, up, lo)

    for case in range(NBR_CASES):
        shift, in_rows = _nbr_case_geometry(case)
        for a in range(NBR_QROWS):
            for cp in range(NBR_SLAB // 2):
                c = 2 * cp
                lo_ok, hi_ok = in_rows(a, c), in_rows(a, c + 1)
                if lo_ok or hi_ok:
                    piece = pair_scr[c - a + shift + 1]
                    if not lo_ok:
                        piece = jnp.where(hi, piece, neg)
                    if not hi_ok:
                        piece = jnp.where(hi, neg, piece)
                else:
                    piece = neg
                o_ref[case, 0, a * GRID_W:(a + 1) * GRID_W, cp * LANES:(cp + 1) * LANES] = piece


def _rpb_table(rpb):
    nq, nk = NBR_QROWS * GRID_W, NBR_SLAB * GRID_W
    return pl.pallas_call(
        _rpb_table_kernel,
        grid=(B_HEADS,),
        in_specs=[pl.BlockSpec(memory_space=pltpu.SMEM)],
        out_specs=pl.BlockSpec((NBR_CASES, 1, nq, nk), lambda h: (0, h, 0, 0)),
        out_shape=jax.ShapeDtypeStruct((NBR_CASES, B_HEADS, nq, nk), F32),
        scratch_shapes=[pltpu.VMEM((N_PAIR, GRID_W, LANES), F32)],
        compiler_params=_cparams(("parallel",)),
        name="rpb_table",
    )(rpb.reshape(-1))


def _pair_attend(q, lane_hi, segs):
    m = q.shape[0]
    zero = jnp.zeros_like(q)
    q_stack = jnp.concatenate([jnp.where(lane_hi, zero, q), jnp.where(lane_hi, q, zero)], axis=0)
    o, l = _softmax_attend(q_stack, segs)
    o = o / l
    return jnp.where(lane_hi, o[m:], o[:m])


NBR_GROUPS_PER_STEP = 4


def _nbr_kernel(q_ref, k_ref, v_ref, kc_ref, vc_ref, bias_ref, o_ref):
    nq = NBR_QROWS * GRID_W
    nk = NBR_SLAB * GRID_W
    lane_hi = lax.broadcasted_iota(jnp.int32, (nq, LANES), 1) >= HEAD_DIM
    for j in range(NBR_GROUPS_PER_STEP):
        g = pl.program_id(1) * NBR_GROUPS_PER_STEP + j
        case = jnp.where(g == 0, 0, jnp.where(g == NBR_GROUPS - 1, 2, 1))
        r0 = jnp.clip(NBR_QROWS * g - B_WIN_ROWS // 2, 0, GRID_H - NBR_SLAB)
        base = pl.multiple_of(r0 * GRID_W, GRID_W)
        rows = slice(j * nq, (j + 1) * nq)
        for pr in range(B_HEADS // 2):
            sl = slice(pr * LANES, (pr + 1) * LANES)
            kp = k_ref[pl.ds(base, nk), sl]
            vp = v_ref[pl.ds(base, nk), sl]
            kcp = kc_ref[:, sl]
            vcp = vc_ref[:, sl]
            bias = bias_ref[case, 2 * pr:2 * pr + 2].reshape(2 * nq, nk)
            segs = [(kp, vp, bias), (kcp, vcp, None)]
            q = q_ref[rows, sl] * ATTN_SCALE
            o_ref[rows, sl] = _pair_attend(q, lane_hi, segs).astype(BF16)


def _nbr_attn(zh, table):
    nq = NBR_GROUPS_PER_STEP * NBR_QROWS * GRID_W
    steps = NBR_GROUPS // NBR_GROUPS_PER_STEP
    c0 = _ctx_blocks(CTX_LEN)
    return pl.pallas_call(
        _nbr_kernel,
        grid=(BATCH, steps),
        in_specs=[
            _zspec(H_BQ, nq, lambda b, s: b * steps + s),
            _zspec(H_BK, SEQ, lambda b, s: b),
            _zspec(H_BV, SEQ, lambda b, s: b),
            _zspec(H_BK, CTX_LEN, lambda b, s: c0 + b),
            _zspec(H_BV, CTX_LEN, lambda b, s: c0 + b),
            pl.BlockSpec(table.shape, lambda b, s: (0, 0, 0, 0), pipeline_mode=pl.Buffered(1)),
        ],
        out_specs=pl.BlockSpec((nq, Z_TILE), lambda b, s: (b * steps + s, 0)),
        out_shape=jax.ShapeDtypeStruct((LATENT_ROWS, BRANCH_W), BF16),
        compiler_params=_cparams(("parallel", "arbitrary")),
        name="nbr_attn",
    )(zh, zh, zh, zh, zh, table)


def _ctx_mha_kernel(q_ref, k_ref, v_ref, o_ref):
    q = q_ref[...] * ATTN_SCALE
    lane_hi = lax.broadcasted_iota(jnp.int32, (q.shape[0], LANES), 1) >= HEAD_DIM
    for pr in range(B_HEADS // 2):
        sl = slice(pr * LANES, (pr + 1) * LANES)
        segs = [(k_ref[:, sl], v_ref[:, sl], None)]
        o_ref[:, sl] = _pair_attend(q[:, sl], lane_hi, segs).astype(BF16)


def _ctx_mha(zh):
    c0 = _ctx_blocks(CTX_LEN)
    rows = lambda b: c0 + b
    return pl.pallas_call(
        _ctx_mha_kernel,
        grid=(BATCH,),
        in_specs=[_zspec(H_BQ, CTX_LEN, rows), _zspec(H_BK, CTX_LEN, rows), _zspec(H_BV, CTX_LEN, rows)],
        out_specs=pl.BlockSpec((CTX_LEN, Z_TILE), lambda b: (b, 0)),
        out_shape=jax.ShapeDtypeStruct((CTX_ROWS, BRANCH_W), BF16),
        compiler_params=_cparams(("parallel",)),
        name="nbr_attn_ctx",
    )(zh, zh, zh)


def _pool_kernel(u_ref, w_ref, scale_ref, o_ref):
    n = u_ref.shape[0]
    t = lax.broadcasted_iota(jnp.int32, (n, C_GROUP_DIM), 0)

    def down(a, k):
        return jnp.where(t >= k, pltpu.roll(a, k, axis=0), 0.0)

    def up(a, k):
        return jnp.where(t < n - k, pltpu.roll(a, n - k, axis=0), 0.0)

    for g, w in enumerate(C_WINDOWS):
        sl = slice(g * C_GROUP_DIM, (g + 1) * C_GROUP_DIM)
        u = u_ref[:, sl].astype(F32)
        half = w // 2
        back = u
        fwd = u
        k = 1
        while k < half:
            back = back + down(back, k)
            fwd = fwd + up(fwd, k)
            k *= 2
        total = down(back, 1) + fwd
        lo = jnp.maximum(t - half, 0)
        hi = jnp.minimum(t + half - 1, n - 1)
        cnt = (hi - lo + 1).astype(F32)
        pooled = total / cnt - u
        y = _dot(pooled.astype(BF16), w_ref[0, g]) * scale_ref[:, sl]
        o_ref[:, sl] = y.astype(BF16)


def _pool(zh, c_w, c_scale, *, layer, n, latent):
    rows = LATENT_ROWS if latent else CTX_ROWS
    b0 = 0 if latent else _ctx_blocks(n)
    return pl.pallas_call(
        _pool_kernel,
        grid=(rows // n,),
        in_specs=[
            _zspec(H_CU, n, lambda b: b0 + b),
            pl.BlockSpec((1, len(C_WINDOWS), C_GROUP_DIM, C_GROUP_DIM), lambda b: (layer, 0, 0, 0)),
            pl.BlockSpec((1, BRANCH_W), lambda b: (0, 0)),
        ],
        out_specs=pl.BlockSpec((n, BRANCH_W), lambda b: (b, 0)),
        out_shape=jax.ShapeDtypeStruct((rows, BRANCH_W), BF16),
        compiler_params=_cparams(("parallel",)),
        name=f"pool_{n}",
    )(zh, c_w, c_scale)


DIFF_ROWS = 256
DIFF_BLOCKS_PER_STEP = 2


def _diff_kernel(lam_ref, q_ref, *rest, lam_init, latent):
    if latent:
        k_ref, v_ref, kc_ref, vc_ref, g_ref, o_ref = rest
    else:
        kc_ref, vc_ref, g_ref, o_ref = rest
    hw = 2 * HEAD_DIM
    dl = lam_ref[...]
    lam = (jnp.exp(jnp.sum(dl[0:1] * dl[1:2], axis=-1, keepdims=True))
           - jnp.exp(jnp.sum(dl[2:3] * dl[3:4], axis=-1, keepdims=True)) + lam_init)
    tq = min(q_ref.shape[0], DIFF_ROWS)
    lane_hi = lax.broadcasted_iota(jnp.int32, (tq, hw), 1) >= HEAD_DIM
    for r0 in range(0, q_ref.shape[0], tq):
        rows = slice(r0, r0 + tq)
        for h in range(D_HEADS):
            sl = slice(h * hw, (h + 1) * hw)
            qh = q_ref[rows, sl] * ATTN_SCALE
            zero = jnp.zeros_like(qh)
            q_stack = jnp.concatenate([jnp.where(lane_hi, zero, qh), jnp.where(lane_hi, qh, zero)], axis=0)
            segs = [(kc_ref[:, sl], vc_ref[:, sl], None)]
            if latent:
                segs.append((k_ref[:, sl], v_ref[:, sl], None))
            o, l = _softmax_attend(q_stack, segs)
            o = o / l
            od = o[:tq] - lam * o[tq:]
            ms = jnp.mean(od * od, axis=-1, keepdims=True)
            y = od * lax.rsqrt(ms + EPS) * g_ref[...] * (1.0 - lam_init)
            o_ref[rows, sl] = y.astype(BF16)


def _diff_attn(zh, d_lambda, subln, *, lam_init, latent):
    full = lambda shape: pl.BlockSpec(shape, lambda *_: (0,) * len(shape))
    c0 = _ctx_blocks(CTX_LEN)
    if latent:
        tq = DIFF_BLOCKS_PER_STEP * DIFF_ROWS
        nq = SEQ // tq
        grid = (BATCH, nq)
        in_specs = [
            full((4, HEAD_DIM)),
            _zspec(H_DQ, tq, lambda b, n: b * nq + n),
            _zspec(H_DK, SEQ, lambda b, n: b),
            _zspec(H_DV, SEQ, lambda b, n: b),
            _zspec(H_DK, CTX_LEN, lambda b, n: c0 + b),
            _zspec(H_DV, CTX_LEN, lambda b, n: c0 + b),
            full((1, 2 * HEAD_DIM)),
        ]
        out_specs = pl.BlockSpec((tq, Z_TILE), lambda b, n: (b * nq + n, 0))
        args = (d_lambda, zh, zh, zh, zh, zh, subln)
        rows = LATENT_ROWS
        sem = ("parallel", "arbitrary")
    else:
        grid = (BATCH,)
        crow = lambda b: c0 + b
        in_specs = [
            full((4, HEAD_DIM)),
            _zspec(H_DQ, CTX_LEN, crow),
            _zspec(H_DK, CTX_LEN, crow),
            _zspec(H_DV, CTX_LEN, crow),
            full((1, 2 * HEAD_DIM)),
        ]
        out_specs = pl.BlockSpec((CTX_LEN, Z_TILE), lambda b: (b, 0))
        args = (d_lambda, zh, zh, zh, subln)
        rows = CTX_ROWS
        sem = ("parallel",)
    return pl.pallas_call(
        functools.partial(_diff_kernel, lam_init=lam_init, latent=latent),
        grid=grid,
        in_specs=in_specs,
        out_specs=out_specs,
        out_shape=jax.ShapeDtypeStruct((rows, BRANCH_W), BF16),
        compiler_params=_cparams(sem),
        name="diff_attn" if latent else "diff_attn_ctx",
    )(*args)


def _mix_kernel(*refs):
    y_refs = refs[:N_BRANCH]
    g_refs = refs[N_BRANCH:2 * N_BRANCH]
    wb_ref, o_ref = refs[2 * N_BRANCH:]
    mixed = None
    for k in range(N_BRANCH):
        gate = 0.5 * (jnp.tanh(0.5 * g_refs[k][...].astype(F32)) + 1.0)
        proj = _dot(y_refs[k][...], wb_ref[0, k]) * gate
        mixed = proj if mixed is None else mixed + proj
    o_ref[...] = mixed.astype(BF16)


def _out_kernel(mixed_ref, wo_ref, x_ref, gt_ref, gn_ref, shift_ref, scale_ref, xo_ref, h_ref):
    xo_ref[...] = x_ref[...] + gt_ref[0] * _dot(mixed_ref[...], wo_ref[0])
    _norm_modulate_rows(xo_ref, gn_ref, shift_ref, scale_ref, h_ref)


def _merge(ys, gates, w_branch, w_out, x, gn2, mod3, *, tm, row0, mod_row):
    m = x.shape[0]
    row = lambda i: (i, 0)
    g0 = row0 // tm
    mod = lambda chunk: pl.BlockSpec((1, 1, D_MODEL), lambda i: (mod_row(i), 0, chunk))
    resident = pl.Buffered(1)
    mixed = pl.pallas_call(
        _mix_kernel,
        grid=(m // tm,),
        in_specs=[pl.BlockSpec((tm, BRANCH_W), row)] * N_BRANCH
        + [pl.BlockSpec((tm, D_MODEL), lambda i, k=k: (g0 + i, k)) for k in range(N_BRANCH)]
        + [pl.BlockSpec((1, N_BRANCH, BRANCH_W, D_MODEL), lambda i: (0, 0, 0, 0), pipeline_mode=resident)],
        out_specs=pl.BlockSpec((tm, D_MODEL), row),
        out_shape=jax.ShapeDtypeStruct((m, D_MODEL), BF16),
        compiler_params=_cparams(("parallel",)),
        name=f"mix_{m}",
    )(*ys, *([gates] * N_BRANCH), w_branch)
    return pl.pallas_call(
        _out_kernel,
        grid=(m // tm,),
        in_specs=[
            pl.BlockSpec((tm, D_MODEL), row),
            pl.BlockSpec((1, D_MODEL, D_MODEL), lambda i: (0, 0, 0), pipeline_mode=resident),
            pl.BlockSpec((tm, D_MODEL), row),
            mod(2),
            pl.BlockSpec((1, D_MODEL), lambda i: (0, 0)),
            mod(3),
            mod(4),
        ],
        out_specs=[pl.BlockSpec((tm, D_MODEL), row), pl.BlockSpec((tm, D_MODEL), row)],
        out_shape=[jax.ShapeDtypeStruct((m, D_MODEL), F32), jax.ShapeDtypeStruct((m, D_MODEL), BF16)],
        compiler_params=_cparams(("parallel",)),
        name=f"out_{m}",
    )(mixed, w_out, x, mod3, gn2, mod3, mod3)


def _mlp_kernel(h_ref, x_ref, gt_ref, w1_ref, w2_ref, *rest, emit_next):
    if emit_next:
        gn_ref, shift_ref, scale_ref, xo_ref, hn_ref = rest
    else:
        (xo_ref,) = rest
    f = pl.program_id(1)

    @pl.when(f == 0)
    def _():
        xo_ref[...] = jnp.zeros_like(xo_ref)

    a = jnp.maximum(_dot(h_ref[...], w1_ref[0]), 0.0)
    xo_ref[...] += _dot((a * a).astype(BF16), w2_ref[0])

    @pl.when(f == pl.num_programs(1) - 1)
    def _():
        xo_ref[...] = x_ref[...] + gt_ref[0] * xo_ref[...]
        if emit_next:
            _norm_modulate_rows(xo_ref, gn_ref, shift_ref, scale_ref, hn_ref)


def _mlp(h, x, mod3, w1, w2, next_norm, *, tm, tf, mod_row):
    m = x.shape[0]
    emit_next = next_norm is not None
    row = lambda i, f: (i, 0)
    in_specs = [
        pl.BlockSpec((tm, D_MODEL), row),
        pl.BlockSpec((tm, D_MODEL), row),
        pl.BlockSpec((1, 1, D_MODEL), lambda i, f: (mod_row(i), 0, 5)),
        pl.BlockSpec((1, D_MODEL, tf), lambda i, f: (0, 0, f)),
        pl.BlockSpec((1, tf, D_MODEL), lambda i, f: (0, f, 0)),
    ]
    args = [h, x, mod3, w1, w2]
    out_specs = [pl.BlockSpec((tm, D_MODEL), row)]
    out_shape = [jax.ShapeDtypeStruct((m, D_MODEL), F32)]
    if emit_next:
        gn_next, mod3_next = next_norm
        in_specs += [
            pl.BlockSpec((1, D_MODEL), lambda i, f: (0, 0)),
            pl.BlockSpec((1, 1, D_MODEL), lambda i, f: (mod_row(i), 0, 0)),
            pl.BlockSpec((1, 1, D_MODEL), lambda i, f: (mod_row(i), 0, 1)),
        ]
        args += [gn_next, mod3_next, mod3_next]
        out_specs.append(pl.BlockSpec((tm, D_MODEL), row))
        out_shape.append(jax.ShapeDtypeStruct((m, D_MODEL), BF16))
    outs = pl.pallas_call(
        functools.partial(_mlp_kernel, emit_next=emit_next),
        grid=(m // tm, D_FF // tf),
        in_specs=in_specs,
        out_specs=out_specs,
        out_shape=out_shape,
        compiler_params=_cparams(("parallel", "arbitrary")),
        name=f"mlp_{m}",
    )(*args)
    return outs if emit_next else (outs[0], None)


def _rope_tables():
    t = jnp.arange(SEQ)
    row = (t // GRID_W).astype(F32)
    col = (t % GRID_W).astype(F32)
    n_freq = HEAD_DIM // 4
    inv = ROPE_BASE ** (-jnp.arange(n_freq, dtype=F32) / n_freq)
    ar = row[:, None] * inv
    ac = col[:, None] * inv
    cos_h = jnp.concatenate([jnp.cos(ar), jnp.cos(ar), jnp.cos(ac), jnp.cos(ac)], axis=-1)
    sin_h = jnp.concatenate([-jnp.sin(ar), jnp.sin(ar), -jnp.sin(ac), jnp.sin(ac)], axis=-1)
    reps = LANES // HEAD_DIM
    cos_t = jnp.concatenate([jnp.tile(cos_h, (1, reps)), jnp.ones((TM_PROJ, LANES), F32)], axis=0)
    sin_t = jnp.concatenate([jnp.tile(sin_h, (1, reps)), jnp.zeros((TM_PROJ, LANES), F32)], axis=0)
    return cos_t, sin_t


def _head_gain_row(qk_gain_l):
    ones = jnp.ones((Z_TILE,), F32)
    rep = lambda g: jnp.tile(g, Z_TILE // HEAD_DIM)
    akv = jnp.concatenate([jnp.tile(qk_gain_l[0, 1], A_KV_HEADS), jnp.ones((Z_TILE - A_KV_HEADS * HEAD_DIM,), F32)])
    return jnp.concatenate([
        rep(qk_gain_l[0, 0]), akv,
        rep(qk_gain_l[1, 0]), rep(qk_gain_l[1, 1]), ones, ones,
        rep(qk_gain_l[2, 0]), rep(qk_gain_l[2, 1]), ones,
    ])[None, :]


def kernel(x, c, ctx, c_ctx, w_ada, b_ada, g_norm1, g_norm2, w_in, b_gate, qk_gain, a_sink, b_rpb,
           c_w, c_scale, d_lambda, d_subln, w_branch, w_out, w_ff1, w_ff2):
    cos_t, sin_t = _rope_tables()
    lane = jnp.arange(HEAD_SUM_LANES)
    bd = (lane[:, None] // HEAD_DIM == lane[None, :] // HEAD_DIM).astype(BF16)

    cc = jnp.concatenate([c, c_ctx[None, :], jnp.zeros((8 - BATCH - 1, D_MODEL), F32)], axis=0)
    mod_all = _modulation(cc, w_ada, b_ada)

    cw = c_w.astype(BF16)
    w_branch2d = w_branch.reshape(DEPTH, N_BRANCH * BRANCH_W, D_MODEL)

    xl = x.reshape(LATENT_ROWS, D_MODEL)
    xc = ctx.reshape(CTX_ROWS, D_MODEL)
    ctx_row = lambda i: BATCH
    batch_row = lambda tm: (lambda i: i // (SEQ // tm))
    mods = [mod_all[l].reshape(8, 1, 6 * D_MODEL) for l in range(DEPTH)]
    gn1 = [g_norm1[l][None, :] for l in range(DEPTH)]

    tm_merge, tm_mlp, tf_mlp, tn_gate = TM_MERGE, TM_MLP, TF_MLP, TN_GATE
    hl = _norm_mod(xl, gn1[0], mods[0], tm=tm_mlp, mod_row=batch_row(tm_mlp))
    hc = _norm_mod(xc, gn1[0], mods[0], tm=tm_mlp, mod_row=ctx_row)

    for l in range(DEPTH):
        last = l == DEPTH - 1
        lam_init = 0.8 - 0.6 * math.exp(-0.3 * l)
        mod3 = mods[l]
        gn2 = g_norm2[l][None, :]
        cs = c_scale[l][None, :]
        subln = d_subln[l][None, :]
        next_norm = None if last else (gn1[l + 1], mods[l + 1])

        gates, w1, w2 = _gate_proj(hl, None if last else hc, w_in, b_gate[l][None, :], w_ff1, w_ff2,
                                   layer=l, tn=tn_gate)
        zh, wb, wo = _head_proj(hl, hc, w_in, _head_gain_row(qk_gain[l]), cos_t, sin_t, bd, w_branch2d, w_out,
                                layer=l, ctx_queries=not last)
        wb = wb.reshape(1, N_BRANCH, BRANCH_W, D_MODEL)

        table = _rpb_table(b_rpb[l])
        ya = _win_gqa(zh, a_sink[l], latent=True)
        yb = _nbr_attn(zh, table)
        yc = _pool(zh, cw, cs, layer=l, n=SEQ, latent=True)
        yd = _diff_attn(zh, d_lambda[l], subln, lam_init=lam_init, latent=True)
        xl, h2 = _merge((ya, yb, yc, yd), gates, wb, wo, xl, gn2, mod3,
                        tm=tm_merge, row0=0, mod_row=batch_row(tm_merge))
        xl, hl = _mlp(h2, xl, mod3, w1, w2, next_norm, tm=tm_mlp, tf=tf_mlp, mod_row=batch_row(tm_mlp))
        if not last:
            yac = _win_gqa(zh, a_sink[l], latent=False)
            ybc = _ctx_mha(zh)
            ycc = _pool(zh, cw, cs, layer=l, n=CTX_LEN, latent=False)
            ydc = _diff_attn(zh, d_lambda[l], subln, lam_init=lam_init, latent=False)
            xc, h2c = _merge((yac, ybc, ycc, ydc), gates, wb, wo, xc, gn2, mod3,
                             tm=tm_merge, row0=LATENT_ROWS, mod_row=ctx_row)
            xc, hc = _mlp(h2c, xc, mod3, w1, w2, next_norm, tm=tm_mlp, tf=tf_mlp, mod_row=ctx_row)

    return xl.reshape(BATCH, SEQ, D_MODEL)
```

```python
import functools
import math

import jax
import jax.numpy as jnp
from jax import lax
from jax.experimental import pallas as pl
from jax.experimental.pallas import tpu as pltpu

D_MODEL = 2048
BATCH = 4
SEQ = 2048
DEPTH = 2
GRID_W = 64
GRID_H = SEQ // GRID_W
CTX_LEN = 256
HEAD_DIM = 64
N_BRANCH = 4
BRANCH_W = D_MODEL // N_BRANCH
A_HEADS = 8
A_KV_HEADS = 2
A_GROUP = A_HEADS // A_KV_HEADS
A_WINDOW = 128
A_BLOCK = 128
B_HEADS = 8
B_WIN_ROWS = 8
B_WIN_COLS = 16
C_WINDOWS = (2, 4, 8, 16)
C_GROUP_DIM = 128
D_HEADS = 4
D_FF = 4 * D_MODEL
ROPE_BASE = 10000.0
EPS = 1e-6
NEG = -1e30
ATTN_SCALE = HEAD_DIM ** -0.5

F32 = jnp.float32
BF16 = jnp.bfloat16

V7X_VMEM_LIMIT_BYTES = 56 * 1024 * 1024
LANES = 128

LATENT_ROWS = BATCH * SEQ
CTX_ROWS = BATCH * CTX_LEN
ALL_ROWS = LATENT_ROWS + CTX_ROWS

Z_TILE = 512
H_AQ, H_AKV, H_BQ, H_BK, H_BV, H_CU, H_DQ, H_DK, H_DV = range(9)
HEAD_TILES = 9
W_SRC = 256
GATE_COLS = N_BRANCH * D_MODEL
GATE_SRC0 = (12544 - GATE_COLS) // W_SRC
HEAD_SUM_LANES = 256

TM_PROJ = 1024
TN_GATE = 1024
TM_MERGE = 256
TM_MLP = 512
TF_MLP = 1024


def _cparams(sem):
    return pltpu.CompilerParams(dimension_semantics=sem, vmem_limit_bytes=V7X_VMEM_LIMIT_BYTES)


def _dot_nt(a, b):
    return lax.dot_general(a, b, (((1,), (1,)), ((), ())), preferred_element_type=F32)


def _dot(a, b):
    return jnp.dot(a, b, preferred_element_type=F32)


def _ctx_blocks(block_rows):
    return LATENT_ROWS // block_rows


def _mod_kernel(c_ref, w_ref, b_ref, o_ref):
    c = c_ref[...]
    s = c * (0.5 * (jnp.tanh(0.5 * c) + 1.0))
    o_ref[0] = _dot(s, w_ref[0]) + b_ref[0]


def _modulation(cc, w_ada, b_ada):
    tn = 1024
    n = 6 * D_MODEL
    return pl.pallas_call(
        _mod_kernel,
        grid=(DEPTH, n // tn),
        in_specs=[
            pl.BlockSpec((8, D_MODEL), lambda l, j: (0, 0)),
            pl.BlockSpec((1, D_MODEL, tn), lambda l, j: (l, 0, j)),
            pl.BlockSpec((1, 1, tn), lambda l, j: (l, 0, j)),
        ],
        out_specs=pl.BlockSpec((1, 8, tn), lambda l, j: (l, 0, j)),
        out_shape=jax.ShapeDtypeStruct((DEPTH, 8, n), F32),
        compiler_params=_cparams(("parallel", "parallel")),
        name="modulation",
    )(cc, w_ada, b_ada.reshape(DEPTH, 1, n))


NORM_ROWS = 16


def _norm_modulate_rows(x_ref, gn_ref, shift_ref, scale_ref, h_ref):
    n = x_ref.shape[0]
    gs = gn_ref[...] * (1.0 + scale_ref[0])
    shift = shift_ref[0]

    def body(c, carry):
        rows = pl.ds(pl.multiple_of(c * NORM_ROWS, NORM_ROWS), NORM_ROWS)
        x = x_ref[rows, :]
        inv = lax.rsqrt(jnp.mean(x * x, axis=-1, keepdims=True) + EPS)
        h_ref[rows, :] = ((x * inv) * gs + shift).astype(BF16)
        return carry

    lax.fori_loop(0, n // NORM_ROWS, body, 0, unroll=8)


def _norm_mod_kernel(x_ref, gn_ref, shift_ref, scale_ref, h_ref):
    _norm_modulate_rows(x_ref, gn_ref, shift_ref, scale_ref, h_ref)


def _norm_mod(x, gn, mod3, *, tm, mod_row):
    m = x.shape[0]
    return pl.pallas_call(
        _norm_mod_kernel,
        grid=(m // tm,),
        in_specs=[
            pl.BlockSpec((tm, D_MODEL), lambda i: (i, 0)),
            pl.BlockSpec((1, D_MODEL), lambda i: (0, 0)),
            pl.BlockSpec((1, 1, D_MODEL), lambda i: (mod_row(i), 0, 0)),
            pl.BlockSpec((1, 1, D_MODEL), lambda i: (mod_row(i), 0, 1)),
        ],
        out_specs=pl.BlockSpec((tm, D_MODEL), lambda i: (i, 0)),
        out_shape=jax.ShapeDtypeStruct((m, D_MODEL), BF16),
        compiler_params=_cparams(("parallel",)),
        name=f"norm_mod_{m}",
    )(x, gn, mod3, mod3)


def _swap16(y):
    lane = lax.broadcasted_iota(jnp.int32, y.shape, 1)
    fwd = pltpu.roll(y, LANES - 16, axis=1)
    bwd = pltpu.roll(y, 16, axis=1)
    return jnp.where((lane & 16) == 0, fwd, bwd)


def _cast_weight_blocks(w_refs, w_scr):
    @pl.when(pl.program_id(1) == 0)
    def _():
        for c, w_ref in enumerate(w_refs):
            w_scr[:, c * W_SRC:(c + 1) * W_SRC] = w_ref[0].astype(BF16)


LATENT_TILES = LATENT_ROWS // TM_PROJ
CAST_CHUNKS = 64


def _row_tile(hl_ref, hc_ref, rows=slice(None)):
    if hc_ref is None:
        return hl_ref[rows, :]
    return jnp.where(pl.program_id(1) >= LATENT_TILES, hc_ref[rows, :], hl_ref[rows, :])


def _h_specs(with_ctx):
    specs = [pl.BlockSpec((TM_PROJ, D_MODEL), lambda n, i: (jnp.minimum(i, LATENT_TILES - 1), 0))]
    if with_ctx:
        specs.append(pl.BlockSpec((TM_PROJ, D_MODEL), lambda n, i: (0, 0), pipeline_mode=pl.Buffered(1)))
    return specs


def _side_cast_specs(arrays, *, layer, row_tiles):
    chunk = lambda n, i: jnp.minimum(n * row_tiles + i, CAST_CHUNKS - 1)
    in_specs, out_specs, out_shape = [], [], []
    for a in arrays:
        rows, cols = a.shape[1] // CAST_CHUNKS, a.shape[2]
        in_specs.append(pl.BlockSpec((1, rows, cols), lambda n, i: (layer, chunk(n, i), 0)))
        out_specs.append(pl.BlockSpec((1, rows, cols), lambda n, i: (0, chunk(n, i), 0)))
        out_shape.append(jax.ShapeDtypeStruct((1,) + a.shape[1:], BF16))
    return in_specs, out_specs, out_shape


def _gate_kernel(*refs, with_ctx):
    hl_ref = refs[0]
    hc_ref = refs[1] if with_ctx else None
    *w_refs, bias_ref, f1_ref, f2_ref, g_ref, f1o_ref, f2o_ref, w_scr = refs[1 + with_ctx:]
    _cast_weight_blocks(w_refs, w_scr)
    f1o_ref[...] = f1_ref[...].astype(BF16)
    f2o_ref[...] = f2_ref[...].astype(BF16)
    g_ref[...] = (_dot(_row_tile(hl_ref, hc_ref), w_scr[...]) + bias_ref[...]).astype(BF16)


def _gate_proj(hl, hc, w_in, bias, w_ff1, w_ff2, *, layer, tn):
    with_ctx = hc is not None
    row_tiles = LATENT_TILES + with_ctx
    n_src = tn // W_SRC
    w_spec = lambda c: pl.BlockSpec((1, D_MODEL, W_SRC), lambda n, i: (layer, 0, GATE_SRC0 + n * n_src + c))
    cast_in, cast_out, cast_shape = _side_cast_specs((w_ff1, w_ff2), layer=layer, row_tiles=row_tiles)
    h_args = (hl, hc) if with_ctx else (hl,)
    return pl.pallas_call(
        functools.partial(_gate_kernel, with_ctx=with_ctx),
        grid=(GATE_COLS // tn, row_tiles),
        in_specs=_h_specs(with_ctx)
        + [w_spec(c) for c in range(n_src)]
        + [pl.BlockSpec((1, tn), lambda n, i: (0, n))]
        + cast_in,
        out_specs=[pl.BlockSpec((TM_PROJ, tn), lambda n, i: (i, n))] + cast_out,
        out_shape=[jax.ShapeDtypeStruct((row_tiles * TM_PROJ, GATE_COLS), BF16)] + cast_shape,
        scratch_shapes=[pltpu.VMEM((D_MODEL, tn), BF16)],
        compiler_params=_cparams(("arbitrary", "arbitrary")),
        name="gate_proj",
    )(*h_args, *([w_in] * n_src), bias, w_ff1, w_ff2)


def _head_src_block(t, half):
    return jnp.where(t == H_AQ, half, jnp.where(t == H_AKV, 2, 2 * t - 1 + half))


def _head_kernel(hl_ref, hc_ref, wa_ref, wb_ref, gain_ref, cos_ref, sin_ref, bd_ref, s1_ref, s2_ref,
                 z_ref, s1o_ref, s2o_ref, w_scr, *, ctx_queries):
    t = pl.program_id(0)
    _cast_weight_blocks((wa_ref, wb_ref), w_scr)
    s1o_ref[...] = s1_ref[...].astype(BF16)
    s2o_ref[...] = s2_ref[...].astype(BF16)
    is_ctx = pl.program_id(1) >= LATENT_TILES
    used = True if ctx_queries else ~is_ctx

    def norm_tile(n_norm_chunks, with_rope):
        wide = bd_ref.shape[0]
        half_rows = TM_PROJ // 2
        for r0 in range(0, TM_PROJ, half_rows):
            rows = slice(r0, r0 + half_rows)
            acc = _dot(_row_tile(hl_ref, hc_ref, rows), w_scr[...])
            for c2 in range(Z_TILE // wide):
                a = acc[:, c2 * wide:(c2 + 1) * wide]
                normed = a
                if c2 * (wide // LANES) < n_norm_chunks:
                    ms = _dot((a * a).astype(BF16), bd_ref[...]) * (1.0 / HEAD_DIM)
                    normed = a * lax.rsqrt(ms + EPS) * gain_ref[:, c2 * wide:(c2 + 1) * wide]
                for c in range(wide // LANES):
                    chunk = c2 * (wide // LANES) + c
                    y = (normed if chunk < n_norm_chunks else a)[:, c * LANES:(c + 1) * LANES]
                    if with_rope and chunk < n_norm_chunks:
                        y = y * cos_ref[rows, :] + _swap16(y) * sin_ref[rows, :]
                    z_ref[rows, chunk * LANES:(chunk + 1) * LANES] = y.astype(BF16)

    is_query = (t == H_AQ) | (t == H_BQ) | (t == H_DQ) | (t == H_CU)

    @pl.when(((t == H_AQ) | (t == H_DQ)) & used | (t == H_DK))
    def _():
        norm_tile(4, True)

    @pl.when(t == H_AKV)
    def _():
        norm_tile(1, True)

    @pl.when((t == H_BQ) & used | (t == H_BK))
    def _():
        norm_tile(4, False)

    @pl.when((t == H_BV) | (t == H_CU) & used | (t == H_DV))
    def _():
        norm_tile(0, False)

    if not ctx_queries:
        @pl.when(is_query & is_ctx)
        def _():
            z_ref[...] = jnp.zeros_like(z_ref)


def _head_proj(hl, hc, w_in, gain, cos_t, sin_t, bd, w_branch, w_out, *, layer, ctx_queries):
    seq_tiles = SEQ // TM_PROJ
    row_tiles = LATENT_TILES + 1
    rope_block = lambda i: jnp.where(i < LATENT_TILES, i % seq_tiles, seq_tiles)
    w_spec = lambda half: pl.BlockSpec((1, D_MODEL, W_SRC), lambda t, i: (layer, 0, _head_src_block(t, half)))
    cast_in, cast_out, cast_shape = _side_cast_specs((w_branch, w_out), layer=layer, row_tiles=row_tiles)
    return pl.pallas_call(
        functools.partial(_head_kernel, ctx_queries=ctx_queries),
        grid=(HEAD_TILES, row_tiles),
        in_specs=_h_specs(True) + [
            w_spec(0), w_spec(1),
            pl.BlockSpec((1, Z_TILE), lambda t, i: (0, t)),
            pl.BlockSpec((TM_PROJ, LANES), lambda t, i: (rope_block(i), 0)),
            pl.BlockSpec((TM_PROJ, LANES), lambda t, i: (rope_block(i), 0)),
            pl.BlockSpec(bd.shape, lambda t, i: (0, 0)),
        ] + cast_in,
        out_specs=[pl.BlockSpec((None, TM_PROJ, Z_TILE), lambda t, i: (t, i, 0))] + cast_out,
        out_shape=[jax.ShapeDtypeStruct((HEAD_TILES, ALL_ROWS, Z_TILE), BF16)] + cast_shape,
        scratch_shapes=[pltpu.VMEM((D_MODEL, Z_TILE), BF16)],
        compiler_params=_cparams(("arbitrary", "arbitrary")),
        name="head_proj",
    )(hl, hc, w_in, w_in, gain, cos_t, sin_t, bd, w_branch, w_out)


def _softmax_attend(q, segs, extra_logit=None):
    scores = []
    for k, _, bias in segs:
        s = _dot_nt(q, k)
        if bias is not None:
            s = s + bias
        scores.append(s)
    m = scores[0].max(axis=-1, keepdims=True)
    for s in scores[1:]:
        m = jnp.maximum(m, s.max(axis=-1, keepdims=True))
    if extra_logit is not None:
        m = jnp.maximum(m, extra_logit)
    l = None
    o = None
    for s, (_, v, _) in zip(scores, segs):
        p = jnp.exp(s - m)
        ls = p.sum(axis=-1, keepdims=True)
        os_ = _dot(p.astype(BF16), v)
        l = ls if l is None else l + ls
        o = os_ if o is None else o + os_
    if extra_logit is not None:
        l = l + jnp.exp(extra_logit - m)
    return o, l


def _lane_group(shape, width):
    return lax.broadcasted_iota(jnp.int32, shape, 1) // width


def _zspec(tile, rows, index_rows, width=Z_TILE):
    return pl.BlockSpec((None, rows, width), lambda *ids: (tile, index_rows(*ids), 0))


def _repeat_kv_heads(kv):
    pieces = [kv[:, i * HEAD_DIM:(i + 1) * HEAD_DIM] for i in range(2 * A_KV_HEADS)]
    k = jnp.concatenate([pieces[g] for g in range(A_KV_HEADS) for _ in range(A_GROUP)], axis=1)
    v = jnp.concatenate([pieces[A_KV_HEADS + g] for g in range(A_KV_HEADS) for _ in range(A_GROUP)], axis=1)
    return k, v


WIN_BLOCKS_PER_STEP = 8


def _win_gqa_kernel(sink_ref, q_ref, *rest, latent):
    if latent:
        kv_ref, ckv_ref, o_ref, k_scr, v_scr, kc_scr, vc_scr = rest
    else:
        ckv_ref, o_ref, kc_scr, vc_scr = rest
    gw = A_GROUP * HEAD_DIM

    def build():
        kc_scr[...], vc_scr[...] = _repeat_kv_heads(ckv_ref[...])
        if latent:
            k_scr[...], v_scr[...] = _repeat_kv_heads(kv_ref[...])

    if latent:
        pl.when(pl.program_id(1) == 0)(build)
    else:
        build()

    tq = A_BLOCK if latent else q_ref.shape[0]
    n_sub = q_ref.shape[0] // tq
    rows = A_GROUP * tq
    row_head = lax.broadcasted_iota(jnp.int32, (rows, 1), 0) // tq
    lane_head = _lane_group((tq, gw), HEAD_DIM)
    for j in range(n_sub):
        q = q_ref[j * tq:(j + 1) * tq, :] * ATTN_SCALE
        if latent:
            n = pl.program_id(1) * n_sub + j
            span = 3 * A_BLOCK
            start = pl.multiple_of(jnp.clip((n - 1) * A_BLOCK, 0, SEQ - span), A_BLOCK)
            qpos = n * A_BLOCK + lax.broadcasted_iota(jnp.int32, (tq, span), 0)
            kpos = start + lax.broadcasted_iota(jnp.int32, (tq, span), 1)
            band = jnp.where(jnp.abs(qpos - kpos) <= A_WINDOW, 0.0, NEG).astype(F32)
            band = jnp.concatenate([band] * A_GROUP, axis=0)
        for g in range(A_KV_HEADS):
            gs = slice(g * gw, (g + 1) * gw)
            qg = q[:, gs]
            zero = jnp.zeros_like(qg)
            q_stack = jnp.concatenate([jnp.where(lane_head == r, qg, zero) for r in range(A_GROUP)], axis=0)
            sink = jnp.zeros((rows, 1), F32)
            for r in range(A_GROUP):
                sink = jnp.where(row_head == r, sink_ref[g * A_GROUP + r], sink)
            segs = [(kc_scr[:, gs], vc_scr[:, gs], None)]
            if latent:
                segs.append((k_scr[pl.ds(start, span), gs], v_scr[pl.ds(start, span), gs], band))
            o, l = _softmax_attend(q_stack, segs, extra_logit=sink)
            o = o / l
            og = jnp.zeros((tq, gw), F32)
            for r in range(A_GROUP):
                og = jnp.where(lane_head == r, o[r * tq:(r + 1) * tq], og)
            o_ref[j * tq:(j + 1) * tq, gs] = og.astype(BF16)


def _win_gqa(zh, sink, *, latent):
    kvb = 2 * A_KV_HEADS * HEAD_DIM
    c0 = _ctx_blocks(CTX_LEN)
    smem = pl.BlockSpec(memory_space=pltpu.SMEM)
    rep = lambda n: pltpu.VMEM((n, A_HEADS * HEAD_DIM), BF16)
    if latent:
        tq = WIN_BLOCKS_PER_STEP * A_BLOCK
        nq = SEQ // tq
        grid = (BATCH, nq)
        in_specs = [
            smem,
            _zspec(H_AQ, tq, lambda b, n: b * nq + n),
            _zspec(H_AKV, SEQ, lambda b, n: b, kvb),
            _zspec(H_AKV, CTX_LEN, lambda b, n: c0 + b, kvb),
        ]
        out_specs = pl.BlockSpec((tq, Z_TILE), lambda b, n: (b * nq + n, 0))
        args = (sink, zh, zh, zh)
        rows = LATENT_ROWS
        sem = ("parallel", "arbitrary")
        scratch = [rep(SEQ), rep(SEQ), rep(CTX_LEN), rep(CTX_LEN)]
    else:
        grid = (BATCH,)
        in_specs = [
            smem,
            _zspec(H_AQ, CTX_LEN, lambda b: c0 + b),
            _zspec(H_AKV, CTX_LEN, lambda b: c0 + b, kvb),
        ]
        out_specs = pl.BlockSpec((CTX_LEN, Z_TILE), lambda b: (b, 0))
        args = (sink, zh, zh)
        rows = CTX_ROWS
        sem = ("parallel",)
        scratch = [rep(CTX_LEN), rep(CTX_LEN)]
    return pl.pallas_call(
        functools.partial(_win_gqa_kernel, latent=latent),
        grid=grid,
        in_specs=in_specs,
        out_specs=out_specs,
        out_shape=jax.ShapeDtypeStruct((rows, BRANCH_W), BF16),
        scratch_shapes=scratch,
        compiler_params=_cparams(sem),
        name="win_gqa" if latent else "win_gqa_ctx",
    )(*args)


N_ROW_OFFSETS = 2 * B_WIN_ROWS - 1
N_COL_OFFSETS = 2 * B_WIN_COLS - 1
N_PAIR = N_ROW_OFFSETS + 1
NBR_QROWS = 4
NBR_SLAB = NBR_QROWS + B_WIN_ROWS
NBR_GROUPS = GRID_H // NBR_QROWS
NBR_CASES = 3


def _nbr_case_geometry(case):
    if case == 0:
        return B_WIN_ROWS - 1, lambda a, c: 0 <= c < B_WIN_ROWS
    if case == 1:
        return B_WIN_ROWS - 1 - NBR_QROWS, lambda a, c: 0 <= c - a < B_WIN_ROWS
    return B_WIN_ROWS - 1 - 2 * NBR_QROWS, lambda a, c: NBR_QROWS <= c < NBR_SLAB


def _rpb_table_kernel(rpb_ref, o_ref, pair_scr):
    h = pl.program_id(0)
    shape = (GRID_W, LANES)
    qcol = lax.broadcasted_iota(jnp.int32, shape, 0)
    lane = lax.broadcasted_iota(jnp.int32, shape, 1)
    kcol = lane & (GRID_W - 1)
    hi = lane >= GRID_W
    cs = jnp.clip(qcol - B_WIN_COLS // 2, 0, GRID_W - B_WIN_COLS)
    in_window = (kcol >= cs) & (kcol < cs + B_WIN_COLS)
    ci = kcol - qcol + (B_WIN_COLS - 1)
    neg = jnp.full(shape, NEG, F32)

    def rpb(d, c):
        return rpb_ref[(h * N_ROW_OFFSETS + d) * N_COL_OFFSETS + c]

    tables = []
    for d in range(N_ROW_OFFSETS):
        acc = neg
        for c in range(N_COL_OFFSETS):
            acc = jnp.where(ci == c, rpb(d, c), acc)
        tables.append(jnp.where(in_window, acc, NEG))
    for p in range(N_PAIR):
        lo = tables[p - 1] if p >= 1 else neg
        up = tables[p] if p < N_ROW_OFFSETS else neg
        pair_scr[p] = jnp.where(hi, up, lo)

    for case in range(NBR_CASES):
        shift, in_rows = _nbr_case_geometry(case)
        for a in range(NBR_QROWS):
            for cp in range(NBR_SLAB // 2):
                c = 2 * cp
                lo_ok, hi_ok = in_rows(a, c), in_rows(a, c + 1)
                if lo_ok or hi_ok:
                    piece = pair_scr[c - a + shift + 1]
                    if not lo_ok:
                        piece = jnp.where(hi, piece, neg)
                    if not hi_ok:
                        piece = jnp.where(hi, neg, piece)
                else:
                    piece = neg
                o_ref[case, 0, a * GRID_W:(a + 1) * GRID_W, cp * LANES:(cp + 1) * LANES] = piece.astype(BF16)


def _rpb_table(rpb):
    nq, nk = NBR_QROWS * GRID_W, NBR_SLAB * GRID_W
    return pl.pallas_call(
        _rpb_table_kernel,
        grid=(B_HEADS,),
        in_specs=[pl.BlockSpec(memory_space=pltpu.SMEM)],
        out_specs=pl.BlockSpec((NBR_CASES, 1, nq, nk), lambda h: (0, h, 0, 0)),
        out_shape=jax.ShapeDtypeStruct((NBR_CASES, B_HEADS, nq, nk), BF16),
        scratch_shapes=[pltpu.VMEM((N_PAIR, GRID_W, LANES), F32)],
        compiler_params=_cparams(("parallel",)),
        name="rpb_table",
    )(rpb.reshape(-1))


def _pair_attend(q, lane_hi, segs):
    m = q.shape[0]
    zero = jnp.zeros_like(q)
    q_stack = jnp.concatenate([jnp.where(lane_hi, zero, q), jnp.where(lane_hi, q, zero)], axis=0)
    o, l = _softmax_attend(q_stack, segs)
    o = o / l
    return jnp.where(lane_hi, o[m:], o[:m])


NBR_GROUPS_PER_STEP = 4


def _nbr_kernel(q_ref, k_ref, v_ref, kc_ref, vc_ref, bias_ref, o_ref):
    nq = NBR_QROWS * GRID_W
    nk = NBR_SLAB * GRID_W
    lane_hi = lax.broadcasted_iota(jnp.int32, (nq, LANES), 1) >= HEAD_DIM
    for j in range(NBR_GROUPS_PER_STEP):
        g = pl.program_id(1) * NBR_GROUPS_PER_STEP + j
        case = jnp.where(g == 0, 0, jnp.where(g == NBR_GROUPS - 1, 2, 1))
        r0 = jnp.clip(NBR_QROWS * g - B_WIN_ROWS // 2, 0, GRID_H - NBR_SLAB)
        base = pl.multiple_of(r0 * GRID_W, GRID_W)
        rows = slice(j * nq, (j + 1) * nq)
        for pr in range(B_HEADS // 2):
            sl = slice(pr * LANES, (pr + 1) * LANES)
            kp = k_ref[pl.ds(base, nk), sl]
            vp = v_ref[pl.ds(base, nk), sl]
            kcp = kc_ref[:, sl]
            vcp = vc_ref[:, sl]
            bias = bias_ref[case, 2 * pr:2 * pr + 2].reshape(2 * nq, nk).astype(F32)
            segs = [(kp, vp, bias), (kcp, vcp, None)]
            q = q_ref[rows, sl] * ATTN_SCALE
            o_ref[rows, sl] = _pair_attend(q, lane_hi, segs).astype(BF16)


def _nbr_attn(zh, table):
    nq = NBR_GROUPS_PER_STEP * NBR_QROWS * GRID_W
    steps = NBR_GROUPS // NBR_GROUPS_PER_STEP
    c0 = _ctx_blocks(CTX_LEN)
    return pl.pallas_call(
        _nbr_kernel,
        grid=(BATCH, steps),
        in_specs=[
            _zspec(H_BQ, nq, lambda b, s: b * steps + s),
            _zspec(H_BK, SEQ, lambda b, s: b),
            _zspec(H_BV, SEQ, lambda b, s: b),
            _zspec(H_BK, CTX_LEN, lambda b, s: c0 + b),
            _zspec(H_BV, CTX_LEN, lambda b, s: c0 + b),
            pl.BlockSpec(table.shape, lambda b, s: (0, 0, 0, 0), pipeline_mode=pl.Buffered(1)),
        ],
        out_specs=pl.BlockSpec((nq, Z_TILE), lambda b, s: (b * steps + s, 0)),
        out_shape=jax.ShapeDtypeStruct((LATENT_ROWS, BRANCH_W), BF16),
        compiler_params=_cparams(("parallel", "arbitrary")),
        name="nbr_attn",
    )(zh, zh, zh, zh, zh, table)


def _ctx_mha_kernel(q_ref, k_ref, v_ref, o_ref):
    q = q_ref[...] * ATTN_SCALE
    lane_hi = lax.broadcasted_iota(jnp.int32, (q.shape[0], LANES), 1) >= HEAD_DIM
    for pr in range(B_HEADS // 2):
        sl = slice(pr * LANES, (pr + 1) * LANES)
        segs = [(k_ref[:, sl], v_ref[:, sl], None)]
        o_ref[:, sl] = _pair_attend(q[:, sl], lane_hi, segs).astype(BF16)


def _ctx_mha(zh):
    c0 = _ctx_blocks(CTX_LEN)
    rows = lambda b: c0 + b
    return pl.pallas_call(
        _ctx_mha_kernel,
        grid=(BATCH,),
        in_specs=[_zspec(H_BQ, CTX_LEN, rows), _zspec(H_BK, CTX_LEN, rows), _zspec(H_BV, CTX_LEN, rows)],
        out_specs=pl.BlockSpec((CTX_LEN, Z_TILE), lambda b: (b, 0)),
        out_shape=jax.ShapeDtypeStruct((CTX_ROWS, BRANCH_W), BF16),
        compiler_params=_cparams(("parallel",)),
        name="nbr_attn_ctx",
    )(zh, zh, zh)


def _pool_kernel(u_ref, w_ref, scale_ref, o_ref):
    n = u_ref.shape[0]
    t = lax.broadcasted_iota(jnp.int32, (n, C_GROUP_DIM), 0)

    def down(a, k):
        return jnp.where(t >= k, pltpu.roll(a, k, axis=0), 0.0)

    def up(a, k):
        return jnp.where(t < n - k, pltpu.roll(a, n - k, axis=0), 0.0)

    for g, w in enumerate(C_WINDOWS):
        sl = slice(g * C_GROUP_DIM, (g + 1) * C_GROUP_DIM)
        u = u_ref[:, sl].astype(F32)
        half = w // 2
        back = u
        fwd = u
        k = 1
        while k < half:
            back = back + down(back, k)
            fwd = fwd + up(fwd, k)
            k *= 2
        total = down(back, 1) + fwd
        lo = jnp.maximum(t - half, 0)
        hi = jnp.minimum(t + half - 1, n - 1)
        cnt = (hi - lo + 1).astype(F32)
        pooled = total / cnt - u
        y = _dot(pooled.astype(BF16), w_ref[0, g]) * scale_ref[:, sl]
        o_ref[:, sl] = y.astype(BF16)


def _pool(zh, c_w, c_scale, *, layer, n, latent):
    rows = LATENT_ROWS if latent else CTX_ROWS
    b0 = 0 if latent else _ctx_blocks(n)
    return pl.pallas_call(
        _pool_kernel,
        grid=(rows // n,),
        in_specs=[
            _zspec(H_CU, n, lambda b: b0 + b),
            pl.BlockSpec((1, len(C_WINDOWS), C_GROUP_DIM, C_GROUP_DIM), lambda b: (layer, 0, 0, 0)),
            pl.BlockSpec((1, BRANCH_W), lambda b: (0, 0)),
        ],
        out_specs=pl.BlockSpec((n, BRANCH_W), lambda b: (b, 0)),
        out_shape=jax.ShapeDtypeStruct((rows, BRANCH_W), BF16),
        compiler_params=_cparams(("parallel",)),
        name=f"pool_{n}",
    )(zh, c_w, c_scale)


DIFF_ROWS = 256
DIFF_BLOCKS_PER_STEP = 2


def _diff_kernel(lam_ref, q_ref, *rest, lam_init, latent):
    if latent:
        k_ref, v_ref, kc_ref, vc_ref, g_ref, o_ref = rest
    else:
        kc_ref, vc_ref, g_ref, o_ref = rest
    hw = 2 * HEAD_DIM
    dl = lam_ref[...]
    lam = (jnp.exp(jnp.sum(dl[0:1] * dl[1:2], axis=-1, keepdims=True))
           - jnp.exp(jnp.sum(dl[2:3] * dl[3:4], axis=-1, keepdims=True)) + lam_init)
    tq = min(q_ref.shape[0], DIFF_ROWS)
    lane_hi = lax.broadcasted_iota(jnp.int32, (tq, hw), 1) >= HEAD_DIM
    for r0 in range(0, q_ref.shape[0], tq):
        rows = slice(r0, r0 + tq)
        for h in range(D_HEADS):
            sl = slice(h * hw, (h + 1) * hw)
            qh = q_ref[rows, sl] * ATTN_SCALE
            zero = jnp.zeros_like(qh)
            q_stack = jnp.concatenate([jnp.where(lane_hi, zero, qh), jnp.where(lane_hi, qh, zero)], axis=0)
            segs = [(kc_ref[:, sl], vc_ref[:, sl], None)]
            if latent:
                segs.append((k_ref[:, sl], v_ref[:, sl], None))
            o, l = _softmax_attend(q_stack, segs)
            o = o / l
            od = o[:tq] - lam * o[tq:]
            ms = jnp.mean(od * od, axis=-1, keepdims=True)
            y = od * lax.rsqrt(ms + EPS) * g_ref[...] * (1.0 - lam_init)
            o_ref[rows, sl] = y.astype(BF16)


def _diff_attn(zh, d_lambda, subln, *, lam_init, latent):
    full = lambda shape: pl.BlockSpec(shape, lambda *_: (0,) * len(shape))
    c0 = _ctx_blocks(CTX_LEN)
    if latent:
        tq = DIFF_BLOCKS_PER_STEP * DIFF_ROWS
        nq = SEQ // tq
        grid = (BATCH, nq)
        in_specs = [
            full((4, HEAD_DIM)),
            _zspec(H_DQ, tq, lambda b, n: b * nq + n),
            _zspec(H_DK, SEQ, lambda b, n: b),
            _zspec(H_DV, SEQ, lambda b, n: b),
            _zspec(H_DK, CTX_LEN, lambda b, n: c0 + b),
            _zspec(H_DV, CTX_LEN, lambda b, n: c0 + b),
            full((1, 2 * HEAD_DIM)),
        ]
        out_specs = pl.BlockSpec((tq, Z_TILE), lambda b, n: (b * nq + n, 0))
        args = (d_lambda, zh, zh, zh, zh, zh, subln)
        rows = LATENT_ROWS
        sem = ("parallel", "arbitrary")
    else:
        grid = (BATCH,)
        crow = lambda b: c0 + b
        in_specs = [
            full((4, HEAD_DIM)),
            _zspec(H_DQ, CTX_LEN, crow),
            _zspec(H_DK, CTX_LEN, crow),
            _zspec(H_DV, CTX_LEN, crow),
            full((1, 2 * HEAD_DIM)),
        ]
        out_specs = pl.BlockSpec((CTX_LEN, Z_TILE), lambda b: (b, 0))
        args = (d_lambda, zh, zh, zh, subln)
        rows = CTX_ROWS
        sem = ("parallel",)
    return pl.pallas_call(
        functools.partial(_diff_kernel, lam_init=lam_init, latent=latent),
        grid=grid,
        in_specs=in_specs,
        out_specs=out_specs,
        out_shape=jax.ShapeDtypeStruct((rows, BRANCH_W), BF16),
        compiler_params=_cparams(sem),
        name="diff_attn" if latent else "diff_attn_ctx",
    )(*args)


def _merge_kernel(*refs):
    y_refs = refs[:N_BRANCH]
    g_refs = refs[N_BRANCH:2 * N_BRANCH]
    wb_ref, wo_ref, x_ref, gt_ref, gn_ref, shift_ref, scale_ref, xo_ref, h_ref = refs[2 * N_BRANCH:]
    mixed = None
    for k in range(N_BRANCH):
        gate = 0.5 * (jnp.tanh(0.5 * g_refs[k][...].astype(F32)) + 1.0)
        proj = _dot(y_refs[k][...], wb_ref[0, k]) * gate
        mixed = proj if mixed is None else mixed + proj
    out = _dot(mixed.astype(BF16), wo_ref[0])
    xo_ref[...] = x_ref[...] + gt_ref[0] * out
    _norm_modulate_rows(xo_ref, gn_ref, shift_ref, scale_ref, h_ref)


def _merge(ys, gates, w_branch, w_out, x, gn2, mod3, *, tm, row0, mod_row):
    m = x.shape[0]
    row = lambda i: (i, 0)
    g0 = row0 // tm
    mod = lambda chunk: pl.BlockSpec((1, 1, D_MODEL), lambda i: (mod_row(i), 0, chunk))
    resident = pl.Buffered(1)
    return pl.pallas_call(
        _merge_kernel,
        grid=(m // tm,),
        in_specs=[pl.BlockSpec((tm, BRANCH_W), row)] * N_BRANCH
        + [pl.BlockSpec((tm, D_MODEL), lambda i, k=k: (g0 + i, k)) for k in range(N_BRANCH)]
        + [
            pl.BlockSpec((1, N_BRANCH, BRANCH_W, D_MODEL), lambda i: (0, 0, 0, 0), pipeline_mode=resident),
            pl.BlockSpec((1, D_MODEL, D_MODEL), lambda i: (0, 0, 0), pipeline_mode=resident),
            pl.BlockSpec((tm, D_MODEL), row),
            mod(2),
            pl.BlockSpec((1, D_MODEL), lambda i: (0, 0)),
            mod(3),
            mod(4),
        ],
        out_specs=[pl.BlockSpec((tm, D_MODEL), row), pl.BlockSpec((tm, D_MODEL), row)],
        out_shape=[jax.ShapeDtypeStruct((m, D_MODEL), F32), jax.ShapeDtypeStruct((m, D_MODEL), BF16)],
        compiler_params=_cparams(("parallel",)),
        name=f"merge_{m}",
    )(*ys, *([gates] * N_BRANCH), w_branch, w_out, x, mod3, gn2, mod3, mod3)


def _mlp_kernel(h_ref, x_ref, gt_ref, w1_ref, w2_ref, *rest, emit_next):
    if emit_next:
        gn_ref, shift_ref, scale_ref, xo_ref, hn_ref = rest
    else:
        (xo_ref,) = rest
    f = pl.program_id(1)

    @pl.when(f == 0)
    def _():
        xo_ref[...] = jnp.zeros_like(xo_ref)

    a = jnp.maximum(_dot(h_ref[...], w1_ref[0]), 0.0)
    xo_ref[...] += _dot((a * a).astype(BF16), w2_ref[0])

    @pl.when(f == pl.num_programs(1) - 1)
    def _():
        xo_ref[...] = x_ref[...] + gt_ref[0] * xo_ref[...]
        if emit_next:
            _norm_modulate_rows(xo_ref, gn_ref, shift_ref, scale_ref, hn_ref)


def _mlp(h, x, mod3, w1, w2, next_norm, *, tm, tf, mod_row):
    m = x.shape[0]
    emit_next = next_norm is not None
    row = lambda i, f: (i, 0)
    in_specs = [
        pl.BlockSpec((tm, D_MODEL), row),
        pl.BlockSpec((tm, D_MODEL), row),
        pl.BlockSpec((1, 1, D_MODEL), lambda i, f: (mod_row(i), 0, 5)),
        pl.BlockSpec((1, D_MODEL, tf), lambda i, f: (0, 0, f)),
        pl.BlockSpec((1, tf, D_MODEL), lambda i, f: (0, f, 0)),
    ]
    args = [h, x, mod3, w1, w2]
    out_specs = [pl.BlockSpec((tm, D_MODEL), row)]
    out_shape = [jax.ShapeDtypeStruct((m, D_MODEL), F32)]
    if emit_next:
        gn_next, mod3_next = next_norm
        in_specs += [
            pl.BlockSpec((1, D_MODEL), lambda i, f: (0, 0)),
            pl.BlockSpec((1, 1, D_MODEL), lambda i, f: (mod_row(i), 0, 0)),
            pl.BlockSpec((1, 1, D_MODEL), lambda i, f: (mod_row(i), 0, 1)),
        ]
        args += [gn_next, mod3_next, mod3_next]
        out_specs.append(pl.BlockSpec((tm, D_MODEL), row))
        out_shape.append(jax.ShapeDtypeStruct((m, D_MODEL), BF16))
    outs = pl.pallas_call(
        functools.partial(_mlp_kernel, emit_next=emit_next),
        grid=(m // tm, D_FF // tf),
        in_specs=in_specs,
        out_specs=out_specs,
        out_shape=out_shape,
        compiler_params=_cparams(("parallel", "arbitrary")),
        name=f"mlp_{m}",
    )(*args)
    return outs if emit_next else (outs[0], None)


def _rope_tables():
    t = jnp.arange(SEQ)
    row = (t // GRID_W).astype(F32)
    col = (t % GRID_W).astype(F32)
    n_freq = HEAD_DIM // 4
    inv = ROPE_BASE ** (-jnp.arange(n_freq, dtype=F32) / n_freq)
    ar = row[:, None] * inv
    ac = col[:, None] * inv
    cos_h = jnp.concatenate([jnp.cos(ar), jnp.cos(ar), jnp.cos(ac), jnp.cos(ac)], axis=-1)
    sin_h = jnp.concatenate([-jnp.sin(ar), jnp.sin(ar), -jnp.sin(ac), jnp.sin(ac)], axis=-1)
    reps = LANES // HEAD_DIM
    cos_t = jnp.concatenate([jnp.tile(cos_h, (1, reps)), jnp.ones((TM_PROJ, LANES), F32)], axis=0)
    sin_t = jnp.concatenate([jnp.tile(sin_h, (1, reps)), jnp.zeros((TM_PROJ, LANES), F32)], axis=0)
    return cos_t, sin_t


def _head_gain_row(qk_gain_l):
    ones = jnp.ones((Z_TILE,), F32)
    rep = lambda g: jnp.tile(g, Z_TILE // HEAD_DIM)
    akv = jnp.concatenate([jnp.tile(qk_gain_l[0, 1], A_KV_HEADS), jnp.ones((Z_TILE - A_KV_HEADS * HEAD_DIM,), F32)])
    return jnp.concatenate([
        rep(qk_gain_l[0, 0]), akv,
        rep(qk_gain_l[1, 0]), rep(qk_gain_l[1, 1]), ones, ones,
        rep(qk_gain_l[2, 0]), rep(qk_gain_l[2, 1]), ones,
    ])[None, :]


def kernel(x, c, ctx, c_ctx, w_ada, b_ada, g_norm1, g_norm2, w_in, b_gate, qk_gain, a_sink, b_rpb,
           c_w, c_scale, d_lambda, d_subln, w_branch, w_out, w_ff1, w_ff2):
    cos_t, sin_t = _rope_tables()
    lane = jnp.arange(HEAD_SUM_LANES)
    bd = (lane[:, None] // HEAD_DIM == lane[None, :] // HEAD_DIM).astype(BF16)

    cc = jnp.concatenate([c, c_ctx[None, :], jnp.zeros((8 - BATCH - 1, D_MODEL), F32)], axis=0)
    mod_all = _modulation(cc, w_ada, b_ada)

    cw = c_w.astype(BF16)
    w_branch2d = w_branch.reshape(DEPTH, N_BRANCH * BRANCH_W, D_MODEL)

    xl = x.reshape(LATENT_ROWS, D_MODEL)
    xc = ctx.reshape(CTX_ROWS, D_MODEL)
    ctx_row = lambda i: BATCH
    batch_row = lambda tm: (lambda i: i // (SEQ // tm))
    mods = [mod_all[l].reshape(8, 1, 6 * D_MODEL) for l in range(DEPTH)]
    gn1 = [g_norm1[l][None, :] for l in range(DEPTH)]

    tm_merge, tm_mlp, tf_mlp, tn_gate = TM_MERGE, TM_MLP, TF_MLP, TN_GATE
    hl = _norm_mod(xl, gn1[0], mods[0], tm=tm_mlp, mod_row=batch_row(tm_mlp))
    hc = _norm_mod(xc, gn1[0], mods[0], tm=tm_mlp, mod_row=ctx_row)

    for l in range(DEPTH):
        last = l == DEPTH - 1
        lam_init = 0.8 - 0.6 * math.exp(-0.3 * l)
        mod3 = mods[l]
        gn2 = g_norm2[l][None, :]
        cs = c_scale[l][None, :]
        subln = d_subln[l][None, :]
        next_norm = None if last else (gn1[l + 1], mods[l + 1])

        gates, w1, w2 = _gate_proj(hl, None if last else hc, w_in, b_gate[l][None, :], w_ff1, w_ff2,
                                   layer=l, tn=tn_gate)
        zh, wb, wo = _head_proj(hl, hc, w_in, _head_gain_row(qk_gain[l]), cos_t, sin_t, bd, w_branch2d, w_out,
                                layer=l, ctx_queries=not last)
        wb = wb.reshape(1, N_BRANCH, BRANCH_W, D_MODEL)

        table = _rpb_table(b_rpb[l])
        ya = _win_gqa(zh, a_sink[l], latent=True)
        yb = _nbr_attn(zh, table)
        yc = _pool(zh, cw, cs, layer=l, n=SEQ, latent=True)
        yd = _diff_attn(zh, d_lambda[l], subln, lam_init=lam_init, latent=True)
        xl, h2 = _merge((ya, yb, yc, yd), gates, wb, wo, xl, gn2, mod3,
                        tm=tm_merge, row0=0, mod_row=batch_row(tm_merge))
        xl, hl = _mlp(h2, xl, mod3, w1, w2, next_norm, tm=tm_mlp, tf=tf_mlp, mod_row=batch_row(tm_mlp))
        if not last:
            yac = _win_gqa(zh, a_sink[l], latent=False)
            ybc = _ctx_mha(zh)
            ycc = _pool(zh, cw, cs, layer=l, n=CTX_LEN, latent=False)
            ydc = _diff_attn(zh, d_lambda[l], subln, lam_init=lam_init, latent=False)
            xc, h2c = _merge((yac, ybc, ycc, ydc), gates, wb, wo, xc, gn2, mod3,
                             tm=tm_merge, row0=LATENT_ROWS, mod_row=ctx_row)
            xc, hc = _mlp(h2c, xc, mod3, w1, w2, next_norm, tm=tm_mlp, tf=tf_mlp, mod_row=ctx_row)

    return xl.reshape(BATCH, SEQ, D_MODEL)
```
